```python
import math
import jax
import jax.numpy as jnp
from jax import lax
import numpy as np

D_MODEL = 1024
BATCH = 8
SEQ = 4096
DEPTH = 4

CTX_LEN = 256
GRID_W = 64

HG_WIDTH = D_MODEL // 4
HG_HEADS = 4
HG_DK = HG_WIDTH // HG_HEADS
HG_DV = HG_WIDTH // HG_HEADS
HG_CHUNK = 64
HG_COLS = 5 * HG_WIDTH

DA_WIDTH = D_MODEL // 2
DA_HEADS = 4
DA_DV = DA_WIDTH // DA_HEADS
DA_DQK = DA_DV // 2
DA_COLS = 3 * DA_WIDTH
Q_BLOCK = 128
ROPE_BASE = 10000.0

RW_WIDTH = D_MODEL - HG_WIDTH - DA_WIDTH
RW_HEADS = 4
RW_DH = RW_WIDTH // RW_HEADS
RW_DECAY_RANK = 32
RW_ICLR_RANK = 32
RW_GATE_RANK = 64
RW_COLS = 3 * RW_WIDTH + 2 * RW_DECAY_RANK + 2 * RW_ICLR_RANK + RW_GATE_RANK
RW_GN_EPS = 64e-5

MIX_WIDTH = HG_WIDTH + DA_WIDTH + RW_WIDTH
IN_COLS = HG_COLS + DA_COLS + RW_COLS
D_FF = -(-8 * D_MODEL // (3 * 256)) * 256

kernel_name = 'hybrid_hgrn2_diffattn_rwkv7_dit_block'

F32 = jnp.float32


def rms_norm(x, gain, eps=1e-6):
    xf = x.astype(F32)
    y = xf * lax.rsqrt(jnp.mean(xf * xf, axis=-1, keepdims=True) + eps)
    return (y * gain.astype(F32)).astype(x.dtype)


def modulate(h, shift, scale):
    return h * (1.0 + scale) + shift


def swiglu(h, w_gate, w_up, w_down):
    return (jax.nn.silu(h @ w_gate) * (h @ w_up)) @ w_down


def direction_order(t, n_ctx, reverse):
    if not reverse:
        return t
    return jnp.concatenate([t[:, :n_ctx][:, ::-1], t[:, n_ctx:][:, ::-1]], axis=1)


def centred_token_shift(z, mu_prev, mu_next):
    z_prev = jnp.pad(z, ((0, 0), (1, 0), (0, 0)))[:, :-1]
    z_next = jnp.pad(z, ((0, 0), (0, 1), (0, 0)))[:, 1:]
    return z + mu_prev * (z_prev - z) + mu_next * (z_next - z)


def axial_rope(n_tokens, dim):
    n_rows = n_tokens // GRID_W
    rows = jnp.repeat(jnp.arange(n_rows, dtype=F32), GRID_W)
    cols = jnp.tile(jnp.arange(GRID_W, dtype=F32), n_rows)
    n_freq = dim // 4
    inv_freq = ROPE_BASE ** (-jnp.arange(n_freq, dtype=F32) / n_freq)
    ang = jnp.concatenate([rows[:, None] * inv_freq, cols[:, None] * inv_freq], axis=-1)
    return jnp.cos(ang), jnp.sin(ang)


def apply_rope(t, cos, sin):
    half = t.shape[-1] // 2
    shape = (1, cos.shape[0]) + (1,) * (t.ndim - 3) + (half,)
    cos = cos.reshape(shape).astype(t.dtype)
    sin = sin.reshape(shape).astype(t.dtype)
    t1, t2 = t[..., :half], t[..., half:]
    return jnp.concatenate([t1 * cos - t2 * sin, t1 * sin + t2 * cos], axis=-1)


def head_group_norm(y, w, b, eps):
    mean = jnp.mean(y, axis=-1, keepdims=True)
    var = jnp.mean(jnp.square(y - mean), axis=-1, keepdims=True)
    yn = (y - mean) * lax.rsqrt(var + eps)
    return yn.reshape(y.shape[0], y.shape[1], -1) * w + b


def hgrn2_chunk_scan(q, k, v, log_f):
    B, N, H, Dk = q.shape
    Dv = v.shape[-1]
    nc = N // HG_CHUNK

    def chunks(t):
        return jnp.moveaxis(t.reshape(B, nc, HG_CHUNK, H, t.shape[-1]), (1, 3), (0, 2))

    prefix = jnp.tril(jnp.ones((HG_CHUNK, HG_CHUNK), dtype=bool))[:, :, None]

    def step(S, inp):
        q_c, k_c, v_c, g_c = inp
        b = jnp.cumsum(g_c, axis=2)
        rel = jnp.where(prefix, b[:, :, :, None, :] - b[:, :, None, :, :], -jnp.inf)
        att = jnp.einsum('bhtk,bhsk,bhtsk->bhts', q_c, k_c, jnp.exp(rel))
        o = (jnp.einsum('bhts,bhsv->bhtv', att, v_c)
             + jnp.einsum('bhtk,bhkv->bhtv', q_c * jnp.exp(b), S))
        b_end = b[:, :, -1:, :]
        S = (S * jnp.exp(b_end)[:, :, 0, :, None]
             + jnp.einsum('bhsk,bhsv->bhkv', k_c * jnp.exp(b_end - b), v_c))
        return S, o

    S0 = jnp.zeros((B, H, Dk, Dv), F32)
    _, o = lax.scan(step, S0, tuple(chunks(t) for t in (q, k, v, log_f)))
    return jnp.moveaxis(o, (0, 2), (1, 3)).reshape(B, N, H, Dv)


def rwkv7_scan(r, w, k, v, a, b):
    B, N, H, Dh = r.shape

    def step(S, inp):
        r_t, w_t, k_t, v_t, a_t, b_t = inp
        sa = jnp.einsum('bhvk,bhk->bhv', S, a_t)
        S = (S * w_t[:, :, None, :] + sa[..., None] * b_t[:, :, None, :]
             + v_t[..., None] * k_t[:, :, None, :])
        return S, jnp.einsum('bhvk,bhk->bhv', S, r_t)

    xs = tuple(jnp.moveaxis(t, 1, 0) for t in (r, w, k, v, a, b))
    _, y = lax.scan(step, jnp.zeros((B, H, Dh, Dh), F32), xs)
    return jnp.moveaxis(y, 0, 1)


def hgrn2_mixer(uc, ux, lower_bound, norm_g):
    n_ctx = uc.shape[1]
    u = jnp.concatenate([uc, ux], axis=1).astype(F32)
    B, N, _ = u.shape
    heads = lambda t: t.reshape(B, N, HG_HEADS, -1)
    q, inp, f_fwd, f_bwd, g = (heads(t) for t in jnp.split(u, 5, axis=-1))
    outs = []
    for f_logit, lb, rev in ((f_fwd, lower_bound[0], False), (f_bwd, lower_bound[1], True)):
        lb = lb.reshape(HG_HEADS, HG_DK).astype(F32)
        f = lb + (1.0 - lb) * jax.nn.sigmoid(f_logit)
        o = hgrn2_chunk_scan(*(direction_order(t, n_ctx, rev) for t in (q, 1.0 - f, inp, jnp.log(f))))
        outs.append(direction_order(o, n_ctx, rev))
    o = rms_norm(outs[0] + outs[1], norm_g) * jax.nn.silu(g)
    o = o.reshape(B, N, HG_WIDTH).astype(ux.dtype)
    return o[:, :n_ctx], o[:, n_ctx:]


def diff_attention_mixer(uc, ux, lam, lam_init, norm_g, cos, sin, with_ctx_queries):
    B, L, _ = uc.shape
    T = ux.shape[1]
    scale = DA_DQK ** -0.5

    def split_heads(u):
        n = u.shape[1]
        q, k, v = jnp.split(u, 3, axis=-1)
        return (q.reshape(B, n, DA_HEADS, 2, DA_DQK), k.reshape(B, n, DA_HEADS, 2, DA_DQK),
                v.reshape(B, n, DA_HEADS, DA_DV))

    qc, kc, vc = split_heads(uc)
    qx, kx, vx = split_heads(ux)
    qx, kx = apply_rope(qx, cos, sin), apply_rope(kx, cos, sin)
    hm = lambda t: jnp.moveaxis(t, 1, 2)
    k_all = hm(jnp.concatenate([kc, kx], axis=1))
    v_all = hm(jnp.concatenate([vc, vx], axis=1))

    def attend(q, k, v):
        s = jnp.einsum('bhqmd,bhsmd->bhmqs', q, k).astype(F32) * scale
        p = jax.nn.softmax(s, axis=-1)
        p = p[:, :, 0] - lam * p[:, :, 1]
        return jnp.einsum('bhqs,bhsv->bhqv', p.astype(v.dtype), v)

    def finish(o):
        o = rms_norm(o, norm_g) * (1.0 - lam_init)
        return hm(o).reshape(B, o.shape[2], DA_WIDTH)

    nb = T // Q_BLOCK
    qb = jnp.moveaxis(hm(qx).reshape(B, DA_HEADS, nb, Q_BLOCK, 2, DA_DQK), 2, 0)
    ox = lax.map(lambda qq: attend(qq, k_all, v_all), qb)
    ox = jnp.moveaxis(ox, 0, 2).reshape(B, DA_HEADS, T, DA_DV)
    oc = finish(attend(hm(qc), hm(kc), hm(vc))) if with_ctx_queries else None
    return oc, finish(ox)


def rwkv7_mixer(uc, ux, mu_prev, mu_next, w0, w2, a0, a2, g2, k_k, k_a, r_k, ln_w, ln_b):
    n_ctx = uc.shape[1]
    u = jnp.concatenate([centred_token_shift(uc, mu_prev, mu_next),
                         centred_token_shift(ux, mu_prev, mu_next)], axis=1).astype(F32)
    B, N, _ = u.shape
    W = RW_WIDTH
    sizes = (W, W, W, RW_DECAY_RANK, RW_DECAY_RANK, RW_ICLR_RANK, RW_ICLR_RANK, RW_GATE_RANK)
    r, k, v, wd_f, wd_b, ad_f, ad_b, gd = jnp.split(u, np.cumsum(sizes)[:-1].tolist(), axis=-1)
    heads = lambda t: t.reshape(B, N, RW_HEADS, RW_DH)
    gate = jax.nn.sigmoid(gd) @ g2
    kk = heads(k * k_k)
    kk = kk * lax.rsqrt(jnp.maximum(jnp.sum(kk * kk, axis=-1, keepdims=True), 1e-24))
    rh, vh = heads(r), heads(v)
    r_k = r_k.reshape(RW_HEADS, RW_DH)
    ys, bonuses = [], []
    for d, (wd, ad, rev) in enumerate(((wd_f, ad_f, False), (wd_b, ad_b, True))):
        w_log = -jax.nn.softplus(-(w0[d] + jnp.tanh(wd) @ w2[d])) - 0.5
        decay = heads(jnp.exp(-jnp.exp(w_log)))
        a = jax.nn.sigmoid(a0[d] + ad @ a2[d])
        kd = heads(k * (1.0 + (a - 1.0) * k_a))
        a = heads(a)
        y = rwkv7_scan(*(direction_order(t, n_ctx, rev) for t in (rh, decay, kd, vh, -kk, kk * a)))
        ys.append(direction_order(y, n_ctx, rev))
        bonuses.append(jnp.sum(rh * kd * r_k, axis=-1, keepdims=True) * vh)
    y = head_group_norm(ys[0] + ys[1], ln_w, ln_b, RW_GN_EPS) + (bonuses[0] + bonuses[1]).reshape(B, N, W)
    o = (y * gate).astype(ux.dtype)
    return o[:, :n_ctx], o[:, n_ctx:]


def setup_inputs(seed: int = 0) -> dict:
    key = jax.random.key(seed)
    keys = list(jax.random.split(key, 40))
    D = D_MODEL

    def nrm(shape, s):
        return jax.random.normal(keys.pop(), shape, F32) * s

    def uni(shape, s):
        return jax.random.uniform(keys.pop(), shape, F32) * s

    return {
        'x': nrm((BATCH, SEQ, D), 1.0),
        'c': nrm((BATCH, D), 1.0),
        'ctx': nrm((BATCH, CTX_LEN, D), 1.0),
        'c_ctx': nrm((D,), 1.0),
        'ada_w': nrm((DEPTH, D, 6 * D), 0.5 * D ** -0.5),
        'ada_b': nrm((DEPTH, 6 * D), 0.02),
        'norm1_g': 1.0 + nrm((DEPTH, D), 0.02),
        'norm2_g': 1.0 + nrm((DEPTH, D), 0.02),
        'w_in': nrm((DEPTH, D, IN_COLS), D ** -0.5),
        'w_out': nrm((DEPTH, MIX_WIDTH, D), MIX_WIDTH ** -0.5),
        'hg_lb_logits': nrm((2, DEPTH, HG_HEADS * HG_DK), 0.5),
        'hg_norm_g': 1.0 + nrm((DEPTH, HG_DV), 0.02),
        'da_lam_q1': nrm((DEPTH, DA_DQK), 0.1),
        'da_lam_k1': nrm((DEPTH, DA_DQK), 0.1),
        'da_lam_q2': nrm((DEPTH, DA_DQK), 0.1),
        'da_lam_k2': nrm((DEPTH, DA_DQK), 0.1),
        'da_norm_g': 1.0 + nrm((DEPTH, DA_DV), 0.02),
        'rw_mu_prev': uni((DEPTH, RW_COLS), 0.5),
        'rw_mu_next': uni((DEPTH, RW_COLS), 0.5),
        'rw_w0': nrm((2, DEPTH, RW_WIDTH), 0.5),
        'rw_w2': nrm((2, DEPTH, RW_DECAY_RANK, RW_WIDTH), 0.1),
        'rw_a0': nrm((2, DEPTH, RW_WIDTH), 0.1),
        'rw_a2': nrm((2, DEPTH, RW_ICLR_RANK, RW_WIDTH), 0.5 * RW_ICLR_RANK ** -0.5),
        'rw_g2': nrm((DEPTH, RW_GATE_RANK, RW_WIDTH), RW_GATE_RANK ** -0.5),
        'rw_k_k': 0.85 + nrm((DEPTH, RW_WIDTH), 0.05),
        'rw_k_a': 1.0 + nrm((DEPTH, RW_WIDTH), 0.05),
        'rw_r_k': nrm((DEPTH, RW_WIDTH), 0.1),
        'rw_ln_w': 1.0 + nrm((DEPTH, RW_WIDTH), 0.02),
        'rw_ln_b': nrm((DEPTH, RW_WIDTH), 0.02),
        'ffn_w_gate': nrm((DEPTH, D, D_FF), D ** -0.5),
        'ffn_w_up': nrm((DEPTH, D, D_FF), D ** -0.5),
        'ffn_w_down': nrm((DEPTH, D_FF, D), D_FF ** -0.5),
        'final_norm_g': 1.0 + nrm((D,), 0.02),
    }


def reference(x, c, ctx, c_ctx, ada_w, ada_b, norm1_g, norm2_g, w_in, w_out,
              hg_lb_logits, hg_norm_g, da_lam_q1, da_lam_k1, da_lam_q2, da_lam_k2, da_norm_g,
              rw_mu_prev, rw_mu_next, rw_w0, rw_w2, rw_a0, rw_a2, rw_g2, rw_k_k, rw_k_a,
              rw_r_k, rw_ln_w, rw_ln_b, ffn_w_gate, ffn_w_up, ffn_w_down, final_norm_g):
    T = x.shape[1]
    cos, sin = axial_rope(T, DA_DQK)
    p_lb = jax.nn.softmax(hg_lb_logits.astype(F32), axis=1)
    lower_bounds = jnp.cumsum(p_lb, axis=1) - p_lb[:, :1]
    silu_c = jax.nn.silu(c)
    silu_cc = jax.nn.silu(c_ctx)
    for l in range(DEPTH):
        last = l == DEPTH - 1
        mod_x = (silu_c @ ada_w[l] + ada_b[l])[:, None, :]
        mod_c = silu_cc @ ada_w[l] + ada_b[l]
        sh1x, sc1x, g1x, sh2x, sc2x, g2x = jnp.split(mod_x, 6, axis=-1)
        sh1c, sc1c, g1c, sh2c, sc2c, g2c = jnp.split(mod_c, 6, axis=-1)

        hx = modulate(rms_norm(x, norm1_g[l]), sh1x, sc1x)
        hc = modulate(rms_norm(ctx, norm1_g[l]), sh1c, sc1c)
        ux = hx @ w_in[l]
        uc = hc @ w_in[l]
        cut = [HG_COLS, HG_COLS + DA_COLS]
        ux_hg, ux_da, ux_rw = jnp.split(ux, cut, axis=-1)
        uc_hg, uc_da, uc_rw = jnp.split(uc, cut, axis=-1)

        hg_c, hg_x = hgrn2_mixer(uc_hg, ux_hg, lower_bounds[:, l], hg_norm_g[l])

        lam_init = 0.8 - 0.6 * math.exp(-0.3 * l)
        lam = (jnp.exp(jnp.sum(da_lam_q1[l] * da_lam_k1[l]))
               - jnp.exp(jnp.sum(da_lam_q2[l] * da_lam_k2[l])) + lam_init).astype(F32)
        da_c, da_x = diff_attention_mixer(uc_da, ux_da, lam, lam_init, da_norm_g[l], cos, sin,
                                          not last)

        rw_c, rw_x = rwkv7_mixer(uc_rw, ux_rw, rw_mu_prev[l], rw_mu_next[l], rw_w0[:, l], rw_w2[:, l],
                                 rw_a0[:, l], rw_a2[:, l], rw_g2[l], rw_k_k[l], rw_k_a[l], rw_r_k[l],
                                 rw_ln_w[l], rw_ln_b[l])

        x = x + g1x * (jnp.concatenate([hg_x, da_x, rw_x], axis=-1) @ w_out[l])
        hx2 = modulate(rms_norm(x, norm2_g[l]), sh2x, sc2x)
        x = x + g2x * swiglu(hx2, ffn_w_gate[l], ffn_w_up[l], ffn_w_down[l])

        if not last:
            ctx = ctx + g1c * (jnp.concatenate([hg_c, da_c, rw_c], axis=-1) @ w_out[l])
            hc2 = modulate(rms_norm(ctx, norm2_g[l]), sh2c, sc2c)
            ctx = ctx + g2c * swiglu(hc2, ffn_w_gate[l], ffn_w_up[l], ffn_w_down[l])
    return rms_norm(x, final_norm_g)
```

```python
import functools
import math

import numpy as np
import jax
import jax.numpy as jnp
from jax import lax
from jax.experimental import pallas as pl
from jax.experimental.pallas import tpu as pltpu

F32 = jnp.float32
BF16 = jnp.bfloat16
HI = lax.Precision.HIGHEST

N_HEADS = 4
GRID_W = 64
ROPE_BASE = 10000.0
RMS_EPS = 1e-6
RW_GN_EPS = 64e-5

ROW_TILE = 256
CHUNK = 64
LANE = 128
VMEM_LIMIT = 56 * 1024 * 1024


def _cparams(sem):
    return pltpu.CompilerParams(dimension_semantics=sem, vmem_limit_bytes=VMEM_LIMIT)


def _dot(a, b, prec=None):
    return jnp.dot(a, b, preferred_element_type=F32, precision=prec)


def _dot_nt(a, b, prec=None):
    return lax.dot_general(a, b, (((1,), (1,)), ((), ())), preferred_element_type=F32, precision=prec)


def _dot_tn(a, b, prec=None):
    return lax.dot_general(a, b, (((0,), (0,)), ((), ())), preferred_element_type=F32, precision=prec)


def _iota(shape, dim):
    return lax.broadcasted_iota(jnp.int32, shape, dim)


def _div(x, d):
    assert d & (d - 1) == 0
    return x >> (d.bit_length() - 1)


def _mod(x, d):
    assert d & (d - 1) == 0
    return x & (d - 1)


def _softplus(x):
    return jnp.maximum(x, 0.0) + jnp.log(1.0 + jnp.exp(-jnp.abs(x)))


def _same_head(w, dh):
    return _div(_iota((w, w), 0), dh) == _div(_iota((w, w), 1), dh)


def _expand_heads(x, dh):
    w = x.shape[-1]
    head = _div(_iota((1, w), 1), dh)
    return jnp.concatenate([jnp.where(head == h, x, 0.0) for h in range(w // dh)], axis=0)


def _collapse_heads(x, c):
    out = x[0:c]
    for h in range(1, x.shape[0] // c):
        out = out + x[h * c:(h + 1) * c]
    return out


def _head_sum(x, dh):
    w = x.shape[-1]
    ones_bd = _same_head(w, dh).astype(F32)
    return _dot(x, ones_bd, HI)


def _sigmoid(x):
    return 1.0 / (1.0 + jnp.exp(-x))


def _ada_kernel(c_ref, w_ref, b_ref, o_ref):
    c = c_ref[...]
    sc = c * _sigmoid(c)
    o_ref[...] = _dot(sc, w_ref[...], HI) + b_ref[...]


def _ada_mod(cond, ada_w, ada_b):
    depth, d, d6 = ada_w.shape
    rows = cond.shape[0]
    tn = 1536
    return pl.pallas_call(
        _ada_kernel,
        grid=(depth, d6 // tn),
        in_specs=[pl.BlockSpec((rows, d), lambda l, j: (0, 0)),
                  pl.BlockSpec((None, d, tn), lambda l, j: (l, 0, j)),
                  pl.BlockSpec((None, 1, tn), lambda l, j: (l, 0, j))],
        out_specs=pl.BlockSpec((None, rows, tn), lambda l, j: (l, 0, j)),
        out_shape=jax.ShapeDtypeStruct((depth, rows, d6), F32),
        compiler_params=_cparams(("arbitrary", "arbitrary")),
        name="ada_mod",
    )(cond, ada_w, ada_b.reshape(depth, 1, d6))


def _rope_cols(t, cos, sin_signed, half):
    lane = _iota((1, t.shape[-1]), 1)
    first = _mod(lane, 2 * half) < half
    swapped = jnp.where(first, pltpu.roll(t, t.shape[-1] - half, 1), pltpu.roll(t, half, 1))
    return t * cos + swapped * sin_signed


def _in_proj_kernel(s_ref, mod_ref, g_ref, cos_ref, sin_ref, whg_ref, wda_ref, wrw_ref,
                    uhg_ref, uda_ref, urw_ref, *, da_width, q_scale):
    x = s_ref[...]
    ms = jnp.mean(x * x, axis=-1, keepdims=True)
    y = x * lax.rsqrt(ms + RMS_EPS) * g_ref[...]
    h = (y * (1.0 + mod_ref[1:2, :]) + mod_ref[0:1, :]).astype(BF16)
    uhg_ref[...] = _dot(h, whg_ref[...])
    urw_ref[...] = _dot(h, wrw_ref[...])
    uda = _dot(h, wda_ref[...])
    cos = cos_ref[...]
    sin = sin_ref[...]
    dqk = da_width // (2 * N_HEADS)
    for j in range(2 * da_width // LANE):
        t = _rope_cols(uda[:, j * LANE:(j + 1) * LANE], cos, sin, dqk // 2)
        if j < da_width // LANE:
            t = t * q_scale
        uda_ref[:, j * LANE:(j + 1) * LANE] = t.astype(BF16)
    uda_ref[:, 2 * da_width:] = uda[:, 2 * da_width:].astype(BF16)


def _in_proj(s, mods_l, g, cos_t, sin_t, whg, wda, wrw, n_ctx_tiles):
    b, n, d = s.shape
    tm = ROW_TILE
    da_width = wda.shape[1] // 3
    dqk = da_width // (2 * N_HEADS)
    sel = lambda bi, i: (jnp.where(i < n_ctx_tiles, b, bi), 0, 0)
    full = lambda a: pl.BlockSpec(a.shape, lambda bi, i: (0, 0))
    row = lambda w: pl.BlockSpec((None, tm, w), lambda bi, i: (bi, i, 0))
    kern = functools.partial(_in_proj_kernel, da_width=da_width, q_scale=dqk ** -0.5)
    return pl.pallas_call(
        kern,
        grid=(b, n // tm),
        in_specs=[row(d), pl.BlockSpec((None, 8, d), sel), full(g),
                  pl.BlockSpec((tm, LANE), lambda bi, i: (i, 0)),
                  pl.BlockSpec((tm, LANE), lambda bi, i: (i, 0)),
                  full(whg), full(wda), full(wrw)],
        out_specs=[row(whg.shape[1]), row(wda.shape[1]), row(wrw.shape[1])],
        out_shape=[jax.ShapeDtypeStruct((b, n, whg.shape[1]), F32),
                   jax.ShapeDtypeStruct((b, n, wda.shape[1]), BF16),
                   jax.ShapeDtypeStruct((b, n, wrw.shape[1]), F32)],
        compiler_params=_cparams(("parallel", "parallel")),
        name="in_proj",
    )(s, mods_l, g, cos_t, sin_t, whg, wda, wrw)


def _chunk_of_step(j, n_chunks, n_ctx_chunks, rev):
    if not rev:
        return j
    return jnp.where(j < n_ctx_chunks, n_ctx_chunks - 1 - j, n_chunks - 1 - (j - n_ctx_chunks))


def _hgrn_levels(c):
    return [1 << i for i in range(int(math.log2(c)))]


def _hgrn_const(c, rev):
    t = np.arange(c)[:, None]
    s = np.arange(c)[None, :]
    tri = (s >= t) if rev else (s <= t)
    mats = [tri.astype(np.float32)]
    for m in _hgrn_levels(c):
        p = (t // (2 * m)) * (2 * m) + (m if rev else m - 1)
        ref = (s >= p) if rev else (s <= p)
        mats.append(tri.astype(np.float32) - ref.astype(np.float32))
    return np.concatenate(mats, axis=0)


def _hgrn_kernel(*refs, rev, finish, dh):
    if finish:
        const_ref, q_ref, i_ref, f_ref, lb_ref, g_ref, other_ref, ng_ref, out_ref, st_ref = refs
    else:
        const_ref, q_ref, i_ref, f_ref, lb_ref, out_ref, st_ref = refs
    c, w = q_ref.shape

    @pl.when(pl.program_id(1) == 0)
    def _():
        st_ref[...] = jnp.zeros_like(st_ref)

    q = q_ref[...]
    v = i_ref[...]
    lb = lb_ref[...]
    f = lb + (1.0 - lb) * _sigmoid(f_ref[...])
    kx = 1.0 - f
    seg = _dot(const_ref[...], jnp.log(f), HI)
    bcum = seg[0:c]
    bend = bcum[0:1] if rev else bcum[c - 1:c]

    nh = w // dh
    t_idx = _iota((c, 1), 0)
    s_idx = _mod(_iota((1, nh * c), 1), c)
    att = jnp.zeros((c, nh * c), F32)
    for li, m in enumerate(_hgrn_levels(c)):
        e = jnp.exp(-jnp.abs(seg[(li + 1) * c:(li + 2) * c]))
        t_up = (t_idx & m) != 0
        s_up = (s_idx & m) != 0
        t_isq = jnp.logical_not(t_up) if rev else t_up
        s_isk = s_up if rev else jnp.logical_not(s_up)
        xm = jnp.where(t_isq, q, kx) * e
        sc = _dot_nt(xm, _expand_heads(xm, dh), HI)
        valid = (_div(t_idx, 2 * m) == _div(s_idx, 2 * m)) & t_isq & s_isk
        att = att + jnp.where(valid, sc, 0.0)
    o = _dot(att, _expand_heads(v, dh), HI)
    o = o + _head_sum(q * kx, dh) * v
    st = st_ref[...]
    o = o + _dot_nt(q * jnp.exp(bcum), st, HI)
    upd = _dot_tn(v, kx * jnp.exp(bend - bcum), HI)
    bd = _same_head(w, dh)
    st_ref[...] = st * jnp.exp(bend) + jnp.where(bd, upd, 0.0)

    if finish:
        o = o + other_ref[...]
        ms = _head_sum(o * o, dh) * (1.0 / dh)
        g = g_ref[...]
        out_ref[...] = o * lax.rsqrt(ms + RMS_EPS) * ng_ref[...] * (g * _sigmoid(g))
    else:
        out_ref[...] = o


def _hgrn_pass(u_hg, lb_row, n_ctx_chunks, rev, other=None, norm_g_row=None):
    b, n, _ = u_hg.shape
    c = CHUNK
    w = lb_row.shape[-1]
    dh = w // N_HEADS
    nc = n // c
    finish = other is not None
    const = jnp.asarray(_hgrn_const(c, rev))
    chunk = lambda j: _chunk_of_step(j, nc, n_ctx_chunks, rev)
    col = lambda k: pl.BlockSpec((None, c, w), lambda bi, j: (bi, chunk(j), k))
    vec = pl.BlockSpec((1, w), lambda bi, j: (0, 0))
    in_specs = [pl.BlockSpec(const.shape, lambda bi, j: (0, 0)), col(0), col(1), col(3 if rev else 2), vec]
    args = [const, u_hg, u_hg, u_hg, lb_row]
    if finish:
        in_specs += [col(4), col(0), vec]
        args += [u_hg, other, norm_g_row]
    return pl.pallas_call(
        functools.partial(_hgrn_kernel, rev=rev, finish=finish, dh=dh),
        grid=(b, nc),
        in_specs=in_specs,
        out_specs=col(0),
        out_shape=jax.ShapeDtypeStruct((b, n, w), F32),
        scratch_shapes=[pltpu.VMEM((w, w), F32)],
        compiler_params=_cparams(("parallel", "arbitrary")),
        name="hgrn_bwd" if rev else "hgrn_fwd",
    )(*args)


def _rwkv_kernel(*refs, rev, finish, dh, n_chunks, n_ctx_chunks):
    if finish:
        (tri_ref, z_ref, zp_ref, zn_ref, mup_ref, mun_ref, w0_ref, w2_ref, a0_ref, a2_ref, g2_ref,
         kk_ref, ka_ref, rk_ref, oy_ref, ob_ref, lnw_ref, lnb_ref, out_ref, st_ref) = refs
    else:
        (tri_ref, z_ref, zp_ref, zn_ref, mup_ref, mun_ref, w0_ref, w2_ref, a0_ref, a2_ref, g2_ref,
         kk_ref, ka_ref, rk_ref, y_ref, bonus_ref, st_ref) = refs
    c = z_ref.shape[0]
    w = w0_ref.shape[-1]
    j = pl.program_id(1)
    chunk = _chunk_of_step(j, n_chunks, n_ctx_chunks, rev)

    @pl.when(j == 0)
    def _():
        st_ref[...] = jnp.zeros_like(st_ref)

    z = z_ref[...]
    row = _iota((c, 1), 0)
    seg_first = (chunk == 0) | (chunk == n_ctx_chunks)
    seg_last = (chunk == n_ctx_chunks - 1) | (chunk == n_chunks - 1)
    prev_row = jnp.where(seg_first, 0.0, zp_ref[7:8, :])
    next_row = jnp.where(seg_last, 0.0, zn_ref[0:1, :])
    z_prev = jnp.where(row == 0, prev_row, pltpu.roll(z, 1, 0))
    z_next = jnp.where(row == c - 1, next_row, pltpu.roll(z, c - 1, 0))
    u = z + mup_ref[...] * (z_prev - z) + mun_ref[...] * (z_next - z)

    r = u[:, 0:w]
    k = u[:, w:2 * w]
    v = u[:, 2 * w:3 * w]
    lora = u[:, 3 * w:3 * w + LANE]
    gd = u[:, 3 * w + LANE:3 * w + 2 * LANE]

    w_log = -_softplus(-(w0_ref[...] + _dot(jnp.tanh(lora), w2_ref[...], HI))) - 0.5
    lw = -jnp.exp(w_log)
    a_lr = _sigmoid(a0_ref[...] + _dot(lora, a2_ref[...], HI))
    kk = k * kk_ref[...]
    kk = kk * lax.rsqrt(jnp.maximum(_head_sum(kk * kk, dh), 1e-24))
    kd = k * (1.0 + (a_lr - 1.0) * ka_ref[...])
    a_vec = -kk
    b_vec = kk * a_lr
    bonus = _head_sum(r * kd * rk_ref[...], dh) * v

    cum = _dot(tri_ref[...], lw, HI)
    tot = cum[0:1] if rev else cum[c - 1:c]
    gam = jnp.exp(cum)
    inv_gam = jnp.exp(-cum)
    rest = jnp.exp(tot - cum)
    r_t = r * gam
    a_t = a_vec * jnp.exp(cum - lw)
    b_t = b_vec * inv_gam
    k_t = kd * inv_gam
    b_h = b_vec * rest
    k_h = kd * rest

    ea, er, eb, ek, ev = (_expand_heads(t, dh) for t in (a_t, r_t, b_t, k_t, v))
    nr = ea.shape[0]
    ti = _mod(_iota((nr, 1), 0), c)
    si = _mod(_iota((1, nr), 1), c)
    strict = (si > ti) if rev else (si < ti)
    incl = (si >= ti) if rev else (si <= ti)
    a_ab = jnp.where(strict, _dot_nt(ea, eb, HI), 0.0)
    a_ak = jnp.where(strict, _dot_nt(ea, ek, HI), 0.0)
    a_rb = jnp.where(incl, _dot_nt(er, eb, HI), 0.0)
    a_rk = jnp.where(incl, _dot_nt(er, ek, HI), 0.0)

    eye = (_iota((nr, nr), 0) == _iota((nr, nr), 1)).astype(F32)
    tinv = eye + a_ab
    pw = a_ab
    for _ in range(int(math.log2(c)) - 1):
        pw = _dot(pw, pw, HI)
        tinv = tinv + _dot(tinv, pw, HI)

    w1e = _dot(tinv, ea, HI)
    w2e = _dot(tinv, _dot(a_ak, ev, HI), HI)
    w1c = _collapse_heads(w1e, c)
    w2c = _collapse_heads(w2e, c)
    p_mat = r_t + _collapse_heads(_dot(a_rb, w1e, HI), c)
    y0 = _collapse_heads(_dot(a_rb, w2e, HI) + _dot(a_rk, ev, HI), c)
    bd = _same_head(w, dh)
    eye_w = (_iota((w, w), 0) == _iota((w, w), 1)).astype(F32)
    m_mat = eye_w * jnp.exp(tot) + jnp.where(bd, _dot_tn(b_h, w1c, HI), 0.0)
    g_mat = jnp.where(bd, _dot_tn(b_h, w2c, HI) + _dot_tn(k_h, v, HI), 0.0)

    h0 = st_ref[...]
    y = _dot(p_mat, h0, HI) + y0
    st_ref[...] = _dot(m_mat, h0, HI) + g_mat

    if finish:
        y = y + oy_ref[...]
        mean = _head_sum(y, dh) * (1.0 / dh)
        yc = y - mean
        var = _head_sum(yc * yc, dh) * (1.0 / dh)
        yn = yc * lax.rsqrt(var + RW_GN_EPS) * lnw_ref[...] + lnb_ref[...]
        gate = _dot(_sigmoid(gd), g2_ref[...], HI)
        out_ref[...] = (yn + bonus + ob_ref[...]) * gate
    else:
        y_ref[...] = y
        bonus_ref[...] = bonus


def _rwkv_pass(u_rw, p, n_ctx_chunks, rev, other=None):
    b, n, wu = u_rw.shape
    c = CHUNK
    w = p["w0"].shape[-1]
    dh = w // N_HEADS
    nc = n // c
    finish = other is not None
    t = np.arange(c)
    tri = jnp.asarray(((t[None, :] >= t[:, None]) if rev else (t[None, :] <= t[:, None])).astype(np.float32))
    d = 1 if rev else 0
    chunk = lambda j: _chunk_of_step(j, nc, n_ctx_chunks, rev)
    sub = c // 8
    const = lambda a: pl.BlockSpec(a.shape, lambda bi, j: (0,) * a.ndim)
    col = pl.BlockSpec((None, c, w), lambda bi, j: (bi, chunk(j), 0))
    args = [tri, u_rw, u_rw, u_rw, p["mu_prev"], p["mu_next"], p["w0"][d], p["w2"][d], p["a0"][d],
            p["a2"][d], p["g2"], p["k_k"], p["k_a"], p["r_k"]]
    in_specs = [const(tri),
                pl.BlockSpec((None, c, wu), lambda bi, j: (bi, chunk(j), 0)),
                pl.BlockSpec((None, 8, wu), lambda bi, j: (bi, jnp.maximum(chunk(j) * sub - 1, 0), 0)),
                pl.BlockSpec((None, 8, wu), lambda bi, j: (bi, jnp.minimum((chunk(j) + 1) * sub, n // 8 - 1), 0))]
    in_specs += [const(a) for a in args[4:]]
    if finish:
        args += [other[0], other[1], p["ln_w"], p["ln_b"]]
        in_specs += [col, col, const(p["ln_w"]), const(p["ln_b"])]
        out_specs, out_shape = col, jax.ShapeDtypeStruct((b, n, w), F32)
    else:
        out_specs = [col, col]
        out_shape = [jax.ShapeDtypeStruct((b, n, w), F32)] * 2
    return pl.pallas_call(
        functools.partial(_rwkv_kernel, rev=rev, finish=finish, dh=dh, n_chunks=nc, n_ctx_chunks=n_ctx_chunks),
        grid=(b, nc),
        in_specs=in_specs,
        out_specs=out_specs,
        out_shape=out_shape,
        scratch_shapes=[pltpu.VMEM((w, w), F32)],
        compiler_params=_cparams(("parallel", "arbitrary")),
        name="rwkv_bwd" if rev else "rwkv_fwd",
    )(*args)


def _attn_kernel(lam_ref, q_ref, k_ref, v_ref, ng_ref, o_ref, *, lam_init, n_ctx, n_ctx_tiles):
    lp = lam_ref[...]
    lam = (jnp.exp(jnp.sum(lp[0:1] * lp[1:2], axis=-1, keepdims=True))
           - jnp.exp(jnp.sum(lp[2:3] * lp[3:4], axis=-1, keepdims=True)) + lam_init)

    def attend(k, v):
        q = q_ref[...]
        half = q.shape[-1] // 2
        lane = _iota((1, q.shape[-1]), 1)

        def one_map(qm):
            s = _dot_nt(qm, k)
            p = jnp.exp(s - jnp.max(s, axis=-1, keepdims=True))
            return _dot(p.astype(BF16), v) / jnp.sum(p, axis=-1, keepdims=True)

        o = one_map(jnp.where(lane < half, q, 0)) - lam * one_map(jnp.where(lane >= half, q, 0))
        ms = jnp.mean(o * o, axis=-1, keepdims=True)
        o_ref[...] = o * lax.rsqrt(ms + RMS_EPS) * ng_ref[...] * (1.0 - lam_init)

    is_ctx = pl.program_id(2) < n_ctx_tiles

    @pl.when(is_ctx)
    def _():
        attend(k_ref[0:n_ctx, :], v_ref[0:n_ctx, :])

    @pl.when(jnp.logical_not(is_ctx))
    def _():
        attend(k_ref[...], v_ref[...])


def _attention(u_da, lam_rows, norm_g_row, lam_init, n_ctx):
    b, n, w3 = u_da.shape
    wd = w3 // 3
    dv = wd // N_HEADS
    tq = ROW_TILE
    kern = functools.partial(_attn_kernel, lam_init=lam_init, n_ctx=n_ctx, n_ctx_tiles=n_ctx // tq)
    return pl.pallas_call(
        kern,
        grid=(b, N_HEADS, n // tq),
        in_specs=[pl.BlockSpec(lam_rows.shape, lambda bi, h, i: (0, 0)),
                  pl.BlockSpec((None, tq, dv), lambda bi, h, i: (bi, i, h)),
                  pl.BlockSpec((None, n, dv), lambda bi, h, i: (bi, 0, N_HEADS + h)),
                  pl.BlockSpec((None, n, dv), lambda bi, h, i: (bi, 0, 2 * N_HEADS + h)),
                  pl.BlockSpec((1, dv), lambda bi, h, i: (0, 0))],
        out_specs=pl.BlockSpec((None, tq, dv), lambda bi, h, i: (bi, i, h)),
        out_shape=jax.ShapeDtypeStruct((b, n, wd), F32),
        compiler_params=_cparams(("parallel", "parallel", "arbitrary")),
        name="diff_attn",
    )(lam_rows, u_da, u_da, u_da, norm_g_row)


def _out_proj_kernel(s_ref, hg_ref, da_ref, rw_ref, mod_ref, g_ref, whg_ref, wda_ref, wrw_ref,
                     x_ref, h_ref):
    mix = (_dot(hg_ref[...].astype(BF16), whg_ref[...]) + _dot(da_ref[...].astype(BF16), wda_ref[...])
           + _dot(rw_ref[...].astype(BF16), wrw_ref[...]))
    x = s_ref[...] + mod_ref[2:3, :] * mix
    x_ref[...] = x
    ms = jnp.mean(x * x, axis=-1, keepdims=True)
    y = x * lax.rsqrt(ms + RMS_EPS) * g_ref[...]
    h_ref[...] = (y * (1.0 + mod_ref[4:5, :]) + mod_ref[3:4, :]).astype(BF16)


def _out_proj(s, o_hg, o_da, o_rw, mods_l, g, whg, wda, wrw, n_ctx_tiles):
    b, n, d = s.shape
    tm = ROW_TILE
    sel = lambda bi, i: (jnp.where(i < n_ctx_tiles, b, bi), 0, 0)
    full = lambda a: pl.BlockSpec(a.shape, lambda bi, i: (0, 0))
    row = lambda w: pl.BlockSpec((None, tm, w), lambda bi, i: (bi, i, 0))
    return pl.pallas_call(
        _out_proj_kernel,
        grid=(b, n // tm),
        in_specs=[row(d), row(o_hg.shape[-1]), row(o_da.shape[-1]), row(o_rw.shape[-1]),
                  pl.BlockSpec((None, 8, d), sel), full(g), full(whg), full(wda), full(wrw)],
        out_specs=[row(d), row(d)],
        out_shape=[jax.ShapeDtypeStruct((b, n, d), F32), jax.ShapeDtypeStruct((b, n, d), BF16)],
        compiler_params=_cparams(("parallel", "parallel")),
        name="out_proj",
    )(s, o_hg, o_da, o_rw, mods_l, g, whg, wda, wrw)


def _ffn_kernel(x_ref, h_ref, mod_ref, wg_ref, wu_ref, wd_ref, o_ref):
    h = h_ref[...]
    gate = _dot(h, wg_ref[...])
    up = _dot(h, wu_ref[...])
    act = (gate * _sigmoid(gate) * up).astype(BF16)
    o_ref[...] = x_ref[...] + mod_ref[5:6, :] * _dot(act, wd_ref[...])


def _ffn(x, h, mods_l, wg, wu, wd, n_ctx_tiles):
    b, n, d = x.shape
    tm = ROW_TILE
    sel = lambda bi, i: (jnp.where(i < n_ctx_tiles, b, bi), 0, 0)
    resident = lambda a: pl.BlockSpec(a.shape, lambda bi, i: (0, 0), pipeline_mode=pl.Buffered(1))
    row = pl.BlockSpec((None, tm, d), lambda bi, i: (bi, i, 0))
    return pl.pallas_call(
        _ffn_kernel,
        grid=(b, n // tm),
        in_specs=[row, row, pl.BlockSpec((None, 8, d), sel), resident(wg), resident(wu), resident(wd)],
        out_specs=row,
        out_shape=jax.ShapeDtypeStruct((b, n, d), F32),
        compiler_params=_cparams(("parallel", "parallel")),
        name="ffn",
    )(x, h, mods_l, wg, wu, wd)


def _final_norm_kernel(x_ref, g_ref, o_ref):
    x = x_ref[...]
    ms = jnp.mean(x * x, axis=-1, keepdims=True)
    o_ref[...] = x * lax.rsqrt(ms + RMS_EPS) * g_ref[...]


def _final_norm(s, g, n_ctx):
    b, n, d = s.shape
    tm = ROW_TILE
    off = n_ctx // tm
    return pl.pallas_call(
        _final_norm_kernel,
        grid=(b, (n - n_ctx) // tm),
        in_specs=[pl.BlockSpec((None, tm, d), lambda bi, i: (bi, i + off, 0)),
                  pl.BlockSpec((1, d), lambda bi, i: (0, 0))],
        out_specs=pl.BlockSpec((None, tm, d), lambda bi, i: (bi, i, 0)),
        out_shape=jax.ShapeDtypeStruct((b, n - n_ctx, d), F32),
        compiler_params=_cparams(("parallel", "parallel")),
        name="final_norm",
    )(s, g)


def _rope_tables(n_ctx, n_lat, dqk):
    n_freq = dqk // 4
    t = np.arange(n_lat)
    inv_freq = ROPE_BASE ** (-np.arange(n_freq, dtype=np.float32) / n_freq)
    ang = np.concatenate([(t // GRID_W)[:, None] * inv_freq, (t % GRID_W)[:, None] * inv_freq], axis=-1)
    reps = LANE // dqk
    cos = np.tile(np.concatenate([np.cos(ang), np.cos(ang)], axis=-1), (1, reps))
    sin = np.tile(np.concatenate([-np.sin(ang), np.sin(ang)], axis=-1), (1, reps))
    cos = np.concatenate([np.ones((n_ctx, LANE)), cos], axis=0)
    sin = np.concatenate([np.zeros((n_ctx, LANE)), sin], axis=0)
    return jnp.asarray(cos, F32), jnp.asarray(sin, F32)


def _pad_rows(a, start, total):
    return jnp.zeros((total, a.shape[-1]), a.dtype).at[start:start + a.shape[0]].set(a)


def kernel(x, c, ctx, c_ctx, ada_w, ada_b, norm1_g, norm2_g, w_in, w_out, hg_lb_logits, hg_norm_g, da_lam_q1, da_lam_k1, da_lam_q2, da_lam_k2, da_norm_g, rw_mu_prev, rw_mu_next, rw_w0, rw_w2, rw_a0, rw_a2, rw_g2, rw_k_k, rw_k_a, rw_r_k, rw_ln_w, rw_ln_b, ffn_w_gate, ffn_w_up, ffn_w_down, final_norm_g):
    b, n_lat, d = x.shape
    n_ctx = ctx.shape[1]
    depth = ada_w.shape[0]
    hg_w = hg_lb_logits.shape[-1]
    rw_w = rw_w0.shape[-1]
    da_w = w_out.shape[1] - hg_w - rw_w
    hg_cols, da_cols = 5 * hg_w, 3 * da_w
    rw_cols = w_in.shape[-1] - hg_cols - da_cols
    rw_pad = 3 * rw_w + 2 * LANE
    r_dec, r_icl, r_gate = rw_w2.shape[2], rw_a2.shape[2], rw_g2.shape[1]
    assert n_ctx % ROW_TILE == 0 and n_lat % ROW_TILE == 0 and n_lat % GRID_W == 0
    assert 2 * (r_dec + r_icl) == LANE and r_gate <= LANE and rw_cols == 3 * rw_w + LANE + r_gate
    n_ctx_tiles, n_ctx_chunks = n_ctx // ROW_TILE, n_ctx // CHUNK

    cond = jnp.zeros((16, d), F32).at[:b].set(c).at[b].set(c_ctx)
    mods = _ada_mod(cond, ada_w, ada_b).reshape(depth, 16, 6, d)
    mods = jnp.pad(mods, ((0, 0), (0, 0), (0, 2), (0, 0)))

    p_lb = jax.nn.softmax(hg_lb_logits.astype(F32), axis=1)
    lower_bounds = jnp.cumsum(p_lb, axis=1) - p_lb[:, :1]
    cos_t, sin_t = _rope_tables(n_ctx, n_lat, da_w // (2 * N_HEADS))

    s = jnp.concatenate([ctx, x], axis=1)
    for l in range(depth):
        w_l = w_in[l]
        whg = w_l[:, :hg_cols].astype(BF16)
        wda = w_l[:, hg_cols:hg_cols + da_cols].astype(BF16)
        wrw = jnp.pad(w_l[:, hg_cols + da_cols:], ((0, 0), (0, rw_pad - rw_cols))).astype(BF16)
        u_hg, u_da, u_rw = _in_proj(s, mods[l], norm1_g[l][None], cos_t, sin_t, whg, wda, wrw, n_ctx_tiles)

        o_b = _hgrn_pass(u_hg, lower_bounds[1, l][None], n_ctx_chunks, True)
        o_hg = _hgrn_pass(u_hg, lower_bounds[0, l][None], n_ctx_chunks, False, other=o_b,
                          norm_g_row=jnp.tile(hg_norm_g[l], N_HEADS)[None])

        lam_init = 0.8 - 0.6 * math.exp(-0.3 * l)
        lam_rows = jnp.zeros((8, LANE), F32).at[0:4, :da_lam_q1.shape[-1]].set(
            jnp.stack([da_lam_q1[l], da_lam_k1[l], da_lam_q2[l], da_lam_k2[l]]))
        o_da = _attention(u_da, lam_rows, da_norm_g[l][None], lam_init, n_ctx)

        mu_pad = lambda m: jnp.pad(m, (0, rw_pad - rw_cols))[None]
        off_dec, off_icl = 0, 2 * r_dec
        rw_p = {
            "mu_prev": mu_pad(rw_mu_prev[l]), "mu_next": mu_pad(rw_mu_next[l]),
            "w0": rw_w0[:, l][:, None, :], "a0": rw_a0[:, l][:, None, :],
            "w2": jnp.stack([_pad_rows(rw_w2[i, l], off_dec + i * r_dec, LANE) for i in range(2)]),
            "a2": jnp.stack([_pad_rows(rw_a2[i, l], off_icl + i * r_icl, LANE) for i in range(2)]),
            "g2": _pad_rows(rw_g2[l], 0, LANE),
            "k_k": rw_k_k[l][None], "k_a": rw_k_a[l][None], "r_k": rw_r_k[l][None],
            "ln_w": rw_ln_w[l][None], "ln_b": rw_ln_b[l][None],
        }
        yb = _rwkv_pass(u_rw, rw_p, n_ctx_chunks, True)
        o_rw = _rwkv_pass(u_rw, rw_p, n_ctx_chunks, False, other=yb)

        wo = w_out[l].astype(BF16)
        x_mid, h2 = _out_proj(s, o_hg, o_da, o_rw, mods[l], norm2_g[l][None],
                              wo[:hg_w], wo[hg_w:hg_w + da_w], wo[hg_w + da_w:], n_ctx_tiles)
        s = _ffn(x_mid, h2, mods[l], ffn_w_gate[l].astype(BF16), ffn_w_up[l].astype(BF16),
                 ffn_w_down[l].astype(BF16), n_ctx_tiles)
    return _final_norm(s, final_norm_g[None], n_ctx)
```

```python
import functools
import math

import numpy as np
import jax
import jax.numpy as jnp
from jax import lax
from jax.experimental import pallas as pl
from jax.experimental.pallas import tpu as pltpu

F32 = jnp.float32
BF16 = jnp.bfloat16
HI = lax.Precision.HIGHEST

N_HEADS = 4
GRID_W = 64
ROPE_BASE = 10000.0
RMS_EPS = 1e-6
RW_GN_EPS = 64e-5

ROW_TILE = 256
CHUNK = 64
REC_BATCH = 2
LANE = 128
VMEM_LIMIT = 56 * 1024 * 1024


def _cparams(sem):
    return pltpu.CompilerParams(dimension_semantics=sem, vmem_limit_bytes=VMEM_LIMIT)


def _dot_dims(a, b, dims, prec):
    dg = lambda x, y, p=None: lax.dot_general(x, y, (dims, ((), ())), preferred_element_type=F32, precision=p)
    if prec == "bf16":
        return dg(a.astype(BF16), b.astype(BF16))
    if prec == "x3":
        a_hi, b_hi = a.astype(BF16), b.astype(BF16)
        a_lo = (a - a_hi.astype(F32)).astype(BF16)
        b_lo = (b - b_hi.astype(F32)).astype(BF16)
        return dg(a_hi, b_hi) + (dg(a_lo, b_hi) + dg(a_hi, b_lo))
    return dg(a, b, prec)


def _dot(a, b, prec=None):
    return _dot_dims(a, b, ((1,), (0,)), prec)


def _dot_nt(a, b, prec=None):
    return _dot_dims(a, b, ((1,), (1,)), prec)


def _dot_tn(a, b, prec=None):
    return _dot_dims(a, b, ((0,), (0,)), prec)


P_HG = "bf16"
P_SC = "bf16"
P_INV = "bf16"
P_APP = "bf16"
P_ST = "bf16"


def _iota(shape, dim):
    return lax.broadcasted_iota(jnp.int32, shape, dim)


def _div(x, d):
    assert d & (d - 1) == 0
    return x >> (d.bit_length() - 1)


def _mod(x, d):
    assert d & (d - 1) == 0
    return x & (d - 1)


def _softplus(x):
    return jnp.maximum(x, 0.0) + jnp.log(1.0 + jnp.exp(-jnp.abs(x)))


def _same_head(w, dh):
    return _div(_iota((w, w), 0), dh) == _div(_iota((w, w), 1), dh)


def _expand_heads(x, dh):
    w = x.shape[-1]
    head = _div(_iota((1, w), 1), dh)
    return jnp.concatenate([jnp.where(head == h, x, 0.0) for h in range(w // dh)], axis=0)


def _collapse_heads(x, c):
    out = x[0:c]
    for h in range(1, x.shape[0] // c):
        out = out + x[h * c:(h + 1) * c]
    return out


def _head_sum(x, dh):
    w = x.shape[-1]
    ones_bd = _same_head(w, dh).astype(F32)
    return _dot(x, ones_bd, HI)


def _sigmoid(x):
    return 1.0 / (1.0 + jnp.exp(-x))


def _ada_kernel(c_ref, w_ref, b_ref, o_ref):
    c = c_ref[...]
    sc = c * _sigmoid(c)
    o_ref[...] = _dot(sc, w_ref[...], HI) + b_ref[...]


def _ada_mod(cond, ada_w, ada_b):
    depth, d, d6 = ada_w.shape
    rows = cond.shape[0]
    tn = 1536
    return pl.pallas_call(
        _ada_kernel,
        grid=(depth, d6 // tn),
        in_specs=[pl.BlockSpec((rows, d), lambda l, j: (0, 0)),
                  pl.BlockSpec((None, d, tn), lambda l, j: (l, 0, j)),
                  pl.BlockSpec((None, 1, tn), lambda l, j: (l, 0, j))],
        out_specs=pl.BlockSpec((None, rows, tn), lambda l, j: (l, 0, j)),
        out_shape=jax.ShapeDtypeStruct((depth, rows, d6), F32),
        compiler_params=_cparams(("arbitrary", "arbitrary")),
        name="ada_mod",
    )(cond, ada_w, ada_b.reshape(depth, 1, d6))


def _rope_cols(t, cos, sin_signed, half):
    lane = _iota((1, t.shape[-1]), 1)
    first = _mod(lane, 2 * half) < half
    swapped = jnp.where(first, pltpu.roll(t, t.shape[-1] - half, 1), pltpu.roll(t, half, 1))
    return t * cos + swapped * sin_signed


def _in_proj_kernel(s_ref, mod_ref, g_ref, cos_ref, sin_ref, whg_ref, wda_ref, wrw_ref,
                    uhg_ref, uda_ref, urw_ref, *, da_width, q_scale):
    x = s_ref[...]
    ms = jnp.mean(x * x, axis=-1, keepdims=True)
    y = x * lax.rsqrt(ms + RMS_EPS) * g_ref[...]
    h = (y * (1.0 + mod_ref[1:2, :]) + mod_ref[0:1, :]).astype(BF16)
    uhg_ref[...] = _dot(h, whg_ref[...])
    urw_ref[...] = _dot(h, wrw_ref[...])
    uda = _dot(h, wda_ref[...])
    cos = cos_ref[...]
    sin = sin_ref[...]
    dqk = da_width // (2 * N_HEADS)
    for j in range(2 * da_width // LANE):
        t = _rope_cols(uda[:, j * LANE:(j + 1) * LANE], cos, sin, dqk // 2)
        if j < da_width // LANE:
            t = t * q_scale
        uda_ref[:, j * LANE:(j + 1) * LANE] = t.astype(BF16)
    uda_ref[:, 2 * da_width:] = uda[:, 2 * da_width:].astype(BF16)


def _in_proj(s, mods_l, g, cos_t, sin_t, whg, wda, wrw, n_ctx_tiles):
    b, n, d = s.shape
    tm = ROW_TILE
    da_width = wda.shape[1] // 3
    dqk = da_width // (2 * N_HEADS)
    sel = lambda bi, i: (jnp.where(i < n_ctx_tiles, b, bi), 0, 0)
    full = lambda a: pl.BlockSpec(a.shape, lambda bi, i: (0, 0))
    row = lambda w: pl.BlockSpec((None, tm, w), lambda bi, i: (bi, i, 0))
    kern = functools.partial(_in_proj_kernel, da_width=da_width, q_scale=dqk ** -0.5)
    return pl.pallas_call(
        kern,
        grid=(b, n // tm),
        in_specs=[row(d), pl.BlockSpec((None, 8, d), sel), full(g),
                  pl.BlockSpec((tm, LANE), lambda bi, i: (i, 0)),
                  pl.BlockSpec((tm, LANE), lambda bi, i: (i, 0)),
                  full(whg), full(wda), full(wrw)],
        out_specs=[row(whg.shape[1]), row(wda.shape[1]), row(wrw.shape[1])],
        out_shape=[jax.ShapeDtypeStruct((b, n, whg.shape[1]), F32),
                   jax.ShapeDtypeStruct((b, n, wda.shape[1]), BF16),
                   jax.ShapeDtypeStruct((b, n, wrw.shape[1]), F32)],
        compiler_params=_cparams(("parallel", "parallel")),
        name="in_proj",
    )(s, mods_l, g, cos_t, sin_t, whg, wda, wrw)


def _chunk_of_step(j, n_chunks, n_ctx_chunks, rev):
    if not rev:
        return j
    return jnp.where(j < n_ctx_chunks, n_ctx_chunks - 1 - j, n_chunks - 1 - (j - n_ctx_chunks))


def _hgrn_levels(c):
    return [1 << i for i in range(int(math.log2(c)))]


def _hgrn_const(c, rev):
    t = np.arange(c)[:, None]
    s = np.arange(c)[None, :]
    tri = (s >= t) if rev else (s <= t)
    mats = [tri.astype(np.float32)]
    for m in _hgrn_levels(c):
        p = (t // (2 * m)) * (2 * m) + (m if rev else m - 1)
        ref = (s >= p) if rev else (s <= p)
        mats.append(tri.astype(np.float32) - ref.astype(np.float32))
    return np.concatenate(mats, axis=0)


def _hgrn_kernel(*refs, rev, finish, dh):
    if finish:
        const_ref, q_ref, i_ref, f_ref, lb_ref, g_ref, other_ref, ng_ref, out_ref, st_ref = refs
    else:
        const_ref, q_ref, i_ref, f_ref, lb_ref, out_ref, st_ref = refs
    nb, c, w = q_ref.shape

    @pl.when(pl.program_id(1) == 0)
    def _():
        st_ref[...] = jnp.zeros_like(st_ref)

    nh = w // dh
    t_idx = _iota((c, 1), 0)
    s_idx = _mod(_iota((1, nh * c), 1), c)
    bd = _same_head(w, dh)
    lb = lb_ref[...]
    for bb in range(nb):
        q = q_ref[bb]
        v = i_ref[bb]
        f = lb + (1.0 - lb) * _sigmoid(f_ref[bb])
        kx = 1.0 - f
        seg = _dot(const_ref[...], jnp.log(f), HI)
        bcum = seg[0:c]
        bend = bcum[0:1] if rev else bcum[c - 1:c]
        att = jnp.zeros((c, nh * c), F32)
        for li, m in enumerate(_hgrn_levels(c)):
            e = jnp.exp(-jnp.abs(seg[(li + 1) * c:(li + 2) * c]))
            t_up = (t_idx & m) != 0
            s_up = (s_idx & m) != 0
            t_isq = jnp.logical_not(t_up) if rev else t_up
            s_isk = s_up if rev else jnp.logical_not(s_up)
            xm = jnp.where(t_isq, q, kx) * e
            sc = _dot_nt(xm, _expand_heads(xm, dh), P_HG)
            valid = (_div(t_idx, 2 * m) == _div(s_idx, 2 * m)) & t_isq & s_isk
            att = att + jnp.where(valid, sc, 0.0)
        o = _dot(att, _expand_heads(v, dh), P_HG)
        o = o + _head_sum(q * kx, dh) * v
        st = st_ref[bb]
        o = o + _dot_nt(q * jnp.exp(bcum), st, P_HG)
        upd = _dot_tn(v, kx * jnp.exp(bend - bcum), P_HG)
        st_ref[bb] = st * jnp.exp(bend) + jnp.where(bd, upd, 0.0)
        if finish:
            o = o + other_ref[bb]
            ms = _head_sum(o * o, dh) * (1.0 / dh)
            g = g_ref[bb]
            out_ref[bb] = o * lax.rsqrt(ms + RMS_EPS) * ng_ref[...] * (g * _sigmoid(g))
        else:
            out_ref[bb] = o


def _hgrn_pass(u_hg, lb_row, n_ctx_chunks, rev, other=None, norm_g_row=None):
    b, n, _ = u_hg.shape
    c = CHUNK
    w = lb_row.shape[-1]
    dh = w // N_HEADS
    nc = n // c
    finish = other is not None
    const = jnp.asarray(_hgrn_const(c, rev))
    chunk = lambda j: _chunk_of_step(j, nc, n_ctx_chunks, rev)
    nb = REC_BATCH
    col = lambda k: pl.BlockSpec((nb, c, w), lambda bi, j: (bi, chunk(j), k))
    vec = pl.BlockSpec((1, w), lambda bi, j: (0, 0))
    in_specs = [pl.BlockSpec(const.shape, lambda bi, j: (0, 0)), col(0), col(1), col(3 if rev else 2), vec]
    args = [const, u_hg, u_hg, u_hg, lb_row]
    if finish:
        in_specs += [col(4), col(0), vec]
        args += [u_hg, other, norm_g_row]
    return pl.pallas_call(
        functools.partial(_hgrn_kernel, rev=rev, finish=finish, dh=dh),
        grid=(b // nb, nc),
        in_specs=in_specs,
        out_specs=col(0),
        out_shape=jax.ShapeDtypeStruct((b, n, w), F32),
        scratch_shapes=[pltpu.VMEM((nb, w, w), F32)],
        compiler_params=_cparams(("parallel", "arbitrary")),
        name="hgrn_bwd" if rev else "hgrn_fwd",
    )(*args)


def _rwkv_kernel(*refs, rev, finish, dh, n_chunks, n_ctx_chunks):
    if finish:
        (tri_ref, z_ref, zp_ref, zn_ref, mup_ref, mun_ref, w0_ref, w2_ref, a0_ref, a2_ref, g2_ref,
         kk_ref, ka_ref, rk_ref, oy_ref, ob_ref, lnw_ref, lnb_ref, out_ref, st_ref) = refs
    else:
        (tri_ref, z_ref, zp_ref, zn_ref, mup_ref, mun_ref, w0_ref, w2_ref, a0_ref, a2_ref, g2_ref,
         kk_ref, ka_ref, rk_ref, y_ref, bonus_ref, st_ref) = refs
    nb, c, _ = z_ref.shape
    w = w0_ref.shape[-1]
    j = pl.program_id(1)
    chunk = _chunk_of_step(j, n_chunks, n_ctx_chunks, rev)

    @pl.when(j == 0)
    def _():
        st_ref[...] = jnp.zeros_like(st_ref)

    row = _iota((c, 1), 0)
    seg_first = (chunk == 0) | (chunk == n_ctx_chunks)
    seg_last = (chunk == n_ctx_chunks - 1) | (chunk == n_chunks - 1)
    nr = (w // dh) * c
    ti = _mod(_iota((nr, 1), 0), c)
    si = _mod(_iota((1, nr), 1), c)
    strict = (si > ti) if rev else (si < ti)
    incl = (si >= ti) if rev else (si <= ti)
    eye = (_iota((nr, nr), 0) == _iota((nr, nr), 1)).astype(F32)
    bd = _same_head(w, dh)
    eye_w = (_iota((w, w), 0) == _iota((w, w), 1)).astype(F32)

    for bb in range(nb):
        z = z_ref[bb]
        prev_row = jnp.where(seg_first, 0.0, zp_ref[bb, 7:8, :])
        next_row = jnp.where(seg_last, 0.0, zn_ref[bb, 0:1, :])
        z_prev = jnp.where(row == 0, prev_row, pltpu.roll(z, 1, 0))
        z_next = jnp.where(row == c - 1, next_row, pltpu.roll(z, c - 1, 0))
        u = z + mup_ref[...] * (z_prev - z) + mun_ref[...] * (z_next - z)

        r = u[:, 0:w]
        k = u[:, w:2 * w]
        v = u[:, 2 * w:3 * w]
        lora = u[:, 3 * w:3 * w + LANE]
        gd = u[:, 3 * w + LANE:3 * w + 2 * LANE]

        w_log = -_softplus(-(w0_ref[...] + _dot(jnp.tanh(lora), w2_ref[...], HI))) - 0.5
        lw = -jnp.exp(w_log)
        a_lr = _sigmoid(a0_ref[...] + _dot(lora, a2_ref[...], HI))
        kk = k * kk_ref[...]
        kk = kk * lax.rsqrt(jnp.maximum(_head_sum(kk * kk, dh), 1e-24))
        kd = k * (1.0 + (a_lr - 1.0) * ka_ref[...])
        a_vec = -kk
        b_vec = kk * a_lr
        bonus = _head_sum(r * kd * rk_ref[...], dh) * v

        cum = _dot(tri_ref[...], lw, HI)
        tot = cum[0:1] if rev else cum[c - 1:c]
        inv_gam = jnp.exp(-cum)
        rest = jnp.exp(tot - cum)
        r_t = r * jnp.exp(cum)
        a_t = a_vec * jnp.exp(cum - lw)
        b_t = b_vec * inv_gam
        k_t = kd * inv_gam
        b_h = b_vec * rest
        k_h = kd * rest

        ea, er, eb, ek, ev = (_expand_heads(t, dh) for t in (a_t, r_t, b_t, k_t, v))
        a_ab = jnp.where(strict, _dot_nt(ea, eb, P_SC), 0.0)
        a_ak = jnp.where(strict, _dot_nt(ea, ek, P_SC), 0.0)
        a_rb = jnp.where(incl, _dot_nt(er, eb, P_SC), 0.0)
        a_rk = jnp.where(incl, _dot_nt(er, ek, P_SC), 0.0)

        tinv = eye + a_ab
        pw = a_ab
        for _ in range(int(math.log2(c)) - 1):
            pw = _dot(pw, pw, P_INV)
            tinv = tinv + _dot(tinv, pw, P_INV)

        w1e = _dot(tinv, ea, P_APP)
        w2e = _dot(tinv, _dot(a_ak, ev, P_APP), P_APP)
        w1c = _collapse_heads(w1e, c)
        w2c = _collapse_heads(w2e, c)
        p_mat = r_t + _collapse_heads(_dot(a_rb, w1e, P_APP), c)
        y0 = _collapse_heads(_dot(a_rb, w2e, P_APP) + _dot(a_rk, ev, P_APP), c)
        m_mat = eye_w * jnp.exp(tot) + jnp.where(bd, _dot_tn(b_h, w1c, P_ST), 0.0)
        g_mat = jnp.where(bd, _dot_tn(b_h, w2c, P_ST) + _dot_tn(k_h, v, P_ST), 0.0)

        h0 = st_ref[bb]
        y = _dot(p_mat, h0, P_ST) + y0
        st_ref[bb] = _dot(m_mat, h0, P_ST) + g_mat

        if finish:
            y = y + oy_ref[bb]
            mean = _head_sum(y, dh) * (1.0 / dh)
            yc = y - mean
            var = _head_sum(yc * yc, dh) * (1.0 / dh)
            yn = yc * lax.rsqrt(var + RW_GN_EPS) * lnw_ref[...] + lnb_ref[...]
            gate = _dot(_sigmoid(gd), g2_ref[...], HI)
            out_ref[bb] = (yn + bonus + ob_ref[bb]) * gate
        else:
            y_ref[bb] = y
            bonus_ref[bb] = bonus


def _rwkv_pass(u_rw, p, n_ctx_chunks, rev, other=None):
    b, n, wu = u_rw.shape
    c = CHUNK
    w = p["w0"].shape[-1]
    dh = w // N_HEADS
    nc = n // c
    finish = other is not None
    t = np.arange(c)
    tri = jnp.asarray(((t[None, :] >= t[:, None]) if rev else (t[None, :] <= t[:, None])).astype(np.float32))
    d = 1 if rev else 0
    chunk = lambda j: _chunk_of_step(j, nc, n_ctx_chunks, rev)
    sub = c // 8
    nb = REC_BATCH
    const = lambda a: pl.BlockSpec(a.shape, lambda bi, j: (0,) * a.ndim)
    col = pl.BlockSpec((nb, c, w), lambda bi, j: (bi, chunk(j), 0))
    args = [tri, u_rw, u_rw, u_rw, p["mu_prev"], p["mu_next"], p["w0"][d], p["w2"][d], p["a0"][d],
            p["a2"][d], p["g2"], p["k_k"], p["k_a"], p["r_k"]]
    in_specs = [const(tri),
                pl.BlockSpec((nb, c, wu), lambda bi, j: (bi, chunk(j), 0)),
                pl.BlockSpec((nb, 8, wu), lambda bi, j: (bi, jnp.maximum(chunk(j) * sub - 1, 0), 0)),
                pl.BlockSpec((nb, 8, wu), lambda bi, j: (bi, jnp.minimum((chunk(j) + 1) * sub, n // 8 - 1), 0))]
    in_specs += [const(a) for a in args[4:]]
    if finish:
        args += [other[0], other[1], p["ln_w"], p["ln_b"]]
        in_specs += [col, col, const(p["ln_w"]), const(p["ln_b"])]
        out_specs, out_shape = col, jax.ShapeDtypeStruct((b, n, w), F32)
    else:
        out_specs = [col, col]
        out_shape = [jax.ShapeDtypeStruct((b, n, w), F32)] * 2
    return pl.pallas_call(
        functools.partial(_rwkv_kernel, rev=rev, finish=finish, dh=dh, n_chunks=nc, n_ctx_chunks=n_ctx_chunks),
        grid=(b // nb, nc),
        in_specs=in_specs,
        out_specs=out_specs,
        out_shape=out_shape,
        scratch_shapes=[pltpu.VMEM((nb, w, w), F32)],
        compiler_params=_cparams(("parallel", "arbitrary")),
        name="rwkv_bwd" if rev else "rwkv_fwd",
    )(*args)


def _attn_kernel(lam_ref, q_ref, k_ref, v_ref, ng_ref, o_ref, *, lam_init, n_ctx, n_ctx_tiles):
    lp = lam_ref[...]
    lam = (jnp.exp(jnp.sum(lp[0:1] * lp[1:2], axis=-1, keepdims=True))
           - jnp.exp(jnp.sum(lp[2:3] * lp[3:4], axis=-1, keepdims=True)) + lam_init)

    def attend(k, v):
        q = q_ref[...]
        half = q.shape[-1] // 2
        lane = _iota((1, q.shape[-1]), 1)

        def one_map(qm):
            s = _dot_nt(qm, k)
            p = jnp.exp(s - jnp.max(s, axis=-1, keepdims=True))
            return _dot(p.astype(BF16), v) / jnp.sum(p, axis=-1, keepdims=True)

        o = one_map(jnp.where(lane < half, q, 0)) - lam * one_map(jnp.where(lane >= half, q, 0))
        ms = jnp.mean(o * o, axis=-1, keepdims=True)
        o_ref[...] = o * lax.rsqrt(ms + RMS_EPS) * ng_ref[...] * (1.0 - lam_init)

    is_ctx = pl.program_id(2) < n_ctx_tiles

    @pl.when(is_ctx)
    def _():
        attend(k_ref[0:n_ctx, :], v_ref[0:n_ctx, :])

    @pl.when(jnp.logical_not(is_ctx))
    def _():
        attend(k_ref[...], v_ref[...])


def _attention(u_da, lam_rows, norm_g_row, lam_init, n_ctx):
    b, n, w3 = u_da.shape
    wd = w3 // 3
    dv = wd // N_HEADS
    tq = ROW_TILE
    kern = functools.partial(_attn_kernel, lam_init=lam_init, n_ctx=n_ctx, n_ctx_tiles=n_ctx // tq)
    return pl.pallas_call(
        kern,
        grid=(b, N_HEADS, n // tq),
        in_specs=[pl.BlockSpec(lam_rows.shape, lambda bi, h, i: (0, 0)),
                  pl.BlockSpec((None, tq, dv), lambda bi, h, i: (bi, i, h)),
                  pl.BlockSpec((None, n, dv), lambda bi, h, i: (bi, 0, N_HEADS + h)),
                  pl.BlockSpec((None, n, dv), lambda bi, h, i: (bi, 0, 2 * N_HEADS + h)),
                  pl.BlockSpec((1, dv), lambda bi, h, i: (0, 0))],
        out_specs=pl.BlockSpec((None, tq, dv), lambda bi, h, i: (bi, i, h)),
        out_shape=jax.ShapeDtypeStruct((b, n, wd), F32),
        compiler_params=_cparams(("parallel", "parallel", "arbitrary")),
        name="diff_attn",
    )(lam_rows, u_da, u_da, u_da, norm_g_row)


def _out_proj_kernel(s_ref, hg_ref, da_ref, rw_ref, mod_ref, g_ref, whg_ref, wda_ref, wrw_ref,
                     x_ref, h_ref):
    mix = (_dot(hg_ref[...].astype(BF16), whg_ref[...]) + _dot(da_ref[...].astype(BF16), wda_ref[...])
           + _dot(rw_ref[...].astype(BF16), wrw_ref[...]))
    x = s_ref[...] + mod_ref[2:3, :] * mix
    x_ref[...] = x
    ms = jnp.mean(x * x, axis=-1, keepdims=True)
    y = x * lax.rsqrt(ms + RMS_EPS) * g_ref[...]
    h_ref[...] = (y * (1.0 + mod_ref[4:5, :]) + mod_ref[3:4, :]).astype(BF16)


def _out_proj(s, o_hg, o_da, o_rw, mods_l, g, whg, wda, wrw, n_ctx_tiles):
    b, n, d = s.shape
    tm = ROW_TILE
    sel = lambda bi, i: (jnp.where(i < n_ctx_tiles, b, bi), 0, 0)
    full = lambda a: pl.BlockSpec(a.shape, lambda bi, i: (0, 0))
    row = lambda w: pl.BlockSpec((None, tm, w), lambda bi, i: (bi, i, 0))
    return pl.pallas_call(
        _out_proj_kernel,
        grid=(b, n // tm),
        in_specs=[row(d), row(o_hg.shape[-1]), row(o_da.shape[-1]), row(o_rw.shape[-1]),
                  pl.BlockSpec((None, 8, d), sel), full(g), full(whg), full(wda), full(wrw)],
        out_specs=[row(d), row(d)],
        out_shape=[jax.ShapeDtypeStruct((b, n, d), F32), jax.ShapeDtypeStruct((b, n, d), BF16)],
        compiler_params=_cparams(("parallel", "parallel")),
        name="out_proj",
    )(s, o_hg, o_da, o_rw, mods_l, g, whg, wda, wrw)


def _ffn_kernel(x_ref, h_ref, mod_ref, wg_ref, wu_ref, wd_ref, o_ref):
    h = h_ref[...]
    gate = _dot(h, wg_ref[...])
    up = _dot(h, wu_ref[...])
    act = (gate * _sigmoid(gate) * up).astype(BF16)
    o_ref[...] = x_ref[...] + mod_ref[5:6, :] * _dot(act, wd_ref[...])


def _ffn(x, h, mods_l, wg, wu, wd, n_ctx_tiles):
    b, n, d = x.shape
    tm = ROW_TILE
    sel = lambda bi, i: (jnp.where(i < n_ctx_tiles, b, bi), 0, 0)
    resident = lambda a: pl.BlockSpec(a.shape, lambda bi, i: (0, 0), pipeline_mode=pl.Buffered(1))
    row = pl.BlockSpec((None, tm, d), lambda bi, i: (bi, i, 0))
    return pl.pallas_call(
        _ffn_kernel,
        grid=(b, n // tm),
        in_specs=[row, row, pl.BlockSpec((None, 8, d), sel), resident(wg), resident(wu), resident(wd)],
        out_specs=row,
        out_shape=jax.ShapeDtypeStruct((b, n, d), F32),
        compiler_params=_cparams(("parallel", "parallel")),
        name="ffn",
    )(x, h, mods_l, wg, wu, wd)


def _final_norm_kernel(x_ref, g_ref, o_ref):
    x = x_ref[...]
    ms = jnp.mean(x * x, axis=-1, keepdims=True)
    o_ref[...] = x * lax.rsqrt(ms + RMS_EPS) * g_ref[...]


def _final_norm(s, g, n_ctx):
    b, n, d = s.shape
    tm = ROW_TILE
    off = n_ctx // tm
    return pl.pallas_call(
        _final_norm_kernel,
        grid=(b, (n - n_ctx) // tm),
        in_specs=[pl.BlockSpec((None, tm, d), lambda bi, i: (bi, i + off, 0)),
                  pl.BlockSpec((1, d), lambda bi, i: (0, 0))],
        out_specs=pl.BlockSpec((None, tm, d), lambda bi, i: (bi, i, 0)),
        out_shape=jax.ShapeDtypeStruct((b, n - n_ctx, d), F32),
        compiler_params=_cparams(("parallel", "parallel")),
        name="final_norm",
    )(s, g)


def _rope_tables(n_ctx, n_lat, dqk):
    n_freq = dqk // 4
    t = np.arange(n_lat)
    inv_freq = ROPE_BASE ** (-np.arange(n_freq, dtype=np.float32) / n_freq)
    ang = np.concatenate([(t // GRID_W)[:, None] * inv_freq, (t % GRID_W)[:, None] * inv_freq], axis=-1)
    reps = LANE // dqk
    cos = np.tile(np.concatenate([np.cos(ang), np.cos(ang)], axis=-1), (1, reps))
    sin = np.tile(np.concatenate([-np.sin(ang), np.sin(ang)], axis=-1), (1, reps))
    cos = np.concatenate([np.ones((n_ctx, LANE)), cos], axis=0)
    sin = np.concatenate([np.zeros((n_ctx, LANE)), sin], axis=0)
    return jnp.asarray(cos, F32), jnp.asarray(sin, F32)


def _pad_rows(a, start, total):
    return jnp.zeros((total, a.shape[-1]), a.dtype).at[start:start + a.shape[0]].set(a)


def kernel(x, c, ctx, c_ctx, ada_w, ada_b, norm1_g, norm2_g, w_in, w_out, hg_lb_logits, hg_norm_g, da_lam_q1, da_lam_k1, da_lam_q2, da_lam_k2, da_norm_g, rw_mu_prev, rw_mu_next, rw_w0, rw_w2, rw_a0, rw_a2, rw_g2, rw_k_k, rw_k_a, rw_r_k, rw_ln_w, rw_ln_b, ffn_w_gate, ffn_w_up, ffn_w_down, final_norm_g):
    b, n_lat, d = x.shape
    n_ctx = ctx.shape[1]
    depth = ada_w.shape[0]
    hg_w = hg_lb_logits.shape[-1]
    rw_w = rw_w0.shape[-1]
    da_w = w_out.shape[1] - hg_w - rw_w
    hg_cols, da_cols = 5 * hg_w, 3 * da_w
    rw_cols = w_in.shape[-1] - hg_cols - da_cols
    rw_pad = 3 * rw_w + 2 * LANE
    r_dec, r_icl, r_gate = rw_w2.shape[2], rw_a2.shape[2], rw_g2.shape[1]
    assert n_ctx % ROW_TILE == 0 and n_lat % ROW_TILE == 0 and n_lat % GRID_W == 0 and b % REC_BATCH == 0
    assert 2 * (r_dec + r_icl) == LANE and r_gate <= LANE and rw_cols == 3 * rw_w + LANE + r_gate
    n_ctx_tiles, n_ctx_chunks = n_ctx // ROW_TILE, n_ctx // CHUNK

    cond = jnp.zeros((16, d), F32).at[:b].set(c).at[b].set(c_ctx)
    mods = _ada_mod(cond, ada_w, ada_b).reshape(depth, 16, 6, d)
    mods = jnp.pad(mods, ((0, 0), (0, 0), (0, 2), (0, 0)))

    p_lb = jax.nn.softmax(hg_lb_logits.astype(F32), axis=1)
    lower_bounds = jnp.cumsum(p_lb, axis=1) - p_lb[:, :1]
    cos_t, sin_t = _rope_tables(n_ctx, n_lat, da_w // (2 * N_HEADS))

    s = jnp.concatenate([ctx, x], axis=1)
    for l in range(depth):
        w_l = w_in[l]
        whg = w_l[:, :hg_cols].astype(BF16)
        wda = w_l[:, hg_cols:hg_cols + da_cols].astype(BF16)
        wrw = jnp.pad(w_l[:, hg_cols + da_cols:], ((0, 0), (0, rw_pad - rw_cols))).astype(BF16)
        u_hg, u_da, u_rw = _in_proj(s, mods[l], norm1_g[l][None], cos_t, sin_t, whg, wda, wrw, n_ctx_tiles)

        o_b = _hgrn_pass(u_hg, lower_bounds[1, l][None], n_ctx_chunks, True)
        o_hg = _hgrn_pass(u_hg, lower_bounds[0, l][None], n_ctx_chunks, False, other=o_b,
                          norm_g_row=jnp.tile(hg_norm_g[l], N_HEADS)[None])

        lam_init = 0.8 - 0.6 * math.exp(-0.3 * l)
        lam_rows = jnp.zeros((8, LANE), F32).at[0:4, :da_lam_q1.shape[-1]].set(
            jnp.stack([da_lam_q1[l], da_lam_k1[l], da_lam_q2[l], da_lam_k2[l]]))
        o_da = _attention(u_da, lam_rows, da_norm_g[l][None], lam_init, n_ctx)

        mu_pad = lambda m: jnp.pad(m, (0, rw_pad - rw_cols))[None]
        off_dec, off_icl = 0, 2 * r_dec
        rw_p = {
            "mu_prev": mu_pad(rw_mu_prev[l]), "mu_next": mu_pad(rw_mu_next[l]),
            "w0": rw_w0[:, l][:, None, :], "a0": rw_a0[:, l][:, None, :],
            "w2": jnp.stack([_pad_rows(rw_w2[i, l], off_dec + i * r_dec, LANE) for i in range(2)]),
            "a2": jnp.stack([_pad_rows(rw_a2[i, l], off_icl + i * r_icl, LANE) for i in range(2)]),
            "g2": _pad_rows(rw_g2[l], 0, LANE),
            "k_k": rw_k_k[l][None], "k_a": rw_k_a[l][None], "r_k": rw_r_k[l][None],
            "ln_w": rw_ln_w[l][None], "ln_b": rw_ln_b[l][None],
        }
        yb = _rwkv_pass(u_rw, rw_p, n_ctx_chunks, True)
        o_rw = _rwkv_pass(u_rw, rw_p, n_ctx_chunks, False, other=yb)

        wo = w_out[l].astype(BF16)
        x_mid, h2 = _out_proj(s, o_hg, o_da, o_rw, mods[l], norm2_g[l][None],
                              wo[:hg_w], wo[hg_w:hg_w + da_w], wo[hg_w + da_w:], n_ctx_tiles)
        s = _ffn(x_mid, h2, mods[l], ffn_w_gate[l].astype(BF16), ffn_w_up[l].astype(BF16),
                 ffn_w_down[l].astype(BF16), n_ctx_tiles)
    return _final_norm(s, final_norm_g[None], n_ctx)
```

```python
import functools
import math

import numpy as np
import jax
import jax.numpy as jnp
from jax import lax
from jax.experimental import pallas as pl
from jax.experimental.pallas import tpu as pltpu

F32 = jnp.float32
BF16 = jnp.bfloat16
HI = lax.Precision.HIGHEST

N_HEADS = 4
GRID_W = 64
ROPE_BASE = 10000.0
RMS_EPS = 1e-6
RW_GN_EPS = 64e-5

ROW_TILE = 256
CHUNK = 64
REC_BATCH = 8
LANE = 128
VMEM_LIMIT = 56 * 1024 * 1024


def _cparams(sem):
    return pltpu.CompilerParams(dimension_semantics=sem, vmem_limit_bytes=VMEM_LIMIT)


def _dot_dims(a, b, dims, prec):
    dg = lambda x, y, p=None: lax.dot_general(x, y, (dims, ((), ())), preferred_element_type=F32, precision=p)
    if prec == "bf16":
        return dg(a.astype(BF16), b.astype(BF16))
    if prec == "x3":
        a_hi, b_hi = a.astype(BF16), b.astype(BF16)
        a_lo = (a - a_hi.astype(F32)).astype(BF16)
        b_lo = (b - b_hi.astype(F32)).astype(BF16)
        return dg(a_hi, b_hi) + (dg(a_lo, b_hi) + dg(a_hi, b_lo))
    return dg(a, b, prec)


def _dot(a, b, prec=None):
    return _dot_dims(a, b, ((1,), (0,)), prec)


def _dot_nt(a, b, prec=None):
    return _dot_dims(a, b, ((1,), (1,)), prec)


def _dot_tn(a, b, prec=None):
    return _dot_dims(a, b, ((0,), (0,)), prec)


P_HG = "bf16"
P_INV = "bf16"
P_ST = "bf16"


def _iota(shape, dim):
    return lax.broadcasted_iota(jnp.int32, shape, dim)


def _div(x, d):
    assert d & (d - 1) == 0
    return x >> (d.bit_length() - 1)


def _mod(x, d):
    assert d & (d - 1) == 0
    return x & (d - 1)


def _softplus(x):
    return jnp.maximum(x, 0.0) + jnp.log(1.0 + jnp.exp(-jnp.abs(x)))


def _same_head(w, dh):
    return _div(_iota((w, w), 0), dh) == _div(_iota((w, w), 1), dh)


def _expand_heads(x, dh):
    w = x.shape[-1]
    head = _div(_iota((1, w), 1), dh)
    return jnp.concatenate([jnp.where(head == h, x, 0.0) for h in range(w // dh)], axis=0)


def _collapse_heads(x, c):
    out = x[0:c]
    for h in range(1, x.shape[0] // c):
        out = out + x[h * c:(h + 1) * c]
    return out


def _split3(x):
    hi = x.astype(BF16)
    r = x - hi.astype(F32)
    mid = r.astype(BF16)
    lo = (r - mid.astype(F32)).astype(BF16)
    return hi, mid, lo


def _split2(x):
    hi = x.astype(BF16)
    return hi, (x - hi.astype(F32)).astype(BF16)


def _head_sums(xs, dh):
    w = xs[0].shape[-1]
    ones_bd = _same_head(w, dh).astype(BF16)
    out = _dot(jnp.concatenate([p for x in xs for p in _split3(x)], axis=0), ones_bd)
    res, off = [], 0
    for x in xs:
        r = x.shape[0]
        res.append(out[off:off + r] + (out[off + r:off + 2 * r] + out[off + 2 * r:off + 3 * r]))
        off += 3 * r
    return res


def _dot_sel(sel3, x):
    return _dot(sel3, jnp.concatenate(_split3(x), axis=0))


def _dot_x3(x, w_hi, w_lo):
    x_hi, x_lo = _split2(x)
    r = x.shape[0]
    t = _dot(jnp.concatenate([x_hi, x_lo], axis=0), w_hi)
    return t[0:r] + t[r:2 * r] + _dot(x_hi, w_lo)


def _sigmoid(x):
    return 1.0 / (1.0 + jnp.exp(-x))


def _ada_kernel(c_ref, w_ref, b_ref, o_ref):
    c = c_ref[...]
    sc = c * _sigmoid(c)
    o_ref[...] = _dot(sc, w_ref[...], HI) + b_ref[...]


def _ada_mod(cond, ada_w, ada_b):
    depth, d, d6 = ada_w.shape
    rows = cond.shape[0]
    tn = 1536
    return pl.pallas_call(
        _ada_kernel,
        grid=(depth, d6 // tn),
        in_specs=[pl.BlockSpec((rows, d), lambda l, j: (0, 0)),
                  pl.BlockSpec((None, d, tn), lambda l, j: (l, 0, j)),
                  pl.BlockSpec((None, 1, tn), lambda l, j: (l, 0, j))],
        out_specs=pl.BlockSpec((None, rows, tn), lambda l, j: (l, 0, j)),
        out_shape=jax.ShapeDtypeStruct((depth, rows, d6), F32),
        compiler_params=_cparams(("arbitrary", "arbitrary")),
        name="ada_mod",
    )(cond, ada_w, ada_b.reshape(depth, 1, d6))


def _rope_cols(t, cos, sin_signed, half):
    lane = _iota((1, t.shape[-1]), 1)
    first = _mod(lane, 2 * half) < half
    swapped = jnp.where(first, pltpu.roll(t, t.shape[-1] - half, 1), pltpu.roll(t, half, 1))
    return t * cos + swapped * sin_signed


def _in_proj_kernel(s_ref, mod_ref, g_ref, cos_ref, sin_ref, whg_ref, wda_ref, wrw_ref,
                    uhg_ref, uda_ref, urw_ref, *, da_width, q_scale):
    x = s_ref[...]
    ms = jnp.mean(x * x, axis=-1, keepdims=True)
    y = x * lax.rsqrt(ms + RMS_EPS) * g_ref[...]
    h = (y * (1.0 + mod_ref[1:2, :]) + mod_ref[0:1, :]).astype(BF16)
    uhg_ref[...] = _dot(h, whg_ref[...])
    urw_ref[...] = _dot(h, wrw_ref[...])
    uda = _dot(h, wda_ref[...])
    cos = cos_ref[...]
    sin = sin_ref[...]
    dqk = da_width // (2 * N_HEADS)
    for j in range(2 * da_width // LANE):
        t = _rope_cols(uda[:, j * LANE:(j + 1) * LANE], cos, sin, dqk // 2)
        if j < da_width // LANE:
            t = t * q_scale
        uda_ref[:, j * LANE:(j + 1) * LANE] = t.astype(BF16)
    uda_ref[:, 2 * da_width:] = uda[:, 2 * da_width:].astype(BF16)


def _in_proj(s, mods_l, g, cos_t, sin_t, whg, wda, wrw, n_ctx_tiles):
    b, n, d = s.shape
    tm = ROW_TILE
    da_width = wda.shape[1] // 3
    dqk = da_width // (2 * N_HEADS)
    sel = lambda bi, i: (jnp.where(i < n_ctx_tiles, b, bi), 0, 0)
    full = lambda a: pl.BlockSpec(a.shape, lambda bi, i: (0, 0))
    row = lambda w: pl.BlockSpec((None, tm, w), lambda bi, i: (bi, i, 0))
    kern = functools.partial(_in_proj_kernel, da_width=da_width, q_scale=dqk ** -0.5)
    return pl.pallas_call(
        kern,
        grid=(b, n // tm),
        in_specs=[row(d), pl.BlockSpec((None, 8, d), sel), full(g),
                  pl.BlockSpec((tm, LANE), lambda bi, i: (i, 0)),
                  pl.BlockSpec((tm, LANE), lambda bi, i: (i, 0)),
                  full(whg), full(wda), full(wrw)],
        out_specs=[row(whg.shape[1]), row(wda.shape[1]), row(wrw.shape[1])],
        out_shape=[jax.ShapeDtypeStruct((b, n, whg.shape[1]), F32),
                   jax.ShapeDtypeStruct((b, n, wda.shape[1]), BF16),
                   jax.ShapeDtypeStruct((b, n, wrw.shape[1]), F32)],
        compiler_params=_cparams(("parallel", "parallel")),
        name="in_proj",
    )(s, mods_l, g, cos_t, sin_t, whg, wda, wrw)


def _chunk_of_step(j, n_chunks, n_ctx_chunks, rev):
    if not rev:
        return j
    return jnp.where(j < n_ctx_chunks, n_ctx_chunks - 1 - j, n_chunks - 1 - (j - n_ctx_chunks))


def _hgrn_levels(c):
    return [1 << i for i in range(int(math.log2(c)))]


def _hgrn_const(c, rev):
    t = np.arange(c)[:, None]
    s = np.arange(c)[None, :]
    tri = (s >= t) if rev else (s <= t)
    mats = [tri.astype(np.float32)]
    for m in _hgrn_levels(c):
        p = (t // (2 * m)) * (2 * m) + (m if rev else m - 1)
        ref = (s >= p) if rev else (s <= p)
        mats.append(tri.astype(np.float32) - ref.astype(np.float32))
    return np.concatenate(mats, axis=0)


def _hgrn_kernel(*refs, rev, finish, dh):
    if finish:
        const_ref, q_ref, i_ref, f_ref, lb_ref, g_ref, other_ref, ng_ref, out_ref, st_ref = refs
    else:
        const_ref, q_ref, i_ref, f_ref, lb_ref, out_ref, st_ref = refs
    nb, c, w = q_ref.shape

    @pl.when(pl.program_id(1) == 0)
    def _():
        st_ref[...] = jnp.zeros_like(st_ref)

    nh = w // dh
    t_idx = _iota((c, 1), 0)
    s_idx = _mod(_iota((1, nh * c), 1), c)
    bd = _same_head(w, dh)
    lb = lb_ref[...]
    rows = lambda ref: ref[...].reshape(nb * c, w)
    q_all = rows(q_ref)
    v_all = rows(i_ref)
    f_all = lb + (1.0 - lb) * _sigmoid(rows(f_ref))
    kx_all = 1.0 - f_all
    logf_all = jnp.log(f_all)
    (diag_all,) = _head_sums([q_all * kx_all], dh)
    bf = lambda t: t.astype(BF16)
    cs = []
    for bb in range(nb):
        sl = slice(bb * c, (bb + 1) * c)
        seg = _dot_sel(const_ref[...], logf_all[sl])
        bcum = seg[0:c]
        cs.append(dict(bb=bb, q=q_all[sl], v=v_all[sl], kx=kx_all[sl], diag=diag_all[sl], seg=seg, bcum=bcum,
                       bend=bcum[0:1] if rev else bcum[c - 1:c], att=jnp.zeros((c, nh * c), F32)))
    for li, m in enumerate(_hgrn_levels(c)):
        t_up = (t_idx & m) != 0
        s_up = (s_idx & m) != 0
        t_isq = jnp.logical_not(t_up) if rev else t_up
        s_isk = s_up if rev else jnp.logical_not(s_up)
        valid = (_div(t_idx, 2 * m) == _div(s_idx, 2 * m)) & t_isq & s_isk
        for s in cs:
            e = jnp.exp(-jnp.abs(s["seg"][(li + 1) * c:(li + 2) * c]))
            xm = bf(jnp.where(t_isq, s["q"], s["kx"]) * e)
            sc = _dot_nt(xm, _expand_heads(xm, dh))
            s["att"] = s["att"] + jnp.where(valid, sc, 0.0)
    outs = []
    for s in cs:
        q, v, kx, bcum, bend = s["q"], s["v"], s["kx"], s["bcum"], s["bend"]
        o = _dot(bf(s["att"]), _expand_heads(bf(v), dh))
        o = o + s["diag"] * v
        st = st_ref[s["bb"]]
        o = o + _dot_nt(q * jnp.exp(bcum), st, P_HG)
        upd = _dot_tn(v, kx * jnp.exp(bend - bcum), P_HG)
        st_ref[s["bb"]] = st * jnp.exp(bend) + jnp.where(bd, upd, 0.0)
        outs.append(o)
    o_all = jnp.concatenate(outs, axis=0)
    if finish:
        o_all = o_all + rows(other_ref)
        (ms,) = _head_sums([o_all * o_all], dh)
        g = rows(g_ref)
        o_all = o_all * lax.rsqrt(ms * (1.0 / dh) + RMS_EPS) * ng_ref[...] * (g * _sigmoid(g))
    out_ref[...] = o_all.reshape(nb, c, w)


def _hgrn_pass(u_hg, lb_row, n_ctx_chunks, rev, other=None, norm_g_row=None):
    b, n, _ = u_hg.shape
    c = CHUNK
    w = lb_row.shape[-1]
    dh = w // N_HEADS
    nc = n // c
    finish = other is not None
    const = jnp.asarray(np.tile(_hgrn_const(c, rev), (1, 3)), BF16)
    chunk = lambda j: _chunk_of_step(j, nc, n_ctx_chunks, rev)
    nb = math.gcd(b, REC_BATCH)
    col = lambda k: pl.BlockSpec((nb, c, w), lambda bi, j: (bi, chunk(j), k))
    vec = pl.BlockSpec((1, w), lambda bi, j: (0, 0))
    in_specs = [pl.BlockSpec(const.shape, lambda bi, j: (0, 0)), col(0), col(1), col(3 if rev else 2), vec]
    args = [const, u_hg, u_hg, u_hg, lb_row]
    if finish:
        in_specs += [col(4), col(0), vec]
        args += [u_hg, other, norm_g_row]
    return pl.pallas_call(
        functools.partial(_hgrn_kernel, rev=rev, finish=finish, dh=dh),
        grid=(b // nb, nc),
        in_specs=in_specs,
        out_specs=col(0),
        out_shape=jax.ShapeDtypeStruct((b, n, w), F32),
        scratch_shapes=[pltpu.VMEM((nb, w, w), F32)],
        compiler_params=_cparams(("parallel", "arbitrary")),
        name="hgrn_bwd" if rev else "hgrn_fwd",
    )(*args)


def _rwkv_kernel(*refs, rev, finish, dh, n_chunks, n_ctx_chunks):
    if finish:
        (tri_ref, z_ref, zp_ref, zn_ref, mup_ref, mun_ref, w0_ref, w2_ref, a0_ref, a2_ref, g2_ref,
         kk_ref, ka_ref, rk_ref, oy_ref, ob_ref, lnw_ref, lnb_ref, out_ref, st_ref) = refs
    else:
        (tri_ref, z_ref, zp_ref, zn_ref, mup_ref, mun_ref, w0_ref, w2_ref, a0_ref, a2_ref, g2_ref,
         kk_ref, ka_ref, rk_ref, y_ref, bonus_ref, st_ref) = refs
    nb, c, _ = z_ref.shape
    w = w0_ref.shape[-1]
    j = pl.program_id(1)
    chunk = _chunk_of_step(j, n_chunks, n_ctx_chunks, rev)

    @pl.when(j == 0)
    def _():
        st_ref[...] = jnp.zeros_like(st_ref)

    row = _iota((c, 1), 0)
    seg_first = (chunk == 0) | (chunk == n_ctx_chunks)
    seg_last = (chunk == n_ctx_chunks - 1) | (chunk == n_chunks - 1)
    nr = (w // dh) * c
    si = _mod(_iota((1, nr), 1), c)
    strict = (si > row) if rev else (si < row)
    incl = (si >= row) if rev else (si <= row)
    eye = (_iota((nr, nr), 0) == _iota((nr, nr), 1)).astype(F32)
    blk = _same_head(nr, c)
    bd = _same_head(w, dh)
    eye_w = (_iota((w, w), 0) == _iota((w, w), 1)).astype(F32)

    us = []
    for bb in range(nb):
        z = z_ref[bb]
        prev_row = jnp.where(seg_first, 0.0, zp_ref[bb, 7:8, :])
        next_row = jnp.where(seg_last, 0.0, zn_ref[bb, 0:1, :])
        z_prev = jnp.where(row == 0, prev_row, pltpu.roll(z, 1, 0))
        z_next = jnp.where(row == c - 1, next_row, pltpu.roll(z, c - 1, 0))
        us.append(z + mup_ref[...] * (z_prev - z) + mun_ref[...] * (z_next - z))
    u = jnp.concatenate(us, axis=0)

    r_all = u[:, 0:w]
    k_all = u[:, w:2 * w]
    v_all = u[:, 2 * w:3 * w]
    lora = u[:, 3 * w:3 * w + LANE]
    gd = u[:, 3 * w + LANE:3 * w + 2 * LANE]

    w_log = -_softplus(-(w0_ref[...] + _dot_x3(jnp.tanh(lora), w2_ref[0], w2_ref[1]))) - 0.5
    lw_all = -jnp.exp(w_log)
    a_lr = _sigmoid(a0_ref[...] + _dot_x3(lora, a2_ref[0], a2_ref[1]))
    kk = k_all * kk_ref[...]
    kd_all = k_all * (1.0 + (a_lr - 1.0) * ka_ref[...])
    kk_sq, bonus_dot = _head_sums([kk * kk, r_all * kd_all * rk_ref[...]], dh)
    kk = kk * lax.rsqrt(jnp.maximum(kk_sq, 1e-24))
    a_all = -kk
    b_all = kk * a_lr
    bonus_all = bonus_dot * v_all

    chains = range(nb)
    bf = lambda t: t.astype(BF16)
    ex = lambda t: _expand_heads(bf(t), dh)
    st = []
    for bb in chains:
        sl = slice(bb * c, (bb + 1) * c)
        r, v, kd, lw = r_all[sl], v_all[sl], kd_all[sl], lw_all[sl]
        cum = _dot_sel(tri_ref[...], lw)
        tot = cum[0:1] if rev else cum[c - 1:c]
        inv_gam = jnp.exp(-cum)
        rest = jnp.exp(tot - cum)
        r_t = r * jnp.exp(cum)
        a_t = a_all[sl] * jnp.exp(cum - lw)
        ea, eb, ek, ev = ex(a_t), ex(b_all[sl] * inv_gam), ex(kd * inv_gam), ex(v)
        sc = _dot_nt(jnp.concatenate([bf(a_t), bf(r_t)], axis=0), jnp.concatenate([eb, ek], axis=0))
        st.append(dict(
            v=v, r_t=r_t, tot=tot, ea=ea, ev=ev, b_h=b_all[sl] * rest, k_h=kd * rest,
            a_ab=jnp.where(strict, sc[0:c, 0:nr], 0.0), a_ak=jnp.where(strict, sc[0:c, nr:2 * nr], 0.0),
            a_rb=jnp.where(incl, sc[c:2 * c, 0:nr], 0.0), a_rk=jnp.where(incl, sc[c:2 * c, nr:2 * nr], 0.0)))

    for s in st:
        s["pw"] = jnp.where(blk, jnp.concatenate([s["a_ab"]] * (nr // c), axis=0), 0.0)
        s["tinv"] = eye + s["pw"]
    for _ in range(int(math.log2(c)) - 1):
        for s in st:
            s["pw"] = _dot(s["pw"], s["pw"], P_INV)
        for s in st:
            s["tinv"] = s["tinv"] + _dot(s["tinv"], s["pw"], P_INV)

    for s in st:
        s["t_cat"] = bf(_collapse_heads(s["tinv"], c))
        s["x_c"] = _dot(bf(s["a_ak"]), s["ev"])
    for s in st:
        s["w1c"] = _dot(s["t_cat"], s["ea"])
        s["w2c"] = _dot(s["t_cat"], ex(s["x_c"]))
    for s in st:
        a_rb = bf(s["a_rb"])
        s["p_mat"] = s["r_t"] + _dot(a_rb, ex(s["w1c"]))
        s["y0"] = _dot(jnp.concatenate([a_rb, bf(s["a_rk"])], axis=1),
                       jnp.concatenate([ex(s["w2c"]), s["ev"]], axis=0))
    for s in st:
        s["m_mat"] = eye_w * jnp.exp(s["tot"]) + jnp.where(bd, _dot_tn(s["b_h"], s["w1c"], P_ST), 0.0)
        s["g_mat"] = jnp.where(bd, _dot_tn(jnp.concatenate([s["b_h"], s["k_h"]], axis=0),
                                           jnp.concatenate([s["w2c"], s["v"]], axis=0), P_ST), 0.0)
    ys = []
    for bb, s in zip(chains, st):
        h0 = st_ref[bb]
        ys.append(_dot(s["p_mat"], h0, P_ST) + s["y0"])
        st_ref[bb] = _dot(s["m_mat"], h0, P_ST) + s["g_mat"]

    y = jnp.concatenate(ys, axis=0)
    if finish:
        y = y + oy_ref[...].reshape(nb * c, w)
        (mean,) = _head_sums([y], dh)
        yc = y - mean * (1.0 / dh)
        (var,) = _head_sums([yc * yc], dh)
        yn = yc * lax.rsqrt(var * (1.0 / dh) + RW_GN_EPS) * lnw_ref[...] + lnb_ref[...]
        gate = _dot_x3(_sigmoid(gd), g2_ref[0], g2_ref[1])
        out = (yn + bonus_all + ob_ref[...].reshape(nb * c, w)) * gate
        out_ref[...] = out.reshape(nb, c, w)
    else:
        y_ref[...] = y.reshape(nb, c, w)
        bonus_ref[...] = bonus_all.reshape(nb, c, w)


def _rwkv_pass(u_rw, p, n_ctx_chunks, rev, other=None):
    b, n, wu = u_rw.shape
    c = CHUNK
    w = p["w0"].shape[-1]
    dh = w // N_HEADS
    nc = n // c
    finish = other is not None
    t = np.arange(c)
    tri = (t[None, :] >= t[:, None]) if rev else (t[None, :] <= t[:, None])
    tri = jnp.asarray(np.tile(tri.astype(np.float32), (1, 3)), BF16)
    hi_lo = lambda m: jnp.stack(_split2(m))
    d = 1 if rev else 0
    chunk = lambda j: _chunk_of_step(j, nc, n_ctx_chunks, rev)
    sub = c // 8
    nb = math.gcd(b, REC_BATCH)
    const = lambda a: pl.BlockSpec(a.shape, lambda bi, j: (0,) * a.ndim)
    col = pl.BlockSpec((nb, c, w), lambda bi, j: (bi, chunk(j), 0))
    args = [tri, u_rw, u_rw, u_rw, p["mu_prev"], p["mu_next"], p["w0"][d], hi_lo(p["w2"][d]), p["a0"][d],
            hi_lo(p["a2"][d]), hi_lo(p["g2"]), p["k_k"], p["k_a"], p["r_k"]]
    in_specs = [const(tri),
                pl.BlockSpec((nb, c, wu), lambda bi, j: (bi, chunk(j), 0)),
                pl.BlockSpec((nb, 8, wu), lambda bi, j: (bi, jnp.maximum(chunk(j) * sub - 1, 0), 0)),
                pl.BlockSpec((nb, 8, wu), lambda bi, j: (bi, jnp.minimum((chunk(j) + 1) * sub, n // 8 - 1), 0))]
    in_specs += [const(a) for a in args[4:]]
    if finish:
        args += [other[0], other[1], p["ln_w"], p["ln_b"]]
        in_specs += [col, col, const(p["ln_w"]), const(p["ln_b"])]
        out_specs, out_shape = col, jax.ShapeDtypeStruct((b, n, w), F32)
    else:
        out_specs = [col, col]
        out_shape = [jax.ShapeDtypeStruct((b, n, w), F32)] * 2
    return pl.pallas_call(
        functools.partial(_rwkv_kernel, rev=rev, finish=finish, dh=dh, n_chunks=nc, n_ctx_chunks=n_ctx_chunks),
        grid=(b // nb, nc),
        in_specs=in_specs,
        out_specs=out_specs,
        out_shape=out_shape,
        scratch_shapes=[pltpu.VMEM((nb, w, w), F32)],
        compiler_params=_cparams(("parallel", "arbitrary")),
        name="rwkv_bwd" if rev else "rwkv_fwd",
    )(*args)


def _attn_kernel(lam_ref, q_ref, k_ref, v_ref, ng_ref, o_ref, *, lam_init, n_ctx, n_ctx_tiles):
    lp = lam_ref[...]
    lam = (jnp.exp(jnp.sum(lp[0:1] * lp[1:2], axis=-1, keepdims=True))
           - jnp.exp(jnp.sum(lp[2:3] * lp[3:4], axis=-1, keepdims=True)) + lam_init)

    def attend(k, v):
        q = q_ref[...]
        half = q.shape[-1] // 2
        lane = _iota((1, q.shape[-1]), 1)

        def one_map(qm):
            s = _dot_nt(qm, k)
            p = jnp.exp(s - jnp.max(s, axis=-1, keepdims=True))
            return _dot(p.astype(BF16), v) / jnp.sum(p, axis=-1, keepdims=True)

        o = one_map(jnp.where(lane < half, q, 0)) - lam * one_map(jnp.where(lane >= half, q, 0))
        ms = jnp.mean(o * o, axis=-1, keepdims=True)
        o_ref[...] = o * lax.rsqrt(ms + RMS_EPS) * ng_ref[...] * (1.0 - lam_init)

    is_ctx = pl.program_id(2) < n_ctx_tiles

    @pl.when(is_ctx)
    def _():
        attend(k_ref[0:n_ctx, :], v_ref[0:n_ctx, :])

    @pl.when(jnp.logical_not(is_ctx))
    def _():
        attend(k_ref[...], v_ref[...])


def _attention(u_da, lam_rows, norm_g_row, lam_init, n_ctx):
    b, n, w3 = u_da.shape
    wd = w3 // 3
    dv = wd // N_HEADS
    tq = ROW_TILE
    kern = functools.partial(_attn_kernel, lam_init=lam_init, n_ctx=n_ctx, n_ctx_tiles=n_ctx // tq)
    return pl.pallas_call(
        kern,
        grid=(b, N_HEADS, n // tq),
        in_specs=[pl.BlockSpec(lam_rows.shape, lambda bi, h, i: (0, 0)),
                  pl.BlockSpec((None, tq, dv), lambda bi, h, i: (bi, i, h)),
                  pl.BlockSpec((None, n, dv), lambda bi, h, i: (bi, 0, N_HEADS + h)),
                  pl.BlockSpec((None, n, dv), lambda bi, h, i: (bi, 0, 2 * N_HEADS + h)),
                  pl.BlockSpec((1, dv), lambda bi, h, i: (0, 0))],
        out_specs=pl.BlockSpec((None, tq, dv), lambda bi, h, i: (bi, i, h)),
        out_shape=jax.ShapeDtypeStruct((b, n, wd), F32),
        compiler_params=_cparams(("parallel", "parallel", "arbitrary")),
        name="diff_attn",
    )(lam_rows, u_da, u_da, u_da, norm_g_row)


def _out_proj_kernel(s_ref, hg_ref, da_ref, rw_ref, mod_ref, g_ref, whg_ref, wda_ref, wrw_ref,
                     x_ref, h_ref):
    mix = (_dot(hg_ref[...].astype(BF16), whg_ref[...]) + _dot(da_ref[...].astype(BF16), wda_ref[...])
           + _dot(rw_ref[...].astype(BF16), wrw_ref[...]))
    x = s_ref[...] + mod_ref[2:3, :] * mix
    x_ref[...] = x
    ms = jnp.mean(x * x, axis=-1, keepdims=True)
    y = x * lax.rsqrt(ms + RMS_EPS) * g_ref[...]
    h_ref[...] = (y * (1.0 + mod_ref[4:5, :]) + mod_ref[3:4, :]).astype(BF16)


def _out_proj(s, o_hg, o_da, o_rw, mods_l, g, whg, wda, wrw, n_ctx_tiles):
    b, n, d = s.shape
    tm = ROW_TILE
    sel = lambda bi, i: (jnp.where(i < n_ctx_tiles, b, bi), 0, 0)
    full = lambda a: pl.BlockSpec(a.shape, lambda bi, i: (0, 0))
    row = lambda w: pl.BlockSpec((None, tm, w), lambda bi, i: (bi, i, 0))
    return pl.pallas_call(
        _out_proj_kernel,
        grid=(b, n // tm),
        in_specs=[row(d), row(o_hg.shape[-1]), row(o_da.shape[-1]), row(o_rw.shape[-1]),
                  pl.BlockSpec((None, 8, d), sel), full(g), full(whg), full(wda), full(wrw)],
        out_specs=[row(d), row(d)],
        out_shape=[jax.ShapeDtypeStruct((b, n, d), F32), jax.ShapeDtypeStruct((b, n, d), BF16)],
        compiler_params=_cparams(("parallel", "parallel")),
        name="out_proj",
    )(s, o_hg, o_da, o_rw, mods_l, g, whg, wda, wrw)


def _ffn_kernel(x_ref, h_ref, mod_ref, wg_ref, wu_ref, wd_ref, o_ref):
    h = h_ref[...]
    gate = _dot(h, wg_ref[...])
    up = _dot(h, wu_ref[...])
    act = (gate * _sigmoid(gate) * up).astype(BF16)
    o_ref[...] = x_ref[...] + mod_ref[5:6, :] * _dot(act, wd_ref[...])


def _ffn(x, h, mods_l, wg, wu, wd, n_ctx_tiles):
    b, n, d = x.shape
    tm = ROW_TILE
    sel = lambda bi, i: (jnp.where(i < n_ctx_tiles, b, bi), 0, 0)
    resident = lambda a: pl.BlockSpec(a.shape, lambda bi, i: (0, 0), pipeline_mode=pl.Buffered(1))
    row = pl.BlockSpec((None, tm, d), lambda bi, i: (bi, i, 0))
    return pl.pallas_call(
        _ffn_kernel,
        grid=(b, n // tm),
        in_specs=[row, row, pl.BlockSpec((None, 8, d), sel), resident(wg), resident(wu), resident(wd)],
        out_specs=row,
        out_shape=jax.ShapeDtypeStruct((b, n, d), F32),
        compiler_params=_cparams(("parallel", "parallel")),
        name="ffn",
    )(x, h, mods_l, wg, wu, wd)


def _final_norm_kernel(x_ref, g_ref, o_ref):
    x = x_ref[...]
    ms = jnp.mean(x * x, axis=-1, keepdims=True)
    o_ref[...] = x * lax.rsqrt(ms + RMS_EPS) * g_ref[...]


def _final_norm(s, g, n_ctx):
    b, n, d = s.shape
    tm = ROW_TILE
    off = n_ctx // tm
    return pl.pallas_call(
        _final_norm_kernel,
        grid=(b, (n - n_ctx) // tm),
        in_specs=[pl.BlockSpec((None, tm, d), lambda bi, i: (bi, i + off, 0)),
                  pl.BlockSpec((1, d), lambda bi, i: (0, 0))],
        out_specs=pl.BlockSpec((None, tm, d), lambda bi, i: (bi, i, 0)),
        out_shape=jax.ShapeDtypeStruct((b, n - n_ctx, d), F32),
        compiler_params=_cparams(("parallel", "parallel")),
        name="final_norm",
    )(s, g)


def _rope_tables(n_ctx, n_lat, dqk):
    n_freq = dqk // 4
    t = np.arange(n_lat)
    inv_freq = ROPE_BASE ** (-np.arange(n_freq, dtype=np.float32) / n_freq)
    ang = np.concatenate([(t // GRID_W)[:, None] * inv_freq, (t % GRID_W)[:, None] * inv_freq], axis=-1)
    reps = LANE // dqk
    cos = np.tile(np.concatenate([np.cos(ang), np.cos(ang)], axis=-1), (1, reps))
    sin = np.tile(np.concatenate([-np.sin(ang), np.sin(ang)], axis=-1), (1, reps))
    cos = np.concatenate([np.ones((n_ctx, LANE)), cos], axis=0)
    sin = np.concatenate([np.zeros((n_ctx, LANE)), sin], axis=0)
    return jnp.asarray(cos, F32), jnp.asarray(sin, F32)


def _pad_rows(a, start, total):
    return jnp.zeros((total, a.shape[-1]), a.dtype).at[start:start + a.shape[0]].set(a)


def kernel(x, c, ctx, c_ctx, ada_w, ada_b, norm1_g, norm2_g, w_in, w_out, hg_lb_logits, hg_norm_g, da_lam_q1, da_lam_k1, da_lam_q2, da_lam_k2, da_norm_g, rw_mu_prev, rw_mu_next, rw_w0, rw_w2, rw_a0, rw_a2, rw_g2, rw_k_k, rw_k_a, rw_r_k, rw_ln_w, rw_ln_b, ffn_w_gate, ffn_w_up, ffn_w_down, final_norm_g):
    b, n_lat, d = x.shape
    n_ctx = ctx.shape[1]
    depth = ada_w.shape[0]
    hg_w = hg_lb_logits.shape[-1]
    rw_w = rw_w0.shape[-1]
    da_w = w_out.shape[1] - hg_w - rw_w
    hg_cols, da_cols = 5 * hg_w, 3 * da_w
    rw_cols = w_in.shape[-1] - hg_cols - da_cols
    rw_pad = 3 * rw_w + 2 * LANE
    r_dec, r_icl, r_gate = rw_w2.shape[2], rw_a2.shape[2], rw_g2.shape[1]
    assert n_ctx % ROW_TILE == 0 and n_lat % ROW_TILE == 0 and n_lat % GRID_W == 0
    assert 2 * (r_dec + r_icl) == LANE and r_gate <= LANE and rw_cols == 3 * rw_w + LANE + r_gate
    n_ctx_tiles, n_ctx_chunks = n_ctx // ROW_TILE, n_ctx // CHUNK

    cond = jnp.zeros((16, d), F32).at[:b].set(c).at[b].set(c_ctx)
    mods = _ada_mod(cond, ada_w, ada_b).reshape(depth, 16, 6, d)
    mods = jnp.pad(mods, ((0, 0), (0, 0), (0, 2), (0, 0)))

    p_lb = jax.nn.softmax(hg_lb_logits.astype(F32), axis=1)
    lower_bounds = jnp.cumsum(p_lb, axis=1) - p_lb[:, :1]
    cos_t, sin_t = _rope_tables(n_ctx, n_lat, da_w // (2 * N_HEADS))

    s = jnp.concatenate([ctx, x], axis=1)
    for l in range(depth):
        w_l = w_in[l]
        whg = w_l[:, :hg_cols].astype(BF16)
        wda = w_l[:, hg_cols:hg_cols + da_cols].astype(BF16)
        wrw = jnp.pad(w_l[:, hg_cols + da_cols:], ((0, 0), (0, rw_pad - rw_cols))).astype(BF16)
        u_hg, u_da, u_rw = _in_proj(s, mods[l], norm1_g[l][None], cos_t, sin_t, whg, wda, wrw, n_ctx_tiles)

        o_b = _hgrn_pass(u_hg, lower_bounds[1, l][None], n_ctx_chunks, True)
        o_hg = _hgrn_pass(u_hg, lower_bounds[0, l][None], n_ctx_chunks, False, other=o_b,
                          norm_g_row=jnp.tile(hg_norm_g[l], N_HEADS)[None])

        lam_init = 0.8 - 0.6 * math.exp(-0.3 * l)
        lam_rows = jnp.zeros((8, LANE), F32).at[0:4, :da_lam_q1.shape[-1]].set(
            jnp.stack([da_lam_q1[l], da_lam_k1[l], da_lam_q2[l], da_lam_k2[l]]))
        o_da = _attention(u_da, lam_rows, da_norm_g[l][None], lam_init, n_ctx)

        mu_pad = lambda m: jnp.pad(m, (0, rw_pad - rw_cols))[None]
        off_dec, off_icl = 0, 2 * r_dec
        rw_p = {
            "mu_prev": mu_pad(rw_mu_prev[l]), "mu_next": mu_pad(rw_mu_next[l]),
            "w0": rw_w0[:, l][:, None, :], "a0": rw_a0[:, l][:, None, :],
            "w2": jnp.stack([_pad_rows(rw_w2[i, l], off_dec + i * r_dec, LANE) for i in range(2)]),
            "a2": jnp.stack([_pad_rows(rw_a2[i, l], off_icl + i * r_icl, LANE) for i in range(2)]),
            "g2": _pad_rows(rw_g2[l], 0, LANE),
            "k_k": rw_k_k[l][None], "k_a": rw_k_a[l][None], "r_k": rw_r_k[l][None],
            "ln_w": rw_ln_w[l][None], "ln_b": rw_ln_b[l][None],
        }
        yb = _rwkv_pass(u_rw, rw_p, n_ctx_chunks, True)
        o_rw = _rwkv_pass(u_rw, rw_p, n_ctx_chunks, False, other=yb)

        wo = w_out[l].astype(BF16)
        x_mid, h2 = _out_proj(s, o_hg, o_da, o_rw, mods[l], norm2_g[l][None],
                              wo[:hg_w], wo[hg_w:hg_w + da_w], wo[hg_w + da_w:], n_ctx_tiles)
        s = _ffn(x_mid, h2, mods[l], ffn_w_gate[l].astype(BF16), ffn_w_up[l].astype(BF16),
                 ffn_w_down[l].astype(BF16), n_ctx_tiles)
    return _final_norm(s, final_norm_g[None], n_ctx)
```

```python
import functools
import math

import numpy as np
import jax
import jax.numpy as jnp
from jax import lax
from jax.experimental import pallas as pl
from jax.experimental.pallas import tpu as pltpu

F32 = jnp.float32
BF16 = jnp.bfloat16
HI = lax.Precision.HIGHEST

N_HEADS = 4
GRID_W = 64
ROPE_BASE = 10000.0
RMS_EPS = 1e-6
RW_GN_EPS = 64e-5

ROW_TILE = 256
ATT_SUB = 128
CHUNK = 64
REC_BATCH = 8
LANE = 128
VMEM_LIMIT = 56 * 1024 * 1024


def _cparams(sem):
    return pltpu.CompilerParams(dimension_semantics=sem, vmem_limit_bytes=VMEM_LIMIT)


def _dot_dims(a, b, dims, prec):
    dg = lambda x, y, p=None: lax.dot_general(x, y, (dims, ((), ())), preferred_element_type=F32, precision=p)
    if prec == "bf16":
        return dg(a.astype(BF16), b.astype(BF16))
    if prec == "x3":
        a_hi, b_hi = a.astype(BF16), b.astype(BF16)
        a_lo = (a - a_hi.astype(F32)).astype(BF16)
        b_lo = (b - b_hi.astype(F32)).astype(BF16)
        return dg(a_hi, b_hi) + (dg(a_lo, b_hi) + dg(a_hi, b_lo))
    return dg(a, b, prec)


def _dot(a, b, prec=None):
    return _dot_dims(a, b, ((1,), (0,)), prec)


def _dot_nt(a, b, prec=None):
    return _dot_dims(a, b, ((1,), (1,)), prec)


def _dot_tn(a, b, prec=None):
    return _dot_dims(a, b, ((0,), (0,)), prec)


P_HG = "bf16"
P_ST = "bf16"


def _iota(shape, dim):
    return lax.broadcasted_iota(jnp.int32, shape, dim)


def _div(x, d):
    assert d & (d - 1) == 0
    return x >> (d.bit_length() - 1)


def _mod(x, d):
    assert d & (d - 1) == 0
    return x & (d - 1)


def _softplus(x):
    return jnp.maximum(x, 0.0) + jnp.log(1.0 + jnp.exp(-jnp.abs(x)))


def _same_head(w, dh):
    return _div(_iota((w, w), 0), dh) == _div(_iota((w, w), 1), dh)


def _expand_heads(x, dh):
    w = x.shape[-1]
    head = _div(_iota((1, w), 1), dh)
    return jnp.concatenate([jnp.where(head == h, x, 0.0) for h in range(w // dh)], axis=0)


def _collapse_heads(x, c):
    out = x[0:c]
    for h in range(1, x.shape[0] // c):
        out = out + x[h * c:(h + 1) * c]
    return out


def _split3(x):
    hi = x.astype(BF16)
    r = x - hi.astype(F32)
    mid = r.astype(BF16)
    lo = (r - mid.astype(F32)).astype(BF16)
    return hi, mid, lo


def _split2(x):
    hi = x.astype(BF16)
    return hi, (x - hi.astype(F32)).astype(BF16)


def _head_sums(xs, dh):
    w = xs[0].shape[-1]
    ones_bd = _same_head(w, dh).astype(BF16)
    out = _dot(jnp.concatenate([p for x in xs for p in _split3(x)], axis=0), ones_bd)
    res, off = [], 0
    for x in xs:
        r = x.shape[0]
        res.append(out[off:off + r] + (out[off + r:off + 2 * r] + out[off + 2 * r:off + 3 * r]))
        off += 3 * r
    return res


def _dot_sel(sel3, x):
    return _dot(sel3, jnp.concatenate(_split3(x), axis=0))


def _dot_x3(x, w_hi, w_lo):
    x_hi, x_lo = _split2(x)
    r = x.shape[0]
    t = _dot(jnp.concatenate([x_hi, x_lo], axis=0), w_hi)
    return t[0:r] + t[r:2 * r] + _dot(x_hi, w_lo)


def _sigmoid(x):
    return 1.0 / (1.0 + jnp.exp(-x))


def _ada_kernel(c_ref, w_ref, b_ref, o_ref):
    c = c_ref[...]
    sc = c * _sigmoid(c)
    o_ref[...] = _dot(sc, w_ref[...], HI) + b_ref[...]


def _ada_mod(cond, ada_w, ada_b):
    depth, d, d6 = ada_w.shape
    rows = cond.shape[0]
    tn = 1536
    return pl.pallas_call(
        _ada_kernel,
        grid=(depth, d6 // tn),
        in_specs=[pl.BlockSpec((rows, d), lambda l, j: (0, 0)),
                  pl.BlockSpec((None, d, tn), lambda l, j: (l, 0, j)),
                  pl.BlockSpec((None, 1, tn), lambda l, j: (l, 0, j))],
        out_specs=pl.BlockSpec((None, rows, tn), lambda l, j: (l, 0, j)),
        out_shape=jax.ShapeDtypeStruct((depth, rows, d6), F32),
        compiler_params=_cparams(("arbitrary", "arbitrary")),
        name="ada_mod",
    )(cond, ada_w, ada_b.reshape(depth, 1, d6))


def _rope_cols(t, cos, sin_signed, half):
    lane = _iota((1, t.shape[-1]), 1)
    first = _mod(lane, 2 * half) < half
    swapped = jnp.where(first, pltpu.roll(t, t.shape[-1] - half, 1), pltpu.roll(t, half, 1))
    return t * cos + swapped * sin_signed


def _in_proj_kernel(s_ref, mod_ref, g_ref, cos_ref, sin_ref, whg_ref, wda_ref, wrw_ref,
                    uhg_ref, uda_ref, urw_ref, *, da_width, q_scale):
    x = s_ref[...]
    ms = jnp.mean(x * x, axis=-1, keepdims=True)
    y = x * lax.rsqrt(ms + RMS_EPS) * g_ref[...]
    h = (y * (1.0 + mod_ref[1:2, :]) + mod_ref[0:1, :]).astype(BF16)
    uhg_ref[...] = _dot(h, whg_ref[...])
    urw_ref[...] = _dot(h, wrw_ref[...])
    uda = _dot(h, wda_ref[...])
    cos = cos_ref[...]
    sin = sin_ref[...]
    dqk = da_width // (2 * N_HEADS)
    for j in range(2 * da_width // LANE):
        t = _rope_cols(uda[:, j * LANE:(j + 1) * LANE], cos, sin, dqk // 2)
        if j < da_width // LANE:
            t = t * q_scale
        uda_ref[:, j * LANE:(j + 1) * LANE] = t.astype(BF16)
    uda_ref[:, 2 * da_width:] = uda[:, 2 * da_width:].astype(BF16)


def _in_proj(s, mods_l, g, cos_t, sin_t, whg, wda, wrw, n_ctx_tiles):
    b, n, d = s.shape
    tm = ROW_TILE
    da_width = wda.shape[1] // 3
    dqk = da_width // (2 * N_HEADS)
    sel = lambda bi, i: (jnp.where(i < n_ctx_tiles, b, bi), 0, 0)
    full = lambda a: pl.BlockSpec(a.shape, lambda bi, i: (0, 0))
    row = lambda w: pl.BlockSpec((None, tm, w), lambda bi, i: (bi, i, 0))
    kern = functools.partial(_in_proj_kernel, da_width=da_width, q_scale=dqk ** -0.5 * math.log2(math.e))
    return pl.pallas_call(
        kern,
        grid=(b, n // tm),
        in_specs=[row(d), pl.BlockSpec((None, 8, d), sel), full(g),
                  pl.BlockSpec((tm, LANE), lambda bi, i: (i, 0)),
                  pl.BlockSpec((tm, LANE), lambda bi, i: (i, 0)),
                  full(whg), full(wda), full(wrw)],
        out_specs=[row(whg.shape[1]), row(wda.shape[1]), row(wrw.shape[1])],
        out_shape=[jax.ShapeDtypeStruct((b, n, whg.shape[1]), F32),
                   jax.ShapeDtypeStruct((b, n, wda.shape[1]), BF16),
                   jax.ShapeDtypeStruct((b, n, wrw.shape[1]), F32)],
        compiler_params=_cparams(("parallel", "parallel")),
        name="in_proj",
    )(s, mods_l, g, cos_t, sin_t, whg, wda, wrw)


def _chunk_of_step(j, n_chunks, n_ctx_chunks, rev):
    if not rev:
        return j
    return jnp.where(j < n_ctx_chunks, n_ctx_chunks - 1 - j, n_chunks - 1 - (j - n_ctx_chunks))


def _hgrn_levels(c):
    return [1 << i for i in range(int(math.log2(c)))]


def _hgrn_const(c, rev):
    t = np.arange(c)[:, None]
    s = np.arange(c)[None, :]
    tri = (s >= t) if rev else (s <= t)
    mats = [tri.astype(np.float32)]
    for m in _hgrn_levels(c):
        p = (t // (2 * m)) * (2 * m) + (m if rev else m - 1)
        ref = (s >= p) if rev else (s <= p)
        mats.append(tri.astype(np.float32) - ref.astype(np.float32))
    return np.concatenate(mats, axis=0)


def _hgrn_kernel(*refs, rev, finish, dh):
    if finish:
        const_ref, q_ref, i_ref, f_ref, lb_ref, g_ref, other_ref, ng_ref, out_ref, st_ref = refs
    else:
        const_ref, q_ref, i_ref, f_ref, lb_ref, out_ref, st_ref = refs
    nb, c, w = q_ref.shape

    @pl.when(pl.program_id(1) == 0)
    def _():
        st_ref[...] = jnp.zeros_like(st_ref)

    nh = w // dh
    t_idx = _iota((c, 1), 0)
    s_idx = _mod(_iota((1, nh * c), 1), c)
    bd = _same_head(w, dh)
    lb = lb_ref[...]
    rows = lambda ref: ref[...].reshape(nb * c, w)
    q_all = rows(q_ref)
    v_all = rows(i_ref)
    f_all = lb + (1.0 - lb) * _sigmoid(rows(f_ref))
    kx_all = 1.0 - f_all
    logf_all = jnp.log(f_all)
    (diag_all,) = _head_sums([q_all * kx_all], dh)
    bf = lambda t: t.astype(BF16)
    cs = []
    for bb in range(nb):
        sl = slice(bb * c, (bb + 1) * c)
        seg = _dot_sel(const_ref[...], logf_all[sl])
        bcum = seg[0:c]
        cs.append(dict(bb=bb, q=q_all[sl], v=v_all[sl], kx=kx_all[sl], diag=diag_all[sl], seg=seg, bcum=bcum,
                       bend=bcum[0:1] if rev else bcum[c - 1:c], att=jnp.zeros((c, nh * c), F32)))
    for li, m in enumerate(_hgrn_levels(c)):
        t_up = (t_idx & m) != 0
        s_up = (s_idx & m) != 0
        t_isq = jnp.logical_not(t_up) if rev else t_up
        s_isk = s_up if rev else jnp.logical_not(s_up)
        valid = (_div(t_idx, 2 * m) == _div(s_idx, 2 * m)) & t_isq & s_isk
        for s in cs:
            e = jnp.exp(-jnp.abs(s["seg"][(li + 1) * c:(li + 2) * c]))
            xm = bf(jnp.where(t_isq, s["q"], s["kx"]) * e)
            sc = _dot_nt(xm, _expand_heads(xm, dh))
            s["att"] = s["att"] + jnp.where(valid, sc, 0.0)
    outs = []
    for s in cs:
        q, v, kx, bcum, bend = s["q"], s["v"], s["kx"], s["bcum"], s["bend"]
        o = _dot(bf(s["att"]), _expand_heads(bf(v), dh))
        o = o + s["diag"] * v
        st = st_ref[s["bb"]]
        o = o + _dot_nt(q * jnp.exp(bcum), st, P_HG)
        upd = _dot_tn(v, kx * jnp.exp(bend - bcum), P_HG)
        st_ref[s["bb"]] = st * jnp.exp(bend) + jnp.where(bd, upd, 0.0)
        outs.append(o)
    o_all = jnp.concatenate(outs, axis=0)
    if finish:
        o_all = o_all + rows(other_ref)
        (ms,) = _head_sums([o_all * o_all], dh)
        g = rows(g_ref)
        o_all = o_all * lax.rsqrt(ms * (1.0 / dh) + RMS_EPS) * ng_ref[...] * (g * _sigmoid(g))
    out_ref[...] = o_all.reshape(nb, c, w)


def _hgrn_pass(u_hg, lb_row, n_ctx_chunks, rev, other=None, norm_g_row=None):
    b, n, _ = u_hg.shape
    c = CHUNK
    w = lb_row.shape[-1]
    dh = w // N_HEADS
    nc = n // c
    finish = other is not None
    const = jnp.asarray(np.tile(_hgrn_const(c, rev), (1, 3)), BF16)
    chunk = lambda j: _chunk_of_step(j, nc, n_ctx_chunks, rev)
    nb = math.gcd(b, REC_BATCH)
    col = lambda k: pl.BlockSpec((nb, c, w), lambda bi, j: (bi, chunk(j), k))
    vec = pl.BlockSpec((1, w), lambda bi, j: (0, 0))
    in_specs = [pl.BlockSpec(const.shape, lambda bi, j: (0, 0)), col(0), col(1), col(3 if rev else 2), vec]
    args = [const, u_hg, u_hg, u_hg, lb_row]
    if finish:
        in_specs += [col(4), col(0), vec]
        args += [u_hg, other, norm_g_row]
    return pl.pallas_call(
        functools.partial(_hgrn_kernel, rev=rev, finish=finish, dh=dh),
        grid=(b // nb, nc),
        in_specs=in_specs,
        out_specs=col(0),
        out_shape=jax.ShapeDtypeStruct((b, n, w), F32),
        scratch_shapes=[pltpu.VMEM((nb, w, w), F32)],
        compiler_params=_cparams(("parallel", "arbitrary")),
        name="hgrn_bwd" if rev else "hgrn_fwd",
    )(*args)


def _rwkv_kernel(*refs, rev, finish, dh, n_chunks, n_ctx_chunks):
    if finish:
        (tri_ref, z_ref, zp_ref, zn_ref, mup_ref, mun_ref, w0_ref, w2_ref, a0_ref, a2_ref, g2_ref,
         kk_ref, ka_ref, rk_ref, oy_ref, ob_ref, lnw_ref, lnb_ref, out_ref, st_ref) = refs
    else:
        (tri_ref, z_ref, zp_ref, zn_ref, mup_ref, mun_ref, w0_ref, w2_ref, a0_ref, a2_ref, g2_ref,
         kk_ref, ka_ref, rk_ref, y_ref, bonus_ref, st_ref) = refs
    nb, c, _ = z_ref.shape
    w = w0_ref.shape[-1]
    j = pl.program_id(1)
    chunk = _chunk_of_step(j, n_chunks, n_ctx_chunks, rev)

    @pl.when(j == 0)
    def _():
        st_ref[...] = jnp.zeros_like(st_ref)

    row = _iota((c, 1), 0)
    seg_first = (chunk == 0) | (chunk == n_ctx_chunks)
    seg_last = (chunk == n_ctx_chunks - 1) | (chunk == n_chunks - 1)
    nr = (w // dh) * c
    si = _mod(_iota((1, nr), 1), c)
    strict = (si > row) if rev else (si < row)
    incl = (si >= row) if rev else (si <= row)
    eye_cat = (si == row).astype(F32)
    blk = _same_head(nr, c)
    bd = _same_head(w, dh)
    eye_w = (_iota((w, w), 0) == _iota((w, w), 1)).astype(F32)

    us = []
    for bb in range(nb):
        z = z_ref[bb]
        prev_row = jnp.where(seg_first, 0.0, zp_ref[bb, 7:8, :])
        next_row = jnp.where(seg_last, 0.0, zn_ref[bb, 0:1, :])
        z_prev = jnp.where(row == 0, prev_row, pltpu.roll(z, 1, 0))
        z_next = jnp.where(row == c - 1, next_row, pltpu.roll(z, c - 1, 0))
        us.append(z + mup_ref[...] * (z_prev - z) + mun_ref[...] * (z_next - z))
    u = jnp.concatenate(us, axis=0)

    r_all = u[:, 0:w]
    k_all = u[:, w:2 * w]
    v_all = u[:, 2 * w:3 * w]
    lora = u[:, 3 * w:3 * w + LANE]
    gd = u[:, 3 * w + LANE:3 * w + 2 * LANE]

    w_log = -_softplus(-(w0_ref[...] + _dot_x3(jnp.tanh(lora), w2_ref[0], w2_ref[1]))) - 0.5
    lw_all = -jnp.exp(w_log)
    a_lr = _sigmoid(a0_ref[...] + _dot_x3(lora, a2_ref[0], a2_ref[1]))
    kk = k_all * kk_ref[...]
    kd_all = k_all * (1.0 + (a_lr - 1.0) * ka_ref[...])
    kk_sq, bonus_dot = _head_sums([kk * kk, r_all * kd_all * rk_ref[...]], dh)
    kk = kk * lax.rsqrt(jnp.maximum(kk_sq, 1e-24))
    a_all = -kk
    b_all = kk * a_lr
    bonus_all = bonus_dot * v_all

    chains = range(nb)
    bf = lambda t: t.astype(BF16)
    ex = lambda t: _expand_heads(bf(t), dh)
    st = []
    for bb in chains:
        sl = slice(bb * c, (bb + 1) * c)
        r, v, kd, lw = r_all[sl], v_all[sl], kd_all[sl], lw_all[sl]
        cum = _dot_sel(tri_ref[...], lw)
        tot = cum[0:1] if rev else cum[c - 1:c]
        inv_gam = jnp.exp(-cum)
        rest = jnp.exp(tot - cum)
        r_t = r * jnp.exp(cum)
        a_t = a_all[sl] * jnp.exp(cum - lw)
        ea, eb, ek, ev = ex(a_t), ex(b_all[sl] * inv_gam), ex(kd * inv_gam), ex(v)
        sc = _dot_nt(jnp.concatenate([bf(a_t), bf(r_t)], axis=0), jnp.concatenate([eb, ek], axis=0))
        st.append(dict(
            v=v, r_t=r_t, tot=tot, ea=ea, ev=ev, b_h=b_all[sl] * rest, k_h=kd * rest,
            a_ab=jnp.where(strict, sc[0:c, 0:nr], 0.0), a_ak=jnp.where(strict, sc[0:c, nr:2 * nr], 0.0),
            a_rb=jnp.where(incl, sc[c:2 * c, 0:nr], 0.0), a_rk=jnp.where(incl, sc[c:2 * c, nr:2 * nr], 0.0)))

    bd_cat = lambda t: jnp.where(blk, jnp.concatenate([bf(t)] * (nr // c), axis=0), 0.0)
    for s in st:
        s["pw"] = _dot(bf(s["a_ab"]), bd_cat(s["a_ab"]))
        s["t_cat"] = eye_cat + s["a_ab"]
    n_iter = int(math.log2(c)) - 1
    for it in range(n_iter):
        for s in st:
            pw = bf(s["pw"])
            if it + 1 < n_iter:
                both = _dot(pw, jnp.concatenate([bd_cat(s["t_cat"]), bd_cat(s["pw"])], axis=1))
                s["t_cat"], s["pw"] = s["t_cat"] + both[:, 0:nr], both[:, nr:2 * nr]
            else:
                s["t_cat"] = s["t_cat"] + _dot(pw, bd_cat(s["t_cat"]))

    for s in st:
        s["t_cat"] = bf(s["t_cat"])
        s["x_c"] = _dot(bf(s["a_ak"]), s["ev"])
    for s in st:
        s["w1c"] = _dot(s["t_cat"], s["ea"])
        s["w2c"] = _dot(s["t_cat"], ex(s["x_c"]))
    for s in st:
        a_rb = bf(s["a_rb"])
        s["p_mat"] = s["r_t"] + _dot(a_rb, ex(s["w1c"]))
        s["y0"] = _dot(jnp.concatenate([a_rb, bf(s["a_rk"])], axis=1),
                       jnp.concatenate([ex(s["w2c"]), s["ev"]], axis=0))
    ys = []
    for bb, s in zip(chains, st):
        lhs = jnp.concatenate([s["w1c"], s["p_mat"], eye_w * jnp.exp(s["tot"])], axis=0)
        big = _dot(lhs, st_ref[bb], P_ST)
        ys.append(big[c:2 * c] + s["y0"])
        upd = _dot_tn(jnp.concatenate([s["b_h"], s["k_h"]], axis=0),
                      jnp.concatenate([big[0:c] + s["w2c"], s["v"]], axis=0), P_ST)
        st_ref[bb] = big[2 * c:] + jnp.where(bd, upd, 0.0)

    y = jnp.concatenate(ys, axis=0)
    if finish:
        y = y + oy_ref[...].reshape(nb * c, w)
        (mean,) = _head_sums([y], dh)
        yc = y - mean * (1.0 / dh)
        (var,) = _head_sums([yc * yc], dh)
        yn = yc * lax.rsqrt(var * (1.0 / dh) + RW_GN_EPS) * lnw_ref[...] + lnb_ref[...]
        gate = _dot_x3(_sigmoid(gd), g2_ref[0], g2_ref[1])
        out = (yn + bonus_all + ob_ref[...].reshape(nb * c, w)) * gate
        out_ref[...] = out.reshape(nb, c, w)
    else:
        y_ref[...] = y.reshape(nb, c, w)
        bonus_ref[...] = bonus_all.reshape(nb, c, w)


def _rwkv_pass(u_rw, p, n_ctx_chunks, rev, other=None):
    b, n, wu = u_rw.shape
    c = CHUNK
    w = p["w0"].shape[-1]
    dh = w // N_HEADS
    nc = n // c
    finish = other is not None
    t = np.arange(c)
    tri = (t[None, :] >= t[:, None]) if rev else (t[None, :] <= t[:, None])
    tri = jnp.asarray(np.tile(tri.astype(np.float32), (1, 3)), BF16)
    hi_lo = lambda m: jnp.stack(_split2(m))
    d = 1 if rev else 0
    chunk = lambda j: _chunk_of_step(j, nc, n_ctx_chunks, rev)
    sub = c // 8
    nb = math.gcd(b, REC_BATCH)
    const = lambda a: pl.BlockSpec(a.shape, lambda bi, j: (0,) * a.ndim)
    col = pl.BlockSpec((nb, c, w), lambda bi, j: (bi, chunk(j), 0))
    args = [tri, u_rw, u_rw, u_rw, p["mu_prev"], p["mu_next"], p["w0"][d], hi_lo(p["w2"][d]), p["a0"][d],
            hi_lo(p["a2"][d]), hi_lo(p["g2"]), p["k_k"], p["k_a"], p["r_k"]]
    in_specs = [const(tri),
                pl.BlockSpec((nb, c, wu), lambda bi, j: (bi, chunk(j), 0)),
                pl.BlockSpec((nb, 8, wu), lambda bi, j: (bi, jnp.maximum(chunk(j) * sub - 1, 0), 0)),
                pl.BlockSpec((nb, 8, wu), lambda bi, j: (bi, jnp.minimum((chunk(j) + 1) * sub, n // 8 - 1), 0))]
    in_specs += [const(a) for a in args[4:]]
    if finish:
        args += [other[0], other[1], p["ln_w"], p["ln_b"]]
        in_specs += [col, col, const(p["ln_w"]), const(p["ln_b"])]
        out_specs, out_shape = col, jax.ShapeDtypeStruct((b, n, w), F32)
    else:
        out_specs = [col, col]
        out_shape = [jax.ShapeDtypeStruct((b, n, w), F32)] * 2
    return pl.pallas_call(
        functools.partial(_rwkv_kernel, rev=rev, finish=finish, dh=dh, n_chunks=nc, n_ctx_chunks=n_ctx_chunks),
        grid=(b // nb, nc),
        in_specs=in_specs,
        out_specs=out_specs,
        out_shape=out_shape,
        scratch_shapes=[pltpu.VMEM((nb, w, w), F32)],
        compiler_params=_cparams(("parallel", "arbitrary")),
        name="rwkv_bwd" if rev else "rwkv_fwd",
    )(*args)


def _attn_kernel(lam_ref, q_ref, k_ref, v_ref, ng_ref, o_ref, *, lam_init, n_ctx, n_ctx_tiles):
    lp = lam_ref[...]
    lam = (jnp.exp(jnp.sum(lp[0:1] * lp[1:2], axis=-1, keepdims=True))
           - jnp.exp(jnp.sum(lp[2:3] * lp[3:4], axis=-1, keepdims=True)) + lam_init)

    def attend(k, v):
        half = q_ref.shape[-1] // 2
        lane = _iota((1, q_ref.shape[-1]), 1)
        ts = ATT_SUB
        subs = range(q_ref.shape[0] // ts)
        qs = []
        for i in subs:
            q = q_ref[i * ts:(i + 1) * ts, :]
            qs.append(jnp.concatenate([jnp.where(lane < half, q, 0), jnp.where(lane >= half, q, 0)], axis=0))
        s = [_dot_nt(x, k) for x in qs]
        for i, x in zip(subs, s):
            p = jnp.exp2(x - jnp.max(x, axis=-1, keepdims=True))
            den = jnp.sum(p, axis=-1, keepdims=True)
            pc = p[0:ts] - p[ts:2 * ts] * (lam * den[0:ts] / den[ts:2 * ts])
            o = _dot(pc.astype(BF16), v) / den[0:ts]
            ms = jnp.mean(o * o, axis=-1, keepdims=True)
            o_ref[i * ts:(i + 1) * ts, :] = o * lax.rsqrt(ms + RMS_EPS) * ng_ref[...] * (1.0 - lam_init)

    is_ctx = pl.program_id(2) < n_ctx_tiles

    @pl.when(is_ctx)
    def _():
        attend(k_ref[0:n_ctx, :], v_ref[0:n_ctx, :])

    @pl.when(jnp.logical_not(is_ctx))
    def _():
        attend(k_ref[...], v_ref[...])


def _attention(u_da, lam_rows, norm_g_row, lam_init, n_ctx):
    b, n, w3 = u_da.shape
    wd = w3 // 3
    dv = wd // N_HEADS
    tq = ROW_TILE
    kern = functools.partial(_attn_kernel, lam_init=lam_init, n_ctx=n_ctx, n_ctx_tiles=n_ctx // tq)
    return pl.pallas_call(
        kern,
        grid=(b, N_HEADS, n // tq),
        in_specs=[pl.BlockSpec(lam_rows.shape, lambda bi, h, i: (0, 0)),
                  pl.BlockSpec((None, tq, dv), lambda bi, h, i: (bi, i, h)),
                  pl.BlockSpec((None, n, dv), lambda bi, h, i: (bi, 0, N_HEADS + h)),
                  pl.BlockSpec((None, n, dv), lambda bi, h, i: (bi, 0, 2 * N_HEADS + h)),
                  pl.BlockSpec((1, dv), lambda bi, h, i: (0, 0))],
        out_specs=pl.BlockSpec((None, tq, dv), lambda bi, h, i: (bi, i, h)),
        out_shape=jax.ShapeDtypeStruct((b, n, wd), F32),
        compiler_params=_cparams(("parallel", "parallel", "arbitrary")),
        name="diff_attn",
    )(lam_rows, u_da, u_da, u_da, norm_g_row)


def _out_proj_kernel(s_ref, hg_ref, da_ref, rw_ref, mod_ref, g_ref, whg_ref, wda_ref, wrw_ref,
                     x_ref, h_ref):
    mix = (_dot(hg_ref[...].astype(BF16), whg_ref[...]) + _dot(da_ref[...].astype(BF16), wda_ref[...])
           + _dot(rw_ref[...].astype(BF16), wrw_ref[...]))
    x = s_ref[...] + mod_ref[2:3, :] * mix
    x_ref[...] = x
    ms = jnp.mean(x * x, axis=-1, keepdims=True)
    y = x * lax.rsqrt(ms + RMS_EPS) * g_ref[...]
    h_ref[...] = (y * (1.0 + mod_ref[4:5, :]) + mod_ref[3:4, :]).astype(BF16)


def _out_proj(s, o_hg, o_da, o_rw, mods_l, g, whg, wda, wrw, n_ctx_tiles):
    b, n, d = s.shape
    tm = ROW_TILE
    sel = lambda bi, i: (jnp.where(i < n_ctx_tiles, b, bi), 0, 0)
    full = lambda a: pl.BlockSpec(a.shape, lambda bi, i: (0, 0))
    row = lambda w: pl.BlockSpec((None, tm, w), lambda bi, i: (bi, i, 0))
    return pl.pallas_call(
        _out_proj_kernel,
        grid=(b, n // tm),
        in_specs=[row(d), row(o_hg.shape[-1]), row(o_da.shape[-1]), row(o_rw.shape[-1]),
                  pl.BlockSpec((None, 8, d), sel), full(g), full(whg), full(wda), full(wrw)],
        out_specs=[row(d), row(d)],
        out_shape=[jax.ShapeDtypeStruct((b, n, d), F32), jax.ShapeDtypeStruct((b, n, d), BF16)],
        compiler_params=_cparams(("parallel", "parallel")),
        name="out_proj",
    )(s, o_hg, o_da, o_rw, mods_l, g, whg, wda, wrw)


def _ffn_kernel(x_ref, h_ref, mod_ref, wg_ref, wu_ref, wd_ref, o_ref):
    h = h_ref[...]
    gate = _dot(h, wg_ref[...])
    up = _dot(h, wu_ref[...])
    act = (gate * _sigmoid(gate) * up).astype(BF16)
    o_ref[...] = x_ref[...] + mod_ref[5:6, :] * _dot(act, wd_ref[...])


def _ffn(x, h, mods_l, wg, wu, wd, n_ctx_tiles):
    b, n, d = x.shape
    tm = ROW_TILE
    sel = lambda bi, i: (jnp.where(i < n_ctx_tiles, b, bi), 0, 0)
    resident = lambda a: pl.BlockSpec(a.shape, lambda bi, i: (0, 0), pipeline_mode=pl.Buffered(1))
    row = pl.BlockSpec((None, tm, d), lambda bi, i: (bi, i, 0))
    return pl.pallas_call(
        _ffn_kernel,
        grid=(b, n // tm),
        in_specs=[row, row, pl.BlockSpec((None, 8, d), sel), resident(wg), resident(wu), resident(wd)],
        out_specs=row,
        out_shape=jax.ShapeDtypeStruct((b, n, d), F32),
        compiler_params=_cparams(("parallel", "parallel")),
        name="ffn",
    )(x, h, mods_l, wg, wu, wd)


def _final_norm_kernel(x_ref, g_ref, o_ref):
    x = x_ref[...]
    ms = jnp.mean(x * x, axis=-1, keepdims=True)
    o_ref[...] = x * lax.rsqrt(ms + RMS_EPS) * g_ref[...]


def _final_norm(s, g, n_ctx):
    b, n, d = s.shape
    tm = ROW_TILE
    off = n_ctx // tm
    return pl.pallas_call(
        _final_norm_kernel,
        grid=(b, (n - n_ctx) // tm),
        in_specs=[pl.BlockSpec((None, tm, d), lambda bi, i: (bi, i + off, 0)),
                  pl.BlockSpec((1, d), lambda bi, i: (0, 0))],
        out_specs=pl.BlockSpec((None, tm, d), lambda bi, i: (bi, i, 0)),
        out_shape=jax.ShapeDtypeStruct((b, n - n_ctx, d), F32),
        compiler_params=_cparams(("parallel", "parallel")),
        name="final_norm",
    )(s, g)


def _rope_tables(n_ctx, n_lat, dqk):
    n_freq = dqk // 4
    t = np.arange(n_lat)
    inv_freq = ROPE_BASE ** (-np.arange(n_freq, dtype=np.float32) / n_freq)
    ang = np.concatenate([(t // GRID_W)[:, None] * inv_freq, (t % GRID_W)[:, None] * inv_freq], axis=-1)
    reps = LANE // dqk
    cos = np.tile(np.concatenate([np.cos(ang), np.cos(ang)], axis=-1), (1, reps))
    sin = np.tile(np.concatenate([-np.sin(ang), np.sin(ang)], axis=-1), (1, reps))
    cos = np.concatenate([np.ones((n_ctx, LANE)), cos], axis=0)
    sin = np.concatenate([np.zeros((n_ctx, LANE)), sin], axis=0)
    return jnp.asarray(cos, F32), jnp.asarray(sin, F32)


def _pad_rows(a, start, total):
    return jnp.zeros((total, a.shape[-1]), a.dtype).at[start:start + a.shape[0]].set(a)


def kernel(x, c, ctx, c_ctx, ada_w, ada_b, norm1_g, norm2_g, w_in, w_out, hg_lb_logits, hg_norm_g, da_lam_q1, da_lam_k1, da_lam_q2, da_lam_k2, da_norm_g, rw_mu_prev, rw_mu_next, rw_w0, rw_w2, rw_a0, rw_a2, rw_g2, rw_k_k, rw_k_a, rw_r_k, rw_ln_w, rw_ln_b, ffn_w_gate, ffn_w_up, ffn_w_down, final_norm_g):
    b, n_lat, d = x.shape
    n_ctx = ctx.shape[1]
    depth = ada_w.shape[0]
    hg_w = hg_lb_logits.shape[-1]
    rw_w = rw_w0.shape[-1]
    da_w = w_out.shape[1] - hg_w - rw_w
    hg_cols, da_cols = 5 * hg_w, 3 * da_w
    rw_cols = w_in.shape[-1] - hg_cols - da_cols
    rw_pad = 3 * rw_w + 2 * LANE
    r_dec, r_icl, r_gate = rw_w2.shape[2], rw_a2.shape[2], rw_g2.shape[1]
    assert n_ctx % ROW_TILE == 0 and n_lat % ROW_TILE == 0 and n_lat % GRID_W == 0
    assert 2 * (r_dec + r_icl) == LANE and r_gate <= LANE and rw_cols == 3 * rw_w + LANE + r_gate
    n_ctx_tiles, n_ctx_chunks = n_ctx // ROW_TILE, n_ctx // CHUNK

    cond = jnp.zeros((16, d), F32).at[:b].set(c).at[b].set(c_ctx)
    mods = _ada_mod(cond, ada_w, ada_b).reshape(depth, 16, 6, d)
    mods = jnp.pad(mods, ((0, 0), (0, 0), (0, 2), (0, 0)))

    p_lb = jax.nn.softmax(hg_lb_logits.astype(F32), axis=1)
    lower_bounds = jnp.cumsum(p_lb, axis=1) - p_lb[:, :1]
    cos_t, sin_t = _rope_tables(n_ctx, n_lat, da_w // (2 * N_HEADS))

    s = jnp.concatenate([ctx, x], axis=1)
    for l in range(depth):
        w_l = w_in[l]
        whg = w_l[:, :hg_cols].astype(BF16)
        wda = w_l[:, hg_cols:hg_cols + da_cols].astype(BF16)
        wrw = jnp.pad(w_l[:, hg_cols + da_cols:], ((0, 0), (0, rw_pad - rw_cols))).astype(BF16)
        u_hg, u_da, u_rw = _in_proj(s, mods[l], norm1_g[l][None], cos_t, sin_t, whg, wda, wrw, n_ctx_tiles)

        o_b = _hgrn_pass(u_hg, lower_bounds[1, l][None], n_ctx_chunks, True)
        o_hg = _hgrn_pass(u_hg, lower_bounds[0, l][None], n_ctx_chunks, False, other=o_b,
                          norm_g_row=jnp.tile(hg_norm_g[l], N_HEADS)[None])

        lam_init = 0.8 - 0.6 * math.exp(-0.3 * l)
        lam_rows = jnp.zeros((8, LANE), F32).at[0:4, :da_lam_q1.shape[-1]].set(
            jnp.stack([da_lam_q1[l], da_lam_k1[l], da_lam_q2[l], da_lam_k2[l]]))
        o_da = _attention(u_da, lam_rows, da_norm_g[l][None], lam_init, n_ctx)

        mu_pad = lambda m: jnp.pad(m, (0, rw_pad - rw_cols))[None]
        off_dec, off_icl = 0, 2 * r_dec
        rw_p = {
            "mu_prev": mu_pad(rw_mu_prev[l]), "mu_next": mu_pad(rw_mu_next[l]),
            "w0": rw_w0[:, l][:, None, :], "a0": rw_a0[:, l][:, None, :],
            "w2": jnp.stack([_pad_rows(rw_w2[i, l], off_dec + i * r_dec, LANE) for i in range(2)]),
            "a2": jnp.stack([_pad_rows(rw_a2[i, l], off_icl + i * r_icl, LANE) for i in range(2)]),
            "g2": _pad_rows(rw_g2[l], 0, LANE),
            "k_k": rw_k_k[l][None], "k_a": rw_k_a[l][None], "r_k": rw_r_k[l][None],
            "ln_w": rw_ln_w[l][None], "ln_b": rw_ln_b[l][None],
        }
        yb = _rwkv_pass(u_rw, rw_p, n_ctx_chunks, True)
        o_rw = _rwkv_pass(u_rw, rw_p, n_ctx_chunks, False, other=yb)

        wo = w_out[l].astype(BF16)
        x_mid, h2 = _out_proj(s, o_hg, o_da, o_rw, mods[l], norm2_g[l][None],
                              wo[:hg_w], wo[hg_w:hg_w + da_w], wo[hg_w + da_w:], n_ctx_tiles)
        s = _ffn(x_mid, h2, mods[l], ffn_w_gate[l].astype(BF16), ffn_w_up[l].astype(BF16),
                 ffn_w_down[l].astype(BF16), n_ctx_tiles)
    return _final_norm(s, final_norm_g[None], n_ctx)
```

```python
import functools
import math

import numpy as np
import jax
import jax.numpy as jnp
from jax import lax
from jax.experimental import pallas as pl
from jax.experimental.pallas import tpu as pltpu

F32 = jnp.float32
BF16 = jnp.bfloat16
HI = lax.Precision.HIGHEST

N_HEADS = 4
GRID_W = 64
ROPE_BASE = 10000.0
RMS_EPS = 1e-6
RW_GN_EPS = 64e-5

ROW_TILE = 256
ATT_KEY_BLOCK = 256
ATT_PV_BLOCKS = 4
CHUNK = 64
REC_BATCH = 8
LANE = 128
VMEM_LIMIT = 56 * 1024 * 1024


def _cparams(sem):
    return pltpu.CompilerParams(dimension_semantics=sem, vmem_limit_bytes=VMEM_LIMIT)


def _dot_dims(a, b, dims, prec):
    dg = lambda x, y, p=None: lax.dot_general(x, y, (dims, ((), ())), preferred_element_type=F32, precision=p)
    if prec == "bf16":
        return dg(a.astype(BF16), b.astype(BF16))
    if prec == "x3":
        a_hi, b_hi = a.astype(BF16), b.astype(BF16)
        a_lo = (a - a_hi.astype(F32)).astype(BF16)
        b_lo = (b - b_hi.astype(F32)).astype(BF16)
        return dg(a_hi, b_hi) + (dg(a_lo, b_hi) + dg(a_hi, b_lo))
    return dg(a, b, prec)


def _dot(a, b, prec=None):
    return _dot_dims(a, b, ((1,), (0,)), prec)


def _dot_nt(a, b, prec=None):
    return _dot_dims(a, b, ((1,), (1,)), prec)


def _dot_tn(a, b, prec=None):
    return _dot_dims(a, b, ((0,), (0,)), prec)


P_HG = "bf16"
P_ST = "bf16"


def _iota(shape, dim):
    return lax.broadcasted_iota(jnp.int32, shape, dim)


def _div(x, d):
    assert d & (d - 1) == 0
    return x >> (d.bit_length() - 1)


def _mod(x, d):
    assert d & (d - 1) == 0
    return x & (d - 1)


def _softplus(x):
    return jnp.maximum(x, 0.0) + jnp.log(1.0 + jnp.exp(-jnp.abs(x)))


def _same_head(w, dh):
    return _div(_iota((w, w), 0), dh) == _div(_iota((w, w), 1), dh)


def _expand_heads(x, dh):
    w = x.shape[-1]
    head = _div(_iota((1, w), 1), dh)
    return jnp.concatenate([jnp.where(head == h, x, 0.0) for h in range(w // dh)], axis=0)


def _collapse_heads(x, c):
    out = x[0:c]
    for h in range(1, x.shape[0] // c):
        out = out + x[h * c:(h + 1) * c]
    return out


def _split3(x):
    hi = x.astype(BF16)
    r = x - hi.astype(F32)
    mid = r.astype(BF16)
    lo = (r - mid.astype(F32)).astype(BF16)
    return hi, mid, lo


def _split2(x):
    hi = x.astype(BF16)
    return hi, (x - hi.astype(F32)).astype(BF16)


def _head_sums(xs, dh):
    w = xs[0].shape[-1]
    ones_bd = _same_head(w, dh).astype(BF16)
    out = _dot(jnp.concatenate([p for x in xs for p in _split3(x)], axis=0), ones_bd)
    res, off = [], 0
    for x in xs:
        r = x.shape[0]
        res.append(out[off:off + r] + (out[off + r:off + 2 * r] + out[off + 2 * r:off + 3 * r]))
        off += 3 * r
    return res


def _dot_sel(sel3, x):
    return _dot(sel3, jnp.concatenate(_split3(x), axis=0))


def _dot_x3(x, w_hi, w_lo):
    x_hi, x_lo = _split2(x)
    r = x.shape[0]
    t = _dot(jnp.concatenate([x_hi, x_lo], axis=0), w_hi)
    return t[0:r] + t[r:2 * r] + _dot(x_hi, w_lo)


def _sigmoid(x):
    return 1.0 / (1.0 + jnp.exp(-x))


def _ada_kernel(c_ref, w_ref, b_ref, o_ref):
    c = c_ref[...]
    sc = c * _sigmoid(c)
    o_ref[...] = _dot(sc, w_ref[...], HI) + b_ref[...]


def _ada_mod(cond, ada_w, ada_b):
    depth, d, d6 = ada_w.shape
    rows = cond.shape[0]
    tn = 1536
    return pl.pallas_call(
        _ada_kernel,
        grid=(depth, d6 // tn),
        in_specs=[pl.BlockSpec((rows, d), lambda l, j: (0, 0)),
                  pl.BlockSpec((None, d, tn), lambda l, j: (l, 0, j)),
                  pl.BlockSpec((None, 1, tn), lambda l, j: (l, 0, j))],
        out_specs=pl.BlockSpec((None, rows, tn), lambda l, j: (l, 0, j)),
        out_shape=jax.ShapeDtypeStruct((depth, rows, d6), F32),
        compiler_params=_cparams(("arbitrary", "arbitrary")),
        name="ada_mod",
    )(cond, ada_w, ada_b.reshape(depth, 1, d6))


def _rope_cols(t, cos, sin_signed, half):
    lane = _iota((1, t.shape[-1]), 1)
    first = _mod(lane, 2 * half) < half
    swapped = jnp.where(first, pltpu.roll(t, t.shape[-1] - half, 1), pltpu.roll(t, half, 1))
    return t * cos + swapped * sin_signed


def _in_proj_kernel(s_ref, mod_ref, g_ref, cos_ref, sin_ref, whg_ref, wda_ref, wrw_ref,
                    uhg_ref, uda_ref, urw_ref, *, da_width, q_scale):
    x = s_ref[...]
    ms = jnp.mean(x * x, axis=-1, keepdims=True)
    y = x * lax.rsqrt(ms + RMS_EPS) * g_ref[...]
    h = (y * (1.0 + mod_ref[1:2, :]) + mod_ref[0:1, :]).astype(BF16)
    uhg_ref[...] = _dot(h, whg_ref[...])
    urw_ref[...] = _dot(h, wrw_ref[...])
    uda = _dot(h, wda_ref[...])
    cos = cos_ref[...]
    sin = sin_ref[...]
    dqk = da_width // (2 * N_HEADS)
    for j in range(2 * da_width // LANE):
        t = _rope_cols(uda[:, j * LANE:(j + 1) * LANE], cos, sin, dqk // 2)
        if j < da_width // LANE:
            t = t * q_scale
        uda_ref[:, j * LANE:(j + 1) * LANE] = t.astype(BF16)
    uda_ref[:, 2 * da_width:] = uda[:, 2 * da_width:].astype(BF16)


def _in_proj(s, mods_l, g, cos_t, sin_t, whg, wda, wrw, n_ctx_tiles):
    b, n, d = s.shape
    tm = ROW_TILE
    da_width = wda.shape[1] // 3
    dqk = da_width // (2 * N_HEADS)
    sel = lambda bi, i: (jnp.where(i < n_ctx_tiles, b, bi), 0, 0)
    full = lambda a: pl.BlockSpec(a.shape, lambda bi, i: (0, 0))
    row = lambda w: pl.BlockSpec((None, tm, w), lambda bi, i: (bi, i, 0))
    kern = functools.partial(_in_proj_kernel, da_width=da_width, q_scale=dqk ** -0.5 * math.log2(math.e))
    return pl.pallas_call(
        kern,
        grid=(b, n // tm),
        in_specs=[row(d), pl.BlockSpec((None, 8, d), sel), full(g),
                  pl.BlockSpec((tm, LANE), lambda bi, i: (i, 0)),
                  pl.BlockSpec((tm, LANE), lambda bi, i: (i, 0)),
                  full(whg), full(wda), full(wrw)],
        out_specs=[row(whg.shape[1]), row(wda.shape[1]), row(wrw.shape[1])],
        out_shape=[jax.ShapeDtypeStruct((b, n, whg.shape[1]), F32),
                   jax.ShapeDtypeStruct((b, n, wda.shape[1]), BF16),
                   jax.ShapeDtypeStruct((b, n, wrw.shape[1]), F32)],
        compiler_params=_cparams(("parallel", "parallel")),
        name="in_proj",
    )(s, mods_l, g, cos_t, sin_t, whg, wda, wrw)


def _chunk_of_step(j, n_chunks, n_ctx_chunks, rev):
    if not rev:
        return j
    return jnp.where(j < n_ctx_chunks, n_ctx_chunks - 1 - j, n_chunks - 1 - (j - n_ctx_chunks))


def _hgrn_levels(c):
    return [1 << i for i in range(int(math.log2(c)))]


def _hgrn_const(c, rev):
    t = np.arange(c)[:, None]
    s = np.arange(c)[None, :]
    tri = (s >= t) if rev else (s <= t)
    mats = [tri.astype(np.float32)]
    for m in _hgrn_levels(c):
        p = (t // (2 * m)) * (2 * m) + (m if rev else m - 1)
        ref = (s >= p) if rev else (s <= p)
        mats.append(tri.astype(np.float32) - ref.astype(np.float32))
    return np.concatenate(mats, axis=0)


def _hgrn_kernel(*refs, rev, finish, dh):
    if finish:
        const_ref, q_ref, i_ref, f_ref, lb_ref, g_ref, other_ref, ng_ref, out_ref, st_ref = refs
    else:
        const_ref, q_ref, i_ref, f_ref, lb_ref, out_ref, st_ref = refs
    nb, c, w = q_ref.shape

    @pl.when(pl.program_id(1) == 0)
    def _():
        st_ref[...] = jnp.zeros_like(st_ref)

    nh = w // dh
    t_idx = _iota((c, 1), 0)
    s_idx = _mod(_iota((1, nh * c), 1), c)
    bd = _same_head(w, dh)
    lb = lb_ref[...]
    rows = lambda ref: ref[...].reshape(nb * c, w)
    q_all = rows(q_ref)
    v_all = rows(i_ref)
    f_all = lb + (1.0 - lb) * _sigmoid(rows(f_ref))
    kx_all = 1.0 - f_all
    logf_all = jnp.log(f_all)
    (diag_all,) = _head_sums([q_all * kx_all], dh)
    bf = lambda t: t.astype(BF16)
    cs = []
    for bb in range(nb):
        sl = slice(bb * c, (bb + 1) * c)
        seg = _dot_sel(const_ref[...], logf_all[sl])
        bcum = seg[0:c]
        cs.append(dict(bb=bb, q=q_all[sl], v=v_all[sl], kx=kx_all[sl], diag=diag_all[sl], seg=seg, bcum=bcum,
                       bend=bcum[0:1] if rev else bcum[c - 1:c], att=jnp.zeros((c, nh * c), F32)))
    for li, m in enumerate(_hgrn_levels(c)):
        t_up = (t_idx & m) != 0
        s_up = (s_idx & m) != 0
        t_isq = jnp.logical_not(t_up) if rev else t_up
        s_isk = s_up if rev else jnp.logical_not(s_up)
        valid = (_div(t_idx, 2 * m) == _div(s_idx, 2 * m)) & t_isq & s_isk
        for s in cs:
            e = jnp.exp(-jnp.abs(s["seg"][(li + 1) * c:(li + 2) * c]))
            xm = bf(jnp.where(t_isq, s["q"], s["kx"]) * e)
            sc = _dot_nt(xm, _expand_heads(xm, dh))
            s["att"] = s["att"] + jnp.where(valid, sc, 0.0)
    outs = []
    for s in cs:
        q, v, kx, bcum, bend = s["q"], s["v"], s["kx"], s["bcum"], s["bend"]
        o = _dot(bf(s["att"]), _expand_heads(bf(v), dh))
        o = o + s["diag"] * v
        st = st_ref[s["bb"]]
        o = o + _dot_nt(q * jnp.exp(bcum), st, P_HG)
        upd = _dot_tn(v, kx * jnp.exp(bend - bcum), P_HG)
        st_ref[s["bb"]] = st * jnp.exp(bend) + jnp.where(bd, upd, 0.0)
        outs.append(o)
    o_all = jnp.concatenate(outs, axis=0)
    if finish:
        o_all = o_all + rows(other_ref)
        (ms,) = _head_sums([o_all * o_all], dh)
        g = rows(g_ref)
        o_all = o_all * lax.rsqrt(ms * (1.0 / dh) + RMS_EPS) * ng_ref[...] * (g * _sigmoid(g))
    out_ref[...] = o_all.reshape(nb, c, w)


def _hgrn_pass(u_hg, lb_row, n_ctx_chunks, rev, other=None, norm_g_row=None):
    b, n, _ = u_hg.shape
    c = CHUNK
    w = lb_row.shape[-1]
    dh = w // N_HEADS
    nc = n // c
    finish = other is not None
    const = jnp.asarray(np.tile(_hgrn_const(c, rev), (1, 3)), BF16)
    chunk = lambda j: _chunk_of_step(j, nc, n_ctx_chunks, rev)
    nb = math.gcd(b, REC_BATCH)
    col = lambda k: pl.BlockSpec((nb, c, w), lambda bi, j: (bi, chunk(j), k))
    vec = pl.BlockSpec((1, w), lambda bi, j: (0, 0))
    in_specs = [pl.BlockSpec(const.shape, lambda bi, j: (0, 0)), col(0), col(1), col(3 if rev else 2), vec]
    args = [const, u_hg, u_hg, u_hg, lb_row]
    if finish:
        in_specs += [col(4), col(0), vec]
        args += [u_hg, other, norm_g_row]
    return pl.pallas_call(
        functools.partial(_hgrn_kernel, rev=rev, finish=finish, dh=dh),
        grid=(b // nb, nc),
        in_specs=in_specs,
        out_specs=col(0),
        out_shape=jax.ShapeDtypeStruct((b, n, w), F32),
        scratch_shapes=[pltpu.VMEM((nb, w, w), F32)],
        compiler_params=_cparams(("parallel", "arbitrary")),
        name="hgrn_bwd" if rev else "hgrn_fwd",
    )(*args)


def _rwkv_kernel(*refs, rev, finish, dh, n_chunks, n_ctx_chunks):
    if finish:
        (tri_ref, z_ref, zp_ref, zn_ref, mup_ref, mun_ref, w0_ref, w2_ref, a0_ref, a2_ref, g2_ref,
         kk_ref, ka_ref, rk_ref, oy_ref, ob_ref, lnw_ref, lnb_ref, out_ref, st_ref) = refs
    else:
        (tri_ref, z_ref, zp_ref, zn_ref, mup_ref, mun_ref, w0_ref, w2_ref, a0_ref, a2_ref, g2_ref,
         kk_ref, ka_ref, rk_ref, y_ref, bonus_ref, st_ref) = refs
    nb, c, _ = z_ref.shape
    w = w0_ref.shape[-1]
    j = pl.program_id(1)
    chunk = _chunk_of_step(j, n_chunks, n_ctx_chunks, rev)

    @pl.when(j == 0)
    def _():
        st_ref[...] = jnp.zeros_like(st_ref)

    row = _iota((c, 1), 0)
    seg_first = (chunk == 0) | (chunk == n_ctx_chunks)
    seg_last = (chunk == n_ctx_chunks - 1) | (chunk == n_chunks - 1)
    nr = (w // dh) * c
    si = _mod(_iota((1, nr), 1), c)
    strict = (si > row) if rev else (si < row)
    incl = (si >= row) if rev else (si <= row)
    eye_cat = (si == row).astype(F32)
    blk = _same_head(nr, c)
    bd = _same_head(w, dh)
    eye_w = (_iota((w, w), 0) == _iota((w, w), 1)).astype(F32)

    us = []
    for bb in range(nb):
        z = z_ref[bb]
        prev_row = jnp.where(seg_first, 0.0, zp_ref[bb, 7:8, :])
        next_row = jnp.where(seg_last, 0.0, zn_ref[bb, 0:1, :])
        z_prev = jnp.where(row == 0, prev_row, pltpu.roll(z, 1, 0))
        z_next = jnp.where(row == c - 1, next_row, pltpu.roll(z, c - 1, 0))
        us.append(z + mup_ref[...] * (z_prev - z) + mun_ref[...] * (z_next - z))
    u = jnp.concatenate(us, axis=0)

    r_all = u[:, 0:w]
    k_all = u[:, w:2 * w]
    v_all = u[:, 2 * w:3 * w]
    lora = u[:, 3 * w:3 * w + LANE]
    gd = u[:, 3 * w + LANE:3 * w + 2 * LANE]

    w_log = -_softplus(-(w0_ref[...] + _dot_x3(jnp.tanh(lora), w2_ref[0], w2_ref[1]))) - 0.5
    lw_all = -jnp.exp(w_log)
    a_lr = _sigmoid(a0_ref[...] + _dot_x3(lora, a2_ref[0], a2_ref[1]))
    kk = k_all * kk_ref[...]
    kd_all = k_all * (1.0 + (a_lr - 1.0) * ka_ref[...])
    kk_sq, bonus_dot = _head_sums([kk * kk, r_all * kd_all * rk_ref[...]], dh)
    kk = kk * lax.rsqrt(jnp.maximum(kk_sq, 1e-24))
    a_all = -kk
    b_all = kk * a_lr
    bonus_all = bonus_dot * v_all

    chains = range(nb)
    bf = lambda t: t.astype(BF16)
    ex = lambda t: _expand_heads(bf(t), dh)
    st = []
    for bb in chains:
        sl = slice(bb * c, (bb + 1) * c)
        r, v, kd, lw = r_all[sl], v_all[sl], kd_all[sl], lw_all[sl]
        cum = _dot_sel(tri_ref[...], lw)
        tot = cum[0:1] if rev else cum[c - 1:c]
        inv_gam = jnp.exp(-cum)
        rest = jnp.exp(tot - cum)
        r_t = r * jnp.exp(cum)
        a_t = a_all[sl] * jnp.exp(cum - lw)
        ea, eb, ek, ev = ex(a_t), ex(b_all[sl] * inv_gam), ex(kd * inv_gam), ex(v)
        sc = _dot_nt(jnp.concatenate([bf(a_t), bf(r_t)], axis=0), jnp.concatenate([eb, ek], axis=0))
        st.append(dict(
            v=v, r_t=r_t, tot=tot, ea=ea, ev=ev, b_h=b_all[sl] * rest, k_h=kd * rest,
            a_ab=jnp.where(strict, sc[0:c, 0:nr], 0.0), a_ak=jnp.where(strict, sc[0:c, nr:2 * nr], 0.0),
            a_rb=jnp.where(incl, sc[c:2 * c, 0:nr], 0.0), a_rk=jnp.where(incl, sc[c:2 * c, nr:2 * nr], 0.0)))

    bd_cat = lambda t: jnp.where(blk, jnp.concatenate([bf(t)] * (nr // c), axis=0), 0.0)
    for s in st:
        s["pw"] = _dot(bf(s["a_ab"]), bd_cat(s["a_ab"]))
        s["t_cat"] = eye_cat + s["a_ab"]
    n_iter = int(math.log2(c)) - 1
    for it in range(n_iter):
        for s in st:
            pw = bf(s["pw"])
            if it + 1 < n_iter:
                both = _dot(pw, jnp.concatenate([bd_cat(s["t_cat"]), bd_cat(s["pw"])], axis=1))
                s["t_cat"], s["pw"] = s["t_cat"] + both[:, 0:nr], both[:, nr:2 * nr]
            else:
                s["t_cat"] = s["t_cat"] + _dot(pw, bd_cat(s["t_cat"]))

    for s in st:
        s["t_cat"] = bf(s["t_cat"])
        s["x_c"] = _dot(bf(s["a_ak"]), s["ev"])
    for s in st:
        s["w1c"] = _dot(s["t_cat"], s["ea"])
        s["w2c"] = _dot(s["t_cat"], ex(s["x_c"]))
    for s in st:
        a_rb = bf(s["a_rb"])
        s["p_mat"] = s["r_t"] + _dot(a_rb, ex(s["w1c"]))
        s["y0"] = _dot(jnp.concatenate([a_rb, bf(s["a_rk"])], axis=1),
                       jnp.concatenate([ex(s["w2c"]), s["ev"]], axis=0))
    ys = []
    for bb, s in zip(chains, st):
        lhs = jnp.concatenate([s["w1c"], s["p_mat"], eye_w * jnp.exp(s["tot"])], axis=0)
        big = _dot(lhs, st_ref[bb], P_ST)
        ys.append(big[c:2 * c] + s["y0"])
        upd = _dot_tn(jnp.concatenate([s["b_h"], s["k_h"]], axis=0),
                      jnp.concatenate([big[0:c] + s["w2c"], s["v"]], axis=0), P_ST)
        st_ref[bb] = big[2 * c:] + jnp.where(bd, upd, 0.0)

    y = jnp.concatenate(ys, axis=0)
    if finish:
        y = y + oy_ref[...].reshape(nb * c, w)
        (mean,) = _head_sums([y], dh)
        yc = y - mean * (1.0 / dh)
        (var,) = _head_sums([yc * yc], dh)
        yn = yc * lax.rsqrt(var * (1.0 / dh) + RW_GN_EPS) * lnw_ref[...] + lnb_ref[...]
        gate = _dot_x3(_sigmoid(gd), g2_ref[0], g2_ref[1])
        out = (yn + bonus_all + ob_ref[...].reshape(nb * c, w)) * gate
        out_ref[...] = out.reshape(nb, c, w)
    else:
        y_ref[...] = y.reshape(nb, c, w)
        bonus_ref[...] = bonus_all.reshape(nb, c, w)


def _rwkv_pass(u_rw, p, n_ctx_chunks, rev, other=None):
    b, n, wu = u_rw.shape
    c = CHUNK
    w = p["w0"].shape[-1]
    dh = w // N_HEADS
    nc = n // c
    finish = other is not None
    t = np.arange(c)
    tri = (t[None, :] >= t[:, None]) if rev else (t[None, :] <= t[:, None])
    tri = jnp.asarray(np.tile(tri.astype(np.float32), (1, 3)), BF16)
    hi_lo = lambda m: jnp.stack(_split2(m))
    d = 1 if rev else 0
    chunk = lambda j: _chunk_of_step(j, nc, n_ctx_chunks, rev)
    sub = c // 8
    nb = math.gcd(b, REC_BATCH)
    const = lambda a: pl.BlockSpec(a.shape, lambda bi, j: (0,) * a.ndim)
    col = pl.BlockSpec((nb, c, w), lambda bi, j: (bi, chunk(j), 0))
    args = [tri, u_rw, u_rw, u_rw, p["mu_prev"], p["mu_next"], p["w0"][d], hi_lo(p["w2"][d]), p["a0"][d],
            hi_lo(p["a2"][d]), hi_lo(p["g2"]), p["k_k"], p["k_a"], p["r_k"]]
    in_specs = [const(tri),
                pl.BlockSpec((nb, c, wu), lambda bi, j: (bi, chunk(j), 0)),
                pl.BlockSpec((nb, 8, wu), lambda bi, j: (bi, jnp.maximum(chunk(j) * sub - 1, 0), 0)),
                pl.BlockSpec((nb, 8, wu), lambda bi, j: (bi, jnp.minimum((chunk(j) + 1) * sub, n // 8 - 1), 0))]
    in_specs += [const(a) for a in args[4:]]
    if finish:
        args += [other[0], other[1], p["ln_w"], p["ln_b"]]
        in_specs += [col, col, const(p["ln_w"]), const(p["ln_b"])]
        out_specs, out_shape = col, jax.ShapeDtypeStruct((b, n, w), F32)
    else:
        out_specs = [col, col]
        out_shape = [jax.ShapeDtypeStruct((b, n, w), F32)] * 2
    return pl.pallas_call(
        functools.partial(_rwkv_kernel, rev=rev, finish=finish, dh=dh, n_chunks=nc, n_ctx_chunks=n_ctx_chunks),
        grid=(b // nb, nc),
        in_specs=in_specs,
        out_specs=out_specs,
        out_shape=out_shape,
        scratch_shapes=[pltpu.VMEM((nb, w, w), F32)],
        compiler_params=_cparams(("parallel", "arbitrary")),
        name="rwkv_bwd" if rev else "rwkv_fwd",
    )(*args)


def _attn_lam(lam_ref, lam_init):
    lp = lam_ref[...]
    return (jnp.exp(jnp.sum(lp[0:1] * lp[1:2], axis=-1, keepdims=True))
            - jnp.exp(jnp.sum(lp[2:3] * lp[3:4], axis=-1, keepdims=True)) + lam_init)


def _stack_maps(q):
    half = q.shape[-1] // 2
    lane = _iota((1, q.shape[-1]), 1)
    return jnp.concatenate([jnp.where(lane < half, q, 0), jnp.where(lane >= half, q, 0)], axis=0)


def _order_after(x):
    bits = lax.bitcast_convert_type(x[0:8, 0:LANE], jnp.int32)
    zero = lax.shift_right_logical(lax.shift_right_logical(bits, 16), 16)
    return zero[0:1, :].astype(F32)


def _attn_finish(o, ng, lam_init):
    ms = jnp.mean(o * o, axis=-1, keepdims=True)
    return o * lax.rsqrt(ms + RMS_EPS) * ng * (1.0 - lam_init)


def _attn_ctx_kernel(lam_ref, q_ref, k_ref, v_ref, ng_ref, o_ref, *, lam_init):
    lam = _attn_lam(lam_ref, lam_init)
    tq = q_ref.shape[0]
    s = _dot_nt(_stack_maps(q_ref[...]), k_ref[...])
    p = jnp.exp2(s - jnp.max(s, axis=-1, keepdims=True))
    den = jnp.sum(p, axis=-1, keepdims=True)
    pc = p[0:tq] - p[tq:2 * tq] * (lam * den[0:tq] / den[tq:2 * tq])
    o_ref[...] = _attn_finish(_dot(pc.astype(BF16), v_ref[...]) / den[0:tq], ng_ref[...], lam_init)


def _attn_lat_kernel(lam_ref, q_ref, k_ref, v_ref, ng_ref, prev_ref, o_ref, s_scr, p_scr, mx_scr, den_scr,
                     *, lam_init):
    del prev_ref
    g = pl.program_id(0)
    tq, dv = q_ref.shape
    nk = k_ref.shape[0]
    kb = ATT_KEY_BLOCK
    lam = _attn_lam(lam_ref, lam_init)

    @pl.when(g == 0)
    def _():
        s_scr[...] = jnp.zeros_like(s_scr)
        p_scr[...] = jnp.zeros_like(p_scr)
        mx_scr[...] = jnp.zeros_like(mx_scr)
        den_scr[...] = jnp.ones_like(den_scr)

    def step(cur):
        old = 1 - cur
        qs = _stack_maps(q_ref[...])
        m_prev = mx_scr[old]
        den = den_scr[cur]
        cfac = (lam * den[0:tq] / den[tq:2 * tq]).astype(BF16)
        mrun = drun = None
        acc = jnp.zeros((tq, dv), F32)
        nblk = nk // kb
        for j in range(nblk):
            s_j = _dot_nt(qs, k_ref[j * kb:(j + 1) * kb, :])
            s_scr[cur, :, j * kb:(j + 1) * kb] = s_j
            for c0 in range(0, kb, LANE):
                piece = s_j[:, c0:c0 + LANE]
                mrun = piece if mrun is None else jnp.maximum(mrun, piece)
            m_j = m_prev + _order_after(mrun) if j % 2 == 0 else m_prev
            e_sum = None
            for c0 in range(j * kb, (j + 1) * kb, LANE):
                e = jnp.exp2((s_scr[old, :, c0:c0 + LANE] - m_j).astype(BF16))
                e_sum = e if e_sum is None else e_sum + e
                p_scr[old, :, c0:c0 + LANE] = e
            drun = e_sum.astype(F32) if drun is None else drun + e_sum.astype(F32)
            if (j + 1) % ATT_PV_BLOCKS == 0 or j + 1 == nblk:
                g0, g1 = (j // ATT_PV_BLOCKS) * ATT_PV_BLOCKS * kb, (j + 1) * kb
                c_j = cfac + _order_after(drun).astype(BF16)
                pc = [p_scr[cur, 0:tq, c0:c0 + LANE] - c_j * p_scr[cur, tq:2 * tq, c0:c0 + LANE]
                      for c0 in range(g0, g1, LANE)]
                acc = acc + _dot(jnp.concatenate(pc, axis=1), v_ref[g0:g1, :])
        mx_scr[cur] = jnp.broadcast_to(jnp.max(mrun, axis=-1, keepdims=True), mx_scr.shape[1:])
        den_scr[old] = jnp.broadcast_to(jnp.sum(drun, axis=-1, keepdims=True), den_scr.shape[1:])
        o_ref[...] = _attn_finish(acc / den[0:tq, 0:1], ng_ref[...], lam_init)

    @pl.when(g % 2 == 0)
    def _():
        step(0)

    @pl.when(g % 2 == 1)
    def _():
        step(1)


def _attention(u_da, lam_rows, norm_g_row, lam_init, n_ctx):
    b, n, w3 = u_da.shape
    wd = w3 // 3
    dv = wd // N_HEADS
    tq = ROW_TILE
    nh = N_HEADS
    n_ctx_tiles = n_ctx // tq
    out_shape = jax.ShapeDtypeStruct((b, n, wd), F32)
    o_ctx = pl.pallas_call(
        functools.partial(_attn_ctx_kernel, lam_init=lam_init),
        grid=(b, nh, n_ctx_tiles),
        in_specs=[pl.BlockSpec(lam_rows.shape, lambda bi, h, i: (0, 0)),
                  pl.BlockSpec((None, tq, dv), lambda bi, h, i: (bi, i, h)),
                  pl.BlockSpec((None, n_ctx, dv), lambda bi, h, i: (bi, 0, nh + h)),
                  pl.BlockSpec((None, n_ctx, dv), lambda bi, h, i: (bi, 0, 2 * nh + h)),
                  pl.BlockSpec((1, dv), lambda bi, h, i: (0, 0))],
        out_specs=pl.BlockSpec((None, tq, dv), lambda bi, h, i: (bi, i, h)),
        out_shape=out_shape,
        compiler_params=_cparams(("parallel", "parallel", "arbitrary")),
        name="diff_attn_ctx",
    )(lam_rows, u_da, u_da, u_da, norm_g_row)

    nq = (n - n_ctx) // tq
    n_tiles = b * nh * nq
    tile = lambda t: (t // (nh * nq), (t // nq) % nh, t % nq + n_ctx_tiles)
    head_tile = lambda g: tile(jnp.minimum(g, n_tiles - 1))
    tail_tile = lambda g: tile(jnp.clip(g - 2, 0, n_tiles - 1))

    def q_map(g):
        bi, h, i = head_tile(g)
        return bi, i, h

    def k_map(g):
        bi, h, _ = head_tile(g)
        return bi, 0, nh + h

    def v_map(g):
        bi, h, _ = tail_tile(g)
        return bi, 0, 2 * nh + h

    def o_map(g):
        bi, h, i = tail_tile(g)
        return bi, i, h

    return pl.pallas_call(
        functools.partial(_attn_lat_kernel, lam_init=lam_init),
        grid=(n_tiles + 2,),
        in_specs=[pl.BlockSpec(lam_rows.shape, lambda g: (0, 0)),
                  pl.BlockSpec((None, tq, dv), q_map),
                  pl.BlockSpec((None, n, dv), k_map),
                  pl.BlockSpec((None, n, dv), v_map),
                  pl.BlockSpec((1, dv), lambda g: (0, 0)),
                  pl.BlockSpec(memory_space=pl.ANY)],
        out_specs=pl.BlockSpec((None, tq, dv), o_map),
        out_shape=out_shape,
        input_output_aliases={5: 0},
        scratch_shapes=[pltpu.VMEM((2, 2 * tq, n), F32), pltpu.VMEM((2, 2 * tq, n), BF16),
                        pltpu.VMEM((2, 2 * tq, LANE), F32), pltpu.VMEM((2, 2 * tq, LANE), F32)],
        compiler_params=_cparams(("arbitrary",)),
        name="diff_attn",
    )(lam_rows, u_da, u_da, u_da, norm_g_row, o_ctx)


def _out_proj_kernel(s_ref, hg_ref, da_ref, rw_ref, mod_ref, g_ref, whg_ref, wda_ref, wrw_ref,
                     x_ref, h_ref):
    mix = (_dot(hg_ref[...].astype(BF16), whg_ref[...]) + _dot(da_ref[...].astype(BF16), wda_ref[...])
           + _dot(rw_ref[...].astype(BF16), wrw_ref[...]))
    x = s_ref[...] + mod_ref[2:3, :] * mix
    x_ref[...] = x
    ms = jnp.mean(x * x, axis=-1, keepdims=True)
    y = x * lax.rsqrt(ms + RMS_EPS) * g_ref[...]
    h_ref[...] = (y * (1.0 + mod_ref[4:5, :]) + mod_ref[3:4, :]).astype(BF16)


def _out_proj(s, o_hg, o_da, o_rw, mods_l, g, whg, wda, wrw, n_ctx_tiles):
    b, n, d = s.shape
    tm = ROW_TILE
    sel = lambda bi, i: (jnp.where(i < n_ctx_tiles, b, bi), 0, 0)
    full = lambda a: pl.BlockSpec(a.shape, lambda bi, i: (0, 0))
    row = lambda w: pl.BlockSpec((None, tm, w), lambda bi, i: (bi, i, 0))
    return pl.pallas_call(
        _out_proj_kernel,
        grid=(b, n // tm),
        in_specs=[row(d), row(o_hg.shape[-1]), row(o_da.shape[-1]), row(o_rw.shape[-1]),
                  pl.BlockSpec((None, 8, d), sel), full(g), full(whg), full(wda), full(wrw)],
        out_specs=[row(d), row(d)],
        out_shape=[jax.ShapeDtypeStruct((b, n, d), F32), jax.ShapeDtypeStruct((b, n, d), BF16)],
        compiler_params=_cparams(("parallel", "parallel")),
        name="out_proj",
    )(s, o_hg, o_da, o_rw, mods_l, g, whg, wda, wrw)


def _ffn_kernel(x_ref, h_ref, mod_ref, wg_ref, wu_ref, wd_ref, o_ref):
    h = h_ref[...]
    gate = _dot(h, wg_ref[...])
    up = _dot(h, wu_ref[...])
    act = (gate * _sigmoid(gate) * up).astype(BF16)
    o_ref[...] = x_ref[...] + mod_ref[5:6, :] * _dot(act, wd_ref[...])


def _ffn(x, h, mods_l, wg, wu, wd, n_ctx_tiles):
    b, n, d = x.shape
    tm = ROW_TILE
    sel = lambda bi, i: (jnp.where(i < n_ctx_tiles, b, bi), 0, 0)
    resident = lambda a: pl.BlockSpec(a.shape, lambda bi, i: (0, 0), pipeline_mode=pl.Buffered(1))
    row = pl.BlockSpec((None, tm, d), lambda bi, i: (bi, i, 0))
    return pl.pallas_call(
        _ffn_kernel,
        grid=(b, n // tm),
        in_specs=[row, row, pl.BlockSpec((None, 8, d), sel), resident(wg), resident(wu), resident(wd)],
        out_specs=row,
        out_shape=jax.ShapeDtypeStruct((b, n, d), F32),
        compiler_params=_cparams(("parallel", "parallel")),
        name="ffn",
    )(x, h, mods_l, wg, wu, wd)


def _final_norm_kernel(x_ref, g_ref, o_ref):
    x = x_ref[...]
    ms = jnp.mean(x * x, axis=-1, keepdims=True)
    o_ref[...] = x * lax.rsqrt(ms + RMS_EPS) * g_ref[...]


def _final_norm(s, g, n_ctx):
    b, n, d = s.shape
    tm = ROW_TILE
    off = n_ctx // tm
    return pl.pallas_call(
        _final_norm_kernel,
        grid=(b, (n - n_ctx) // tm),
        in_specs=[pl.BlockSpec((None, tm, d), lambda bi, i: (bi, i + off, 0)),
                  pl.BlockSpec((1, d), lambda bi, i: (0, 0))],
        out_specs=pl.BlockSpec((None, tm, d), lambda bi, i: (bi, i, 0)),
        out_shape=jax.ShapeDtypeStruct((b, n - n_ctx, d), F32),
        compiler_params=_cparams(("parallel", "parallel")),
        name="final_norm",
    )(s, g)


def _rope_tables(n_ctx, n_lat, dqk):
    n_freq = dqk // 4
    t = np.arange(n_lat)
    inv_freq = ROPE_BASE ** (-np.arange(n_freq, dtype=np.float32) / n_freq)
    ang = np.concatenate([(t // GRID_W)[:, None] * inv_freq, (t % GRID_W)[:, None] * inv_freq], axis=-1)
    reps = LANE // dqk
    cos = np.tile(np.concatenate([np.cos(ang), np.cos(ang)], axis=-1), (1, reps))
    sin = np.tile(np.concatenate([-np.sin(ang), np.sin(ang)], axis=-1), (1, reps))
    cos = np.concatenate([np.ones((n_ctx, LANE)), cos], axis=0)
    sin = np.concatenate([np.zeros((n_ctx, LANE)), sin], axis=0)
    return jnp.asarray(cos, F32), jnp.asarray(sin, F32)


def _pad_rows(a, start, total):
    return jnp.zeros((total, a.shape[-1]), a.dtype).at[start:start + a.shape[0]].set(a)


def kernel(x, c, ctx, c_ctx, ada_w, ada_b, norm1_g, norm2_g, w_in, w_out, hg_lb_logits, hg_norm_g, da_lam_q1, da_lam_k1, da_lam_q2, da_lam_k2, da_norm_g, rw_mu_prev, rw_mu_next, rw_w0, rw_w2, rw_a0, rw_a2, rw_g2, rw_k_k, rw_k_a, rw_r_k, rw_ln_w, rw_ln_b, ffn_w_gate, ffn_w_up, ffn_w_down, final_norm_g):
    b, n_lat, d = x.shape
    n_ctx = ctx.shape[1]
    depth = ada_w.shape[0]
    hg_w = hg_lb_logits.shape[-1]
    rw_w = rw_w0.shape[-1]
    da_w = w_out.shape[1] - hg_w - rw_w
    hg_cols, da_cols = 5 * hg_w, 3 * da_w
    rw_cols = w_in.shape[-1] - hg_cols - da_cols
    rw_pad = 3 * rw_w + 2 * LANE
    r_dec, r_icl, r_gate = rw_w2.shape[2], rw_a2.shape[2], rw_g2.shape[1]
    assert n_ctx % ROW_TILE == 0 and n_lat % ROW_TILE == 0 and n_lat % GRID_W == 0
    assert 2 * (r_dec + r_icl) == LANE and r_gate <= LANE and rw_cols == 3 * rw_w + LANE + r_gate
    n_ctx_tiles, n_ctx_chunks = n_ctx // ROW_TILE, n_ctx // CHUNK

    cond = jnp.zeros((16, d), F32).at[:b].set(c).at[b].set(c_ctx)
    mods = _ada_mod(cond, ada_w, ada_b).reshape(depth, 16, 6, d)
    mods = jnp.pad(mods, ((0, 0), (0, 0), (0, 2), (0, 0)))

    p_lb = jax.nn.softmax(hg_lb_logits.astype(F32), axis=1)
    lower_bounds = jnp.cumsum(p_lb, axis=1) - p_lb[:, :1]
    cos_t, sin_t = _rope_tables(n_ctx, n_lat, da_w // (2 * N_HEADS))

    s = jnp.concatenate([ctx, x], axis=1)
    for l in range(depth):
        w_l = w_in[l]
        whg = w_l[:, :hg_cols].astype(BF16)
        wda = w_l[:, hg_cols:hg_cols + da_cols].astype(BF16)
        wrw = jnp.pad(w_l[:, hg_cols + da_cols:], ((0, 0), (0, rw_pad - rw_cols))).astype(BF16)
        u_hg, u_da, u_rw = _in_proj(s, mods[l], norm1_g[l][None], cos_t, sin_t, whg, wda, wrw, n_ctx_tiles)

        o_b = _hgrn_pass(u_hg, lower_bounds[1, l][None], n_ctx_chunks, True)
        o_hg = _hgrn_pass(u_hg, lower_bounds[0, l][None], n_ctx_chunks, False, other=o_b,
                          norm_g_row=jnp.tile(hg_norm_g[l], N_HEADS)[None])

        lam_init = 0.8 - 0.6 * math.exp(-0.3 * l)
        lam_rows = jnp.zeros((8, LANE), F32).at[0:4, :da_lam_q1.shape[-1]].set(
            jnp.stack([da_lam_q1[l], da_lam_k1[l], da_lam_q2[l], da_lam_k2[l]]))
        o_da = _attention(u_da, lam_rows, da_norm_g[l][None], lam_init, n_ctx)

        mu_pad = lambda m: jnp.pad(m, (0, rw_pad - rw_cols))[None]
        off_dec, off_icl = 0, 2 * r_dec
        rw_p = {
            "mu_prev": mu_pad(rw_mu_prev[l]), "mu_next": mu_pad(rw_mu_next[l]),
            "w0": rw_w0[:, l][:, None, :], "a0": rw_a0[:, l][:, None, :],
            "w2": jnp.stack([_pad_rows(rw_w2[i, l], off_dec + i * r_dec, LANE) for i in range(2)]),
            "a2": jnp.stack([_pad_rows(rw_a2[i, l], off_icl + i * r_icl, LANE) for i in range(2)]),
            "g2": _pad_rows(rw_g2[l], 0, LANE),
            "k_k": rw_k_k[l][None], "k_a": rw_k_a[l][None], "r_k": rw_r_k[l][None],
            "ln_w": rw_ln_w[l][None], "ln_b": rw_ln_b[l][None],
        }
        yb = _rwkv_pass(u_rw, rw_p, n_ctx_chunks, True)
        o_rw = _rwkv_pass(u_rw, rw_p, n_ctx_chunks, False, other=yb)

        wo = w_out[l].astype(BF16)
        x_mid, h2 = _out_proj(s, o_hg, o_da, o_rw, mods[l], norm2_g[l][None],
                              wo[:hg_w], wo[hg_w:hg_w + da_w], wo[hg_w + da_w:], n_ctx_tiles)
        s = _ffn(x_mid, h2, mods[l], ffn_w_gate[l].astype(BF16), ffn_w_up[l].astype(BF16),
                 ffn_w_down[l].astype(BF16), n_ctx_tiles)
    return _final_norm(s, final_norm_g[None], n_ctx)
```

```python
import functools
import math

import numpy as np
import jax
import jax.numpy as jnp
from jax import lax
from jax.experimental import pallas as pl
from jax.experimental.pallas import tpu as pltpu

F32 = jnp.float32
BF16 = jnp.bfloat16
HI = lax.Precision.HIGHEST

N_HEADS = 4
GRID_W = 64
ROPE_BASE = 10000.0
RMS_EPS = 1e-6
RW_GN_EPS = 64e-5

ROW_TILE = 256
ATT_KEY_BLOCK = 256
ATT_PV_BLOCKS = 4
CHUNK = 64
REC_BATCH = 8
LANE = 128
VMEM_LIMIT = 56 * 1024 * 1024


def _cparams(sem):
    return pltpu.CompilerParams(dimension_semantics=sem, vmem_limit_bytes=VMEM_LIMIT)


def _dot_dims(a, b, dims, prec):
    dg = lambda x, y, p=None: lax.dot_general(x, y, (dims, ((), ())), preferred_element_type=F32, precision=p)
    if prec == "bf16":
        return dg(a.astype(BF16), b.astype(BF16))
    if prec == "x3":
        a_hi, b_hi = a.astype(BF16), b.astype(BF16)
        a_lo = (a - a_hi.astype(F32)).astype(BF16)
        b_lo = (b - b_hi.astype(F32)).astype(BF16)
        return dg(a_hi, b_hi) + (dg(a_lo, b_hi) + dg(a_hi, b_lo))
    return dg(a, b, prec)


def _dot(a, b, prec=None):
    return _dot_dims(a, b, ((1,), (0,)), prec)


def _dot_nt(a, b, prec=None):
    return _dot_dims(a, b, ((1,), (1,)), prec)


def _dot_tn(a, b, prec=None):
    return _dot_dims(a, b, ((0,), (0,)), prec)


P_HG = "bf16"
P_ST = "bf16"


def _iota(shape, dim):
    return lax.broadcasted_iota(jnp.int32, shape, dim)


def _div(x, d):
    assert d & (d - 1) == 0
    return x >> (d.bit_length() - 1)


def _mod(x, d):
    assert d & (d - 1) == 0
    return x & (d - 1)


def _softplus(x):
    return jnp.maximum(x, 0.0) + jnp.log(1.0 + jnp.exp(-jnp.abs(x)))


def _same_head(w, dh):
    return _div(_iota((w, w), 0), dh) == _div(_iota((w, w), 1), dh)


def _expand_heads(x, dh):
    w = x.shape[-1]
    head = _div(_iota((1, w), 1), dh)
    return jnp.concatenate([jnp.where(head == h, x, 0.0) for h in range(w // dh)], axis=0)


def _collapse_heads(x, c):
    out = x[0:c]
    for h in range(1, x.shape[0] // c):
        out = out + x[h * c:(h + 1) * c]
    return out


def _split3(x):
    hi = x.astype(BF16)
    r = x - hi.astype(F32)
    mid = r.astype(BF16)
    lo = (r - mid.astype(F32)).astype(BF16)
    return hi, mid, lo


def _split2(x):
    hi = x.astype(BF16)
    return hi, (x - hi.astype(F32)).astype(BF16)


def _head_sums(xs, dh):
    w = xs[0].shape[-1]
    ones_bd = _same_head(w, dh).astype(BF16)
    out = _dot(jnp.concatenate([p for x in xs for p in _split3(x)], axis=0), ones_bd)
    res, off = [], 0
    for x in xs:
        r = x.shape[0]
        res.append(out[off:off + r] + (out[off + r:off + 2 * r] + out[off + 2 * r:off + 3 * r]))
        off += 3 * r
    return res


def _dot_sel(sel3, x):
    return _dot(sel3, jnp.concatenate(_split3(x), axis=0))


def _dot_x3(x, w_hi, w_lo):
    x_hi, x_lo = _split2(x)
    r = x.shape[0]
    t = _dot(jnp.concatenate([x_hi, x_lo], axis=0), w_hi)
    return t[0:r] + t[r:2 * r] + _dot(x_hi, w_lo)


def _sigmoid(x):
    return 1.0 / (1.0 + jnp.exp(-x))


def _ada_kernel(c_ref, w_ref, b_ref, o_ref):
    c = c_ref[...]
    sc = c * _sigmoid(c)
    o_ref[...] = _dot(sc, w_ref[...], HI) + b_ref[...]


def _ada_mod(cond, ada_w, ada_b):
    depth, d, d6 = ada_w.shape
    rows = cond.shape[0]
    tn = 1536
    return pl.pallas_call(
        _ada_kernel,
        grid=(depth, d6 // tn),
        in_specs=[pl.BlockSpec((rows, d), lambda l, j: (0, 0)),
                  pl.BlockSpec((None, d, tn), lambda l, j: (l, 0, j)),
                  pl.BlockSpec((None, 1, tn), lambda l, j: (l, 0, j))],
        out_specs=pl.BlockSpec((None, rows, tn), lambda l, j: (l, 0, j)),
        out_shape=jax.ShapeDtypeStruct((depth, rows, d6), F32),
        compiler_params=_cparams(("arbitrary", "arbitrary")),
        name="ada_mod",
    )(cond, ada_w, ada_b.reshape(depth, 1, d6))


def _rope_cols(t, cos, sin_signed, half):
    lane = _iota((1, t.shape[-1]), 1)
    first = _mod(lane, 2 * half) < half
    swapped = jnp.where(first, pltpu.roll(t, t.shape[-1] - half, 1), pltpu.roll(t, half, 1))
    return t * cos + swapped * sin_signed


def _in_proj_kernel(s_ref, mod_ref, g_ref, cos_ref, sin_ref, whg_ref, wda_ref, wrw_ref,
                    uhg_ref, uqv_ref, ukt_ref, urw_ref, *, da_width, q_scale):
    x = s_ref[...]
    ms = jnp.mean(x * x, axis=-1, keepdims=True)
    y = x * lax.rsqrt(ms + RMS_EPS) * g_ref[...]
    h = (y * (1.0 + mod_ref[1:2, :]) + mod_ref[0:1, :]).astype(BF16)
    uhg_ref[...] = _dot(h, whg_ref[...])
    urw_ref[...] = _dot(h, wrw_ref[...])
    uda = _dot(h, wda_ref[...])
    cos = cos_ref[...]
    sin = sin_ref[...]
    dqk = da_width // (2 * N_HEADS)
    nq = da_width // LANE
    for j in range(2 * nq):
        t = _rope_cols(uda[:, j * LANE:(j + 1) * LANE], cos, sin, dqk // 2)
        if j < nq:
            uqv_ref[:, j * LANE:(j + 1) * LANE] = (t * q_scale).astype(BF16)
        else:
            ukt_ref[(j - nq) * LANE:(j - nq + 1) * LANE, :] = t.T.astype(BF16)
    uqv_ref[:, da_width:] = uda[:, 2 * da_width:].astype(BF16)


def _in_proj(s, mods_l, g, cos_t, sin_t, whg, wda, wrw, n_ctx_tiles):
    b, n, d = s.shape
    tm = ROW_TILE
    da_width = wda.shape[1] // 3
    dqk = da_width // (2 * N_HEADS)
    sel = lambda bi, i: (jnp.where(i < n_ctx_tiles, b, bi), 0, 0)
    full = lambda a: pl.BlockSpec(a.shape, lambda bi, i: (0, 0))
    row = lambda w: pl.BlockSpec((None, tm, w), lambda bi, i: (bi, i, 0))
    kern = functools.partial(_in_proj_kernel, da_width=da_width, q_scale=dqk ** -0.5 * math.log2(math.e))
    return pl.pallas_call(
        kern,
        grid=(b, n // tm),
        in_specs=[row(d), pl.BlockSpec((None, 8, d), sel), full(g),
                  pl.BlockSpec((tm, LANE), lambda bi, i: (i, 0)),
                  pl.BlockSpec((tm, LANE), lambda bi, i: (i, 0)),
                  full(whg), full(wda), full(wrw)],
        out_specs=[row(whg.shape[1]), row(2 * da_width),
                   pl.BlockSpec((None, da_width, tm), lambda bi, i: (bi, 0, i)), row(wrw.shape[1])],
        out_shape=[jax.ShapeDtypeStruct((b, n, whg.shape[1]), F32),
                   jax.ShapeDtypeStruct((b, n, 2 * da_width), BF16),
                   jax.ShapeDtypeStruct((b, da_width, n), BF16),
                   jax.ShapeDtypeStruct((b, n, wrw.shape[1]), F32)],
        compiler_params=_cparams(("parallel", "parallel")),
        name="in_proj",
    )(s, mods_l, g, cos_t, sin_t, whg, wda, wrw)


def _chunk_of_step(j, n_chunks, n_ctx_chunks, rev):
    if not rev:
        return j
    return jnp.where(j < n_ctx_chunks, n_ctx_chunks - 1 - j, n_chunks - 1 - (j - n_ctx_chunks))


def _hgrn_levels(c):
    return [1 << i for i in range(int(math.log2(c)))]


def _hgrn_const(c, rev):
    t = np.arange(c)[:, None]
    s = np.arange(c)[None, :]
    tri = (s >= t) if rev else (s <= t)
    mats = [tri.astype(np.float32)]
    for m in _hgrn_levels(c):
        p = (t // (2 * m)) * (2 * m) + (m if rev else m - 1)
        ref = (s >= p) if rev else (s <= p)
        mats.append(tri.astype(np.float32) - ref.astype(np.float32))
    return np.concatenate(mats, axis=0)


def _hgrn_kernel(*refs, rev, finish, dh):
    if finish:
        const_ref, q_ref, i_ref, f_ref, lb_ref, g_ref, other_ref, ng_ref, out_ref, st_ref = refs
    else:
        const_ref, q_ref, i_ref, f_ref, lb_ref, out_ref, st_ref = refs
    nb, c, w = q_ref.shape

    @pl.when(pl.program_id(1) == 0)
    def _():
        st_ref[...] = jnp.zeros_like(st_ref)

    nh = w // dh
    t_idx = _iota((c, 1), 0)
    s_idx = _mod(_iota((1, nh * c), 1), c)
    bd = _same_head(w, dh)
    lb = lb_ref[...]
    rows = lambda ref: ref[...].reshape(nb * c, w)
    q_all = rows(q_ref)
    v_all = rows(i_ref)
    f_all = lb + (1.0 - lb) * _sigmoid(rows(f_ref))
    kx_all = 1.0 - f_all
    logf_all = jnp.log(f_all)
    (diag_all,) = _head_sums([q_all * kx_all], dh)
    bf = lambda t: t.astype(BF16)
    cs = []
    for bb in range(nb):
        sl = slice(bb * c, (bb + 1) * c)
        seg = _dot_sel(const_ref[...], logf_all[sl])
        bcum = seg[0:c]
        cs.append(dict(bb=bb, q=q_all[sl], v=v_all[sl], kx=kx_all[sl], diag=diag_all[sl], seg=seg, bcum=bcum,
                       bend=bcum[0:1] if rev else bcum[c - 1:c], att=jnp.zeros((c, nh * c), F32)))
    for li, m in enumerate(_hgrn_levels(c)):
        t_up = (t_idx & m) != 0
        s_up = (s_idx & m) != 0
        t_isq = jnp.logical_not(t_up) if rev else t_up
        s_isk = s_up if rev else jnp.logical_not(s_up)
        valid = (_div(t_idx, 2 * m) == _div(s_idx, 2 * m)) & t_isq & s_isk
        for s in cs:
            e = jnp.exp(-jnp.abs(s["seg"][(li + 1) * c:(li + 2) * c]))
            xm = bf(jnp.where(t_isq, s["q"], s["kx"]) * e)
            sc = _dot_nt(xm, _expand_heads(xm, dh))
            s["att"] = s["att"] + jnp.where(valid, sc, 0.0)
    outs = []
    for s in cs:
        q, v, kx, bcum, bend = s["q"], s["v"], s["kx"], s["bcum"], s["bend"]
        o = _dot(bf(s["att"]), _expand_heads(bf(v), dh))
        o = o + s["diag"] * v
        st = st_ref[s["bb"]]
        o = o + _dot_nt(q * jnp.exp(bcum), st, P_HG)
        upd = _dot_tn(v, kx * jnp.exp(bend - bcum), P_HG)
        st_ref[s["bb"]] = st * jnp.exp(bend) + jnp.where(bd, upd, 0.0)
        outs.append(o)
    o_all = jnp.concatenate(outs, axis=0)
    if finish:
        o_all = o_all + rows(other_ref)
        (ms,) = _head_sums([o_all * o_all], dh)
        g = rows(g_ref)
        o_all = o_all * lax.rsqrt(ms * (1.0 / dh) + RMS_EPS) * ng_ref[...] * (g * _sigmoid(g))
    out_ref[...] = o_all.reshape(nb, c, w)


def _hgrn_pass(u_hg, lb_row, n_ctx_chunks, rev, other=None, norm_g_row=None):
    b, n, _ = u_hg.shape
    c = CHUNK
    w = lb_row.shape[-1]
    dh = w // N_HEADS
    nc = n // c
    finish = other is not None
    const = jnp.asarray(np.tile(_hgrn_const(c, rev), (1, 3)), BF16)
    chunk = lambda j: _chunk_of_step(j, nc, n_ctx_chunks, rev)
    nb = math.gcd(b, REC_BATCH)
    col = lambda k: pl.BlockSpec((nb, c, w), lambda bi, j: (bi, chunk(j), k))
    vec = pl.BlockSpec((1, w), lambda bi, j: (0, 0))
    in_specs = [pl.BlockSpec(const.shape, lambda bi, j: (0, 0)), col(0), col(1), col(3 if rev else 2), vec]
    args = [const, u_hg, u_hg, u_hg, lb_row]
    if finish:
        in_specs += [col(4), col(0), vec]
        args += [u_hg, other, norm_g_row]
    return pl.pallas_call(
        functools.partial(_hgrn_kernel, rev=rev, finish=finish, dh=dh),
        grid=(b // nb, nc),
        in_specs=in_specs,
        out_specs=col(0),
        out_shape=jax.ShapeDtypeStruct((b, n, w), F32),
        scratch_shapes=[pltpu.VMEM((nb, w, w), F32)],
        compiler_params=_cparams(("parallel", "arbitrary")),
        name="hgrn_bwd" if rev else "hgrn_fwd",
    )(*args)


def _rwkv_kernel(*refs, rev, finish, dh, n_chunks, n_ctx_chunks):
    if finish:
        (tri_ref, z_ref, zp_ref, zn_ref, mup_ref, mun_ref, w0_ref, w2_ref, a0_ref, a2_ref, g2_ref,
         kk_ref, ka_ref, rk_ref, oy_ref, ob_ref, lnw_ref, lnb_ref, out_ref, st_ref) = refs
    else:
        (tri_ref, z_ref, zp_ref, zn_ref, mup_ref, mun_ref, w0_ref, w2_ref, a0_ref, a2_ref, g2_ref,
         kk_ref, ka_ref, rk_ref, y_ref, bonus_ref, st_ref) = refs
    nb, c, _ = z_ref.shape
    w = w0_ref.shape[-1]
    j = pl.program_id(1)
    chunk = _chunk_of_step(j, n_chunks, n_ctx_chunks, rev)

    @pl.when(j == 0)
    def _():
        st_ref[...] = jnp.zeros_like(st_ref)

    row = _iota((c, 1), 0)
    seg_first = (chunk == 0) | (chunk == n_ctx_chunks)
    seg_last = (chunk == n_ctx_chunks - 1) | (chunk == n_chunks - 1)
    nr = (w // dh) * c
    si = _mod(_iota((1, nr), 1), c)
    strict = (si > row) if rev else (si < row)
    incl = (si >= row) if rev else (si <= row)
    eye_cat = (si == row).astype(F32)
    blk = _same_head(nr, c)
    bd = _same_head(w, dh)
    eye_w = (_iota((w, w), 0) == _iota((w, w), 1)).astype(F32)

    us = []
    for bb in range(nb):
        z = z_ref[bb]
        prev_row = jnp.where(seg_first, 0.0, zp_ref[bb, 7:8, :])
        next_row = jnp.where(seg_last, 0.0, zn_ref[bb, 0:1, :])
        z_prev = jnp.where(row == 0, prev_row, pltpu.roll(z, 1, 0))
        z_next = jnp.where(row == c - 1, next_row, pltpu.roll(z, c - 1, 0))
        us.append(z + mup_ref[...] * (z_prev - z) + mun_ref[...] * (z_next - z))
    u = jnp.concatenate(us, axis=0)

    r_all = u[:, 0:w]
    k_all = u[:, w:2 * w]
    v_all = u[:, 2 * w:3 * w]
    lora = u[:, 3 * w:3 * w + LANE]
    gd = u[:, 3 * w + LANE:3 * w + 2 * LANE]

    w_log = -_softplus(-(w0_ref[...] + _dot_x3(jnp.tanh(lora), w2_ref[0], w2_ref[1]))) - 0.5
    lw_all = -jnp.exp(w_log)
    a_lr = _sigmoid(a0_ref[...] + _dot_x3(lora, a2_ref[0], a2_ref[1]))
    kk = k_all * kk_ref[...]
    kd_all = k_all * (1.0 + (a_lr - 1.0) * ka_ref[...])
    kk_sq, bonus_dot = _head_sums([kk * kk, r_all * kd_all * rk_ref[...]], dh)
    kk = kk * lax.rsqrt(jnp.maximum(kk_sq, 1e-24))
    a_all = -kk
    b_all = kk * a_lr
    bonus_all = bonus_dot * v_all

    chains = range(nb)
    bf = lambda t: t.astype(BF16)
    ex = lambda t: _expand_heads(bf(t), dh)
    st = []
    for bb in chains:
        sl = slice(bb * c, (bb + 1) * c)
        r, v, kd, lw = r_all[sl], v_all[sl], kd_all[sl], lw_all[sl]
        cum = _dot_sel(tri_ref[...], lw)
        tot = cum[0:1] if rev else cum[c - 1:c]
        inv_gam = jnp.exp(-cum)
        rest = jnp.exp(tot - cum)
        r_t = r * jnp.exp(cum)
        a_t = a_all[sl] * jnp.exp(cum - lw)
        ea, eb, ek, ev = ex(a_t), ex(b_all[sl] * inv_gam), ex(kd * inv_gam), ex(v)
        sc = _dot_nt(jnp.concatenate([bf(a_t), bf(r_t)], axis=0), jnp.concatenate([eb, ek], axis=0))
        st.append(dict(
            v=v, r_t=r_t, tot=tot, ea=ea, ev=ev, b_h=b_all[sl] * rest, k_h=kd * rest,
            a_ab=jnp.where(strict, sc[0:c, 0:nr], 0.0), a_ak=jnp.where(strict, sc[0:c, nr:2 * nr], 0.0),
            a_rb=jnp.where(incl, sc[c:2 * c, 0:nr], 0.0), a_rk=jnp.where(incl, sc[c:2 * c, nr:2 * nr], 0.0)))

    bd_cat = lambda t: jnp.where(blk, jnp.concatenate([bf(t)] * (nr // c), axis=0), 0.0)
    for s in st:
        s["pw"] = _dot(bf(s["a_ab"]), bd_cat(s["a_ab"]))
        s["t_cat"] = eye_cat + s["a_ab"]
    n_iter = int(math.log2(c)) - 1
    for it in range(n_iter):
        for s in st:
            pw = bf(s["pw"])
            if it + 1 < n_iter:
                both = _dot(pw, jnp.concatenate([bd_cat(s["t_cat"]), bd_cat(s["pw"])], axis=1))
                s["t_cat"], s["pw"] = s["t_cat"] + both[:, 0:nr], both[:, nr:2 * nr]
            else:
                s["t_cat"] = s["t_cat"] + _dot(pw, bd_cat(s["t_cat"]))

    for s in st:
        s["t_cat"] = bf(s["t_cat"])
        s["x_c"] = _dot(bf(s["a_ak"]), s["ev"])
    for s in st:
        s["w1c"] = _dot(s["t_cat"], s["ea"])
        s["w2c"] = _dot(s["t_cat"], ex(s["x_c"]))
    for s in st:
        a_rb = bf(s["a_rb"])
        s["p_mat"] = s["r_t"] + _dot(a_rb, ex(s["w1c"]))
        s["y0"] = _dot(jnp.concatenate([a_rb, bf(s["a_rk"])], axis=1),
                       jnp.concatenate([ex(s["w2c"]), s["ev"]], axis=0))
    ys = []
    for bb, s in zip(chains, st):
        lhs = jnp.concatenate([s["w1c"], s["p_mat"], eye_w * jnp.exp(s["tot"])], axis=0)
        big = _dot(lhs, st_ref[bb], P_ST)
        ys.append(big[c:2 * c] + s["y0"])
        upd = _dot_tn(jnp.concatenate([s["b_h"], s["k_h"]], axis=0),
                      jnp.concatenate([big[0:c] + s["w2c"], s["v"]], axis=0), P_ST)
        st_ref[bb] = big[2 * c:] + jnp.where(bd, upd, 0.0)

    y = jnp.concatenate(ys, axis=0)
    if finish:
        y = y + oy_ref[...].reshape(nb * c, w)
        (mean,) = _head_sums([y], dh)
        yc = y - mean * (1.0 / dh)
        (var,) = _head_sums([yc * yc], dh)
        yn = yc * lax.rsqrt(var * (1.0 / dh) + RW_GN_EPS) * lnw_ref[...] + lnb_ref[...]
        gate = _dot_x3(_sigmoid(gd), g2_ref[0], g2_ref[1])
        out = (yn + bonus_all + ob_ref[...].reshape(nb * c, w)) * gate
        out_ref[...] = out.reshape(nb, c, w)
    else:
        y_ref[...] = y.reshape(nb, c, w)
        bonus_ref[...] = bonus_all.reshape(nb, c, w)


def _rwkv_pass(u_rw, p, n_ctx_chunks, rev, other=None):
    b, n, wu = u_rw.shape
    c = CHUNK
    w = p["w0"].shape[-1]
    dh = w // N_HEADS
    nc = n // c
    finish = other is not None
    t = np.arange(c)
    tri = (t[None, :] >= t[:, None]) if rev else (t[None, :] <= t[:, None])
    tri = jnp.asarray(np.tile(tri.astype(np.float32), (1, 3)), BF16)
    hi_lo = lambda m: jnp.stack(_split2(m))
    d = 1 if rev else 0
    chunk = lambda j: _chunk_of_step(j, nc, n_ctx_chunks, rev)
    sub = c // 8
    nb = math.gcd(b, REC_BATCH)
    const = lambda a: pl.BlockSpec(a.shape, lambda bi, j: (0,) * a.ndim)
    col = pl.BlockSpec((nb, c, w), lambda bi, j: (bi, chunk(j), 0))
    args = [tri, u_rw, u_rw, u_rw, p["mu_prev"], p["mu_next"], p["w0"][d], hi_lo(p["w2"][d]), p["a0"][d],
            hi_lo(p["a2"][d]), hi_lo(p["g2"]), p["k_k"], p["k_a"], p["r_k"]]
    in_specs = [const(tri),
                pl.BlockSpec((nb, c, wu), lambda bi, j: (bi, chunk(j), 0)),
                pl.BlockSpec((nb, 8, wu), lambda bi, j: (bi, jnp.maximum(chunk(j) * sub - 1, 0), 0)),
                pl.BlockSpec((nb, 8, wu), lambda bi, j: (bi, jnp.minimum((chunk(j) + 1) * sub, n // 8 - 1), 0))]
    in_specs += [const(a) for a in args[4:]]
    if finish:
        args += [other[0], other[1], p["ln_w"], p["ln_b"]]
        in_specs += [col, col, const(p["ln_w"]), const(p["ln_b"])]
        out_specs, out_shape = col, jax.ShapeDtypeStruct((b, n, w), F32)
    else:
        out_specs = [col, col]
        out_shape = [jax.ShapeDtypeStruct((b, n, w), F32)] * 2
    return pl.pallas_call(
        functools.partial(_rwkv_kernel, rev=rev, finish=finish, dh=dh, n_chunks=nc, n_ctx_chunks=n_ctx_chunks),
        grid=(b // nb, nc),
        in_specs=in_specs,
        out_specs=out_specs,
        out_shape=out_shape,
        scratch_shapes=[pltpu.VMEM((nb, w, w), F32)],
        compiler_params=_cparams(("parallel", "arbitrary")),
        name="rwkv_bwd" if rev else "rwkv_fwd",
    )(*args)


def _attn_lam(lam_ref, lam_init):
    lp = lam_ref[...]
    return (jnp.exp(jnp.sum(lp[0:1] * lp[1:2], axis=-1, keepdims=True))
            - jnp.exp(jnp.sum(lp[2:3] * lp[3:4], axis=-1, keepdims=True)) + lam_init)


def _stack_maps(q):
    half = q.shape[-1] // 2
    lane = _iota((1, q.shape[-1]), 1)
    return jnp.concatenate([jnp.where(lane < half, q, 0), jnp.where(lane >= half, q, 0)], axis=0)


def _order_after(x):
    bits = lax.bitcast_convert_type(x[0:8, 0:LANE], jnp.int32)
    zero = lax.shift_right_logical(lax.shift_right_logical(bits, 16), 16)
    return zero[0:1, :].astype(F32)


def _attn_finish(o, ng, lam_init):
    ms = jnp.mean(o * o, axis=-1, keepdims=True)
    return o * lax.rsqrt(ms + RMS_EPS) * ng * (1.0 - lam_init)


def _attn_ctx_kernel(lam_ref, q_ref, k_ref, v_ref, ng_ref, o_ref, *, lam_init):
    lam = _attn_lam(lam_ref, lam_init)
    tq = q_ref.shape[0]
    s = _dot(_stack_maps(q_ref[...]), k_ref[...])
    p = jnp.exp2(s - jnp.max(s, axis=-1, keepdims=True))
    den = jnp.sum(p, axis=-1, keepdims=True)
    pc = p[0:tq] - p[tq:2 * tq] * (lam * den[0:tq] / den[tq:2 * tq])
    o_ref[...] = _attn_finish(_dot(pc.astype(BF16), v_ref[...]) / den[0:tq], ng_ref[...], lam_init)


def _attn_lat_kernel(lam_ref, q_ref, k_ref, v_ref, ng_ref, prev_ref, o_ref, s_scr, p_scr, mx_scr, den_scr,
                     *, lam_init):
    del prev_ref
    g = pl.program_id(0)
    tq, dv = q_ref.shape
    nk = k_ref.shape[1]
    kb = ATT_KEY_BLOCK
    lam = _attn_lam(lam_ref, lam_init)

    @pl.when(g == 0)
    def _():
        s_scr[...] = jnp.zeros_like(s_scr)
        p_scr[...] = jnp.zeros_like(p_scr)
        mx_scr[...] = jnp.zeros_like(mx_scr)
        den_scr[...] = jnp.ones_like(den_scr)

    def step(cur):
        old = 1 - cur
        qs = _stack_maps(q_ref[...])
        m_prev = mx_scr[old]
        den = den_scr[cur]
        cfac = (lam * den[0:tq] / den[tq:2 * tq]).astype(BF16)
        mrun = drun = None
        acc = jnp.zeros((tq, dv), F32)
        nblk = nk // kb
        for j in range(nblk):
            s_j = _dot(qs, k_ref[:, j * kb:(j + 1) * kb])
            s_scr[cur, :, j * kb:(j + 1) * kb] = s_j
            for c0 in range(0, kb, LANE):
                piece = s_j[:, c0:c0 + LANE]
                mrun = piece if mrun is None else jnp.maximum(mrun, piece)
            m_j = m_prev + _order_after(mrun) if j % 2 == 0 else m_prev
            e_sum = None
            for c0 in range(j * kb, (j + 1) * kb, LANE):
                e = jnp.exp2((s_scr[old, :, c0:c0 + LANE] - m_j).astype(BF16))
                e_sum = e if e_sum is None else e_sum + e
                p_scr[old, :, c0:c0 + LANE] = e
            drun = e_sum.astype(F32) if drun is None else drun + e_sum.astype(F32)
            if (j + 1) % ATT_PV_BLOCKS == 0 or j + 1 == nblk:
                g0, g1 = (j // ATT_PV_BLOCKS) * ATT_PV_BLOCKS * kb, (j + 1) * kb
                c_j = cfac + _order_after(drun).astype(BF16)
                pc = [p_scr[cur, 0:tq, c0:c0 + LANE] - c_j * p_scr[cur, tq:2 * tq, c0:c0 + LANE]
                      for c0 in range(g0, g1, LANE)]
                acc = acc + _dot(jnp.concatenate(pc, axis=1), v_ref[g0:g1, :])
        mx_scr[cur] = jnp.broadcast_to(jnp.max(mrun, axis=-1, keepdims=True), mx_scr.shape[1:])
        den_scr[old] = jnp.broadcast_to(jnp.sum(drun, axis=-1, keepdims=True), den_scr.shape[1:])
        o_ref[...] = _attn_finish(acc / den[0:tq, 0:1], ng_ref[...], lam_init)

    @pl.when(g % 2 == 0)
    def _():
        step(0)

    @pl.when(g % 2 == 1)
    def _():
        step(1)


def _attention(u_qv, u_kt, lam_rows, norm_g_row, lam_init, n_ctx):
    b, n, w2 = u_qv.shape
    wd = w2 // 2
    dv = wd // N_HEADS
    tq = ROW_TILE
    nh = N_HEADS
    n_ctx_tiles = n_ctx // tq
    out_shape = jax.ShapeDtypeStruct((b, n, wd), F32)
    o_ctx = pl.pallas_call(
        functools.partial(_attn_ctx_kernel, lam_init=lam_init),
        grid=(b, nh, n_ctx_tiles),
        in_specs=[pl.BlockSpec(lam_rows.shape, lambda bi, h, i: (0, 0)),
                  pl.BlockSpec((None, tq, dv), lambda bi, h, i: (bi, i, h)),
                  pl.BlockSpec((None, dv, n_ctx), lambda bi, h, i: (bi, h, 0)),
                  pl.BlockSpec((None, n_ctx, dv), lambda bi, h, i: (bi, 0, nh + h)),
                  pl.BlockSpec((1, dv), lambda bi, h, i: (0, 0))],
        out_specs=pl.BlockSpec((None, tq, dv), lambda bi, h, i: (bi, i, h)),
        out_shape=out_shape,
        compiler_params=_cparams(("parallel", "parallel", "arbitrary")),
        name="diff_attn_ctx",
    )(lam_rows, u_qv, u_kt, u_qv, norm_g_row)

    nq = (n - n_ctx) // tq
    n_tiles = b * nh * nq
    tile = lambda t: (t // (nh * nq), (t // nq) % nh, t % nq + n_ctx_tiles)
    head_tile = lambda g: tile(jnp.minimum(g, n_tiles - 1))
    tail_tile = lambda g: tile(jnp.clip(g - 2, 0, n_tiles - 1))

    def q_map(g):
        bi, h, i = head_tile(g)
        return bi, i, h

    def k_map(g):
        bi, h, _ = head_tile(g)
        return bi, h, 0

    def v_map(g):
        bi, h, _ = tail_tile(g)
        return bi, 0, nh + h

    def o_map(g):
        bi, h, i = tail_tile(g)
        return bi, i, h

    return pl.pallas_call(
        functools.partial(_attn_lat_kernel, lam_init=lam_init),
        grid=(n_tiles + 2,),
        in_specs=[pl.BlockSpec(lam_rows.shape, lambda g: (0, 0)),
                  pl.BlockSpec((None, tq, dv), q_map),
                  pl.BlockSpec((None, dv, n), k_map),
                  pl.BlockSpec((None, n, dv), v_map),
                  pl.BlockSpec((1, dv), lambda g: (0, 0)),
                  pl.BlockSpec(memory_space=pl.ANY)],
        out_specs=pl.BlockSpec((None, tq, dv), o_map),
        out_shape=out_shape,
        input_output_aliases={5: 0},
        scratch_shapes=[pltpu.VMEM((2, 2 * tq, n), F32), pltpu.VMEM((2, 2 * tq, n), BF16),
                        pltpu.VMEM((2, 2 * tq, LANE), F32), pltpu.VMEM((2, 2 * tq, LANE), F32)],
        compiler_params=_cparams(("arbitrary",)),
        name="diff_attn",
    )(lam_rows, u_qv, u_kt, u_qv, norm_g_row, o_ctx)


def _out_proj_kernel(s_ref, hg_ref, da_ref, rw_ref, mod_ref, g_ref, whg_ref, wda_ref, wrw_ref,
                     x_ref, h_ref):
    mix = (_dot(hg_ref[...].astype(BF16), whg_ref[...]) + _dot(da_ref[...].astype(BF16), wda_ref[...])
           + _dot(rw_ref[...].astype(BF16), wrw_ref[...]))
    x = s_ref[...] + mod_ref[2:3, :] * mix
    x_ref[...] = x
    ms = jnp.mean(x * x, axis=-1, keepdims=True)
    y = x * lax.rsqrt(ms + RMS_EPS) * g_ref[...]
    h_ref[...] = (y * (1.0 + mod_ref[4:5, :]) + mod_ref[3:4, :]).astype(BF16)


def _out_proj(s, o_hg, o_da, o_rw, mods_l, g, whg, wda, wrw, n_ctx_tiles):
    b, n, d = s.shape
    tm = ROW_TILE
    sel = lambda bi, i: (jnp.where(i < n_ctx_tiles, b, bi), 0, 0)
    full = lambda a: pl.BlockSpec(a.shape, lambda bi, i: (0, 0))
    row = lambda w: pl.BlockSpec((None, tm, w), lambda bi, i: (bi, i, 0))
    return pl.pallas_call(
        _out_proj_kernel,
        grid=(b, n // tm),
        in_specs=[row(d), row(o_hg.shape[-1]), row(o_da.shape[-1]), row(o_rw.shape[-1]),
                  pl.BlockSpec((None, 8, d), sel), full(g), full(whg), full(wda), full(wrw)],
        out_specs=[row(d), row(d)],
        out_shape=[jax.ShapeDtypeStruct((b, n, d), F32), jax.ShapeDtypeStruct((b, n, d), BF16)],
        compiler_params=_cparams(("parallel", "parallel")),
        name="out_proj",
    )(s, o_hg, o_da, o_rw, mods_l, g, whg, wda, wrw)


def _ffn_kernel(x_ref, h_ref, mod_ref, wg_ref, wu_ref, wd_ref, o_ref):
    h = h_ref[...]
    gate = _dot(h, wg_ref[...])
    up = _dot(h, wu_ref[...])
    act = (gate * _sigmoid(gate) * up).astype(BF16)
    o_ref[...] = x_ref[...] + mod_ref[5:6, :] * _dot(act, wd_ref[...])


def _ffn(x, h, mods_l, wg, wu, wd, n_ctx_tiles):
    b, n, d = x.shape
    tm = ROW_TILE
    sel = lambda bi, i: (jnp.where(i < n_ctx_tiles, b, bi), 0, 0)
    resident = lambda a: pl.BlockSpec(a.shape, lambda bi, i: (0, 0), pipeline_mode=pl.Buffered(1))
    row = pl.BlockSpec((None, tm, d), lambda bi, i: (bi, i, 0))
    return pl.pallas_call(
        _ffn_kernel,
        grid=(b, n // tm),
        in_specs=[row, row, pl.BlockSpec((None, 8, d), sel), resident(wg), resident(wu), resident(wd)],
        out_specs=row,
        out_shape=jax.ShapeDtypeStruct((b, n, d), F32),
        compiler_params=_cparams(("parallel", "parallel")),
        name="ffn",
    )(x, h, mods_l, wg, wu, wd)


def _final_norm_kernel(x_ref, g_ref, o_ref):
    x = x_ref[...]
    ms = jnp.mean(x * x, axis=-1, keepdims=True)
    o_ref[...] = x * lax.rsqrt(ms + RMS_EPS) * g_ref[...]


def _final_norm(s, g, n_ctx):
    b, n, d = s.shape
    tm = ROW_TILE
    off = n_ctx // tm
    return pl.pallas_call(
        _final_norm_kernel,
        grid=(b, (n - n_ctx) // tm),
        in_specs=[pl.BlockSpec((None, tm, d), lambda bi, i: (bi, i + off, 0)),
                  pl.BlockSpec((1, d), lambda bi, i: (0, 0))],
        out_specs=pl.BlockSpec((None, tm, d), lambda bi, i: (bi, i, 0)),
        out_shape=jax.ShapeDtypeStruct((b, n - n_ctx, d), F32),
        compiler_params=_cparams(("parallel", "parallel")),
        name="final_norm",
    )(s, g)


def _rope_tables(n_ctx, n_lat, dqk):
    n_freq = dqk // 4
    t = np.arange(n_lat)
    inv_freq = ROPE_BASE ** (-np.arange(n_freq, dtype=np.float32) / n_freq)
    ang = np.concatenate([(t // GRID_W)[:, None] * inv_freq, (t % GRID_W)[:, None] * inv_freq], axis=-1)
    reps = LANE // dqk
    cos = np.tile(np.concatenate([np.cos(ang), np.cos(ang)], axis=-1), (1, reps))
    sin = np.tile(np.concatenate([-np.sin(ang), np.sin(ang)], axis=-1), (1, reps))
    cos = np.concatenate([np.ones((n_ctx, LANE)), cos], axis=0)
    sin = np.concatenate([np.zeros((n_ctx, LANE)), sin], axis=0)
    return jnp.asarray(cos, F32), jnp.asarray(sin, F32)


def _pad_rows(a, start, total):
    return jnp.zeros((total, a.shape[-1]), a.dtype).at[start:start + a.shape[0]].set(a)


def kernel(x, c, ctx, c_ctx, ada_w, ada_b, norm1_g, norm2_g, w_in, w_out, hg_lb_logits, hg_norm_g, da_lam_q1, da_lam_k1, da_lam_q2, da_lam_k2, da_norm_g, rw_mu_prev, rw_mu_next, rw_w0, rw_w2, rw_a0, rw_a2, rw_g2, rw_k_k, rw_k_a, rw_r_k, rw_ln_w, rw_ln_b, ffn_w_gate, ffn_w_up, ffn_w_down, final_norm_g):
    b, n_lat, d = x.shape
    n_ctx = ctx.shape[1]
    depth = ada_w.shape[0]
    hg_w = hg_lb_logits.shape[-1]
    rw_w = rw_w0.shape[-1]
    da_w = w_out.shape[1] - hg_w - rw_w
    hg_cols, da_cols = 5 * hg_w, 3 * da_w
    rw_cols = w_in.shape[-1] - hg_cols - da_cols
    rw_pad = 3 * rw_w + 2 * LANE
    r_dec, r_icl, r_gate = rw_w2.shape[2], rw_a2.shape[2], rw_g2.shape[1]
    assert n_ctx % ROW_TILE == 0 and n_lat % ROW_TILE == 0 and n_lat % GRID_W == 0
    assert 2 * (r_dec + r_icl) == LANE and r_gate <= LANE and rw_cols == 3 * rw_w + LANE + r_gate
    n_ctx_tiles, n_ctx_chunks = n_ctx // ROW_TILE, n_ctx // CHUNK

    cond = jnp.zeros((16, d), F32).at[:b].set(c).at[b].set(c_ctx)
    mods = _ada_mod(cond, ada_w, ada_b).reshape(depth, 16, 6, d)
    mods = jnp.pad(mods, ((0, 0), (0, 0), (0, 2), (0, 0)))

    p_lb = jax.nn.softmax(hg_lb_logits.astype(F32), axis=1)
    lower_bounds = jnp.cumsum(p_lb, axis=1) - p_lb[:, :1]
    cos_t, sin_t = _rope_tables(n_ctx, n_lat, da_w // (2 * N_HEADS))

    s = jnp.concatenate([ctx, x], axis=1)
    for l in range(depth):
        w_l = w_in[l]
        whg = w_l[:, :hg_cols].astype(BF16)
        wda = w_l[:, hg_cols:hg_cols + da_cols].astype(BF16)
        wrw = jnp.pad(w_l[:, hg_cols + da_cols:], ((0, 0), (0, rw_pad - rw_cols))).astype(BF16)
        u_hg, u_qv, u_kt, u_rw = _in_proj(s, mods[l], norm1_g[l][None], cos_t, sin_t, whg, wda, wrw, n_ctx_tiles)

        o_b = _hgrn_pass(u_hg, lower_bounds[1, l][None], n_ctx_chunks, True)
        o_hg = _hgrn_pass(u_hg, lower_bounds[0, l][None], n_ctx_chunks, False, other=o_b,
                          norm_g_row=jnp.tile(hg_norm_g[l], N_HEADS)[None])

        lam_init = 0.8 - 0.6 * math.exp(-0.3 * l)
        lam_rows = jnp.zeros((8, LANE), F32).at[0:4, :da_lam_q1.shape[-1]].set(
            jnp.stack([da_lam_q1[l], da_lam_k1[l], da_lam_q2[l], da_lam_k2[l]]))
        o_da = _attention(u_qv, u_kt, lam_rows, da_norm_g[l][None], lam_init, n_ctx)

        mu_pad = lambda m: jnp.pad(m, (0, rw_pad - rw_cols))[None]
        off_dec, off_icl = 0, 2 * r_dec
        rw_p = {
            "mu_prev": mu_pad(rw_mu_prev[l]), "mu_next": mu_pad(rw_mu_next[l]),
            "w0": rw_w0[:, l][:, None, :], "a0": rw_a0[:, l][:, None, :],
            "w2": jnp.stack([_pad_rows(rw_w2[i, l], off_dec + i * r_dec, LANE) for i in range(2)]),
            "a2": jnp.stack([_pad_rows(rw_a2[i, l], off_icl + i * r_icl, LANE) for i in range(2)]),
            "g2": _pad_rows(rw_g2[l], 0, LANE),
            "k_k": rw_k_k[l][None], "k_a": rw_k_a[l][None], "r_k": rw_r_k[l][None],
            "ln_w": rw_ln_w[l][None], "ln_b": rw_ln_b[l][None],
        }
        yb = _rwkv_pass(u_rw, rw_p, n_ctx_chunks, True)
        o_rw = _rwkv_pass(u_rw, rw_p, n_ctx_chunks, False, other=yb)

        wo = w_out[l].astype(BF16)
        x_mid, h2 = _out_proj(s, o_hg, o_da, o_rw, mods[l], norm2_g[l][None],
                              wo[:hg_w], wo[hg_w:hg_w + da_w], wo[hg_w + da_w:], n_ctx_tiles)
        s = _ffn(x_mid, h2, mods[l], ffn_w_gate[l].astype(BF16), ffn_w_up[l].astype(BF16),
                 ffn_w_down[l].astype(BF16), n_ctx_tiles)
    return _final_norm(s, final_norm_g[None], n_ctx)
```

```python
import functools
import math

import numpy as np
import jax
import jax.numpy as jnp
from jax import lax
from jax.experimental import pallas as pl
from jax.experimental.pallas import tpu as pltpu

F32 = jnp.float32
BF16 = jnp.bfloat16
HI = lax.Precision.HIGHEST

N_HEADS = 4
GRID_W = 64
ROPE_BASE = 10000.0
RMS_EPS = 1e-6
RW_GN_EPS = 64e-5

ROW_TILE = 256
ATT_SUB = 128
CHUNK = 64
REC_BATCH = 8
LANE = 128
VMEM_LIMIT = 56 * 1024 * 1024


def _cparams(sem):
    return pltpu.CompilerParams(dimension_semantics=sem, vmem_limit_bytes=VMEM_LIMIT)


def _dot_dims(a, b, dims, prec):
    dg = lambda x, y, p=None: lax.dot_general(x, y, (dims, ((), ())), preferred_element_type=F32, precision=p)
    if prec == "bf16":
        return dg(a.astype(BF16), b.astype(BF16))
    if prec == "x3":
        a_hi, b_hi = a.astype(BF16), b.astype(BF16)
        a_lo = (a - a_hi.astype(F32)).astype(BF16)
        b_lo = (b - b_hi.astype(F32)).astype(BF16)
        return dg(a_hi, b_hi) + (dg(a_lo, b_hi) + dg(a_hi, b_lo))
    return dg(a, b, prec)


def _dot(a, b, prec=None):
    return _dot_dims(a, b, ((1,), (0,)), prec)


def _dot_nt(a, b, prec=None):
    return _dot_dims(a, b, ((1,), (1,)), prec)


def _dot_tn(a, b, prec=None):
    return _dot_dims(a, b, ((0,), (0,)), prec)


P_HG = "bf16"
P_ST = "bf16"


def _iota(shape, dim):
    return lax.broadcasted_iota(jnp.int32, shape, dim)


def _div(x, d):
    assert d & (d - 1) == 0
    return x >> (d.bit_length() - 1)


def _mod(x, d):
    assert d & (d - 1) == 0
    return x & (d - 1)


def _softplus(x):
    return jnp.maximum(x, 0.0) + jnp.log(1.0 + jnp.exp(-jnp.abs(x)))


def _same_head(w, dh):
    return _div(_iota((w, w), 0), dh) == _div(_iota((w, w), 1), dh)


def _expand_heads(x, dh):
    w = x.shape[-1]
    head = _div(_iota((1, w), 1), dh)
    return jnp.concatenate([jnp.where(head == h, x, 0.0) for h in range(w // dh)], axis=0)


def _collapse_heads(x, c):
    out = x[0:c]
    for h in range(1, x.shape[0] // c):
        out = out + x[h * c:(h + 1) * c]
    return out


def _split3(x):
    hi = x.astype(BF16)
    r = x - hi.astype(F32)
    mid = r.astype(BF16)
    lo = (r - mid.astype(F32)).astype(BF16)
    return hi, mid, lo


def _split2(x):
    hi = x.astype(BF16)
    return hi, (x - hi.astype(F32)).astype(BF16)


def _head_sums(xs, dh):
    w = xs[0].shape[-1]
    ones_bd = _same_head(w, dh).astype(BF16)
    out = _dot(jnp.concatenate([p for x in xs for p in _split3(x)], axis=0), ones_bd)
    res, off = [], 0
    for x in xs:
        r = x.shape[0]
        res.append(out[off:off + r] + (out[off + r:off + 2 * r] + out[off + 2 * r:off + 3 * r]))
        off += 3 * r
    return res


def _dot_sel(sel3, x):
    return _dot(sel3, jnp.concatenate(_split3(x), axis=0))


def _dot_x3(x, w_hi, w_lo):
    x_hi, x_lo = _split2(x)
    r = x.shape[0]
    t = _dot(jnp.concatenate([x_hi, x_lo], axis=0), w_hi)
    return t[0:r] + t[r:2 * r] + _dot(x_hi, w_lo)


def _sigmoid(x):
    return 1.0 / (1.0 + jnp.exp(-x))


def _ada_kernel(c_ref, w_ref, b_ref, o_ref):
    c = c_ref[...]
    sc = c * _sigmoid(c)
    o_ref[...] = _dot(sc, w_ref[...], HI) + b_ref[...]


def _ada_mod(cond, ada_w, ada_b):
    depth, d, d6 = ada_w.shape
    rows = cond.shape[0]
    tn = 1536
    return pl.pallas_call(
        _ada_kernel,
        grid=(depth, d6 // tn),
        in_specs=[pl.BlockSpec((rows, d), lambda l, j: (0, 0)),
                  pl.BlockSpec((None, d, tn), lambda l, j: (l, 0, j)),
                  pl.BlockSpec((None, 1, tn), lambda l, j: (l, 0, j))],
        out_specs=pl.BlockSpec((None, rows, tn), lambda l, j: (l, 0, j)),
        out_shape=jax.ShapeDtypeStruct((depth, rows, d6), F32),
        compiler_params=_cparams(("arbitrary", "arbitrary")),
        name="ada_mod",
    )(cond, ada_w, ada_b.reshape(depth, 1, d6))


def _rope_cols(t, cos, sin_signed, half):
    lane = _iota((1, t.shape[-1]), 1)
    first = _mod(lane, 2 * half) < half
    swapped = jnp.where(first, pltpu.roll(t, t.shape[-1] - half, 1), pltpu.roll(t, half, 1))
    return t * cos + swapped * sin_signed


def _in_proj_kernel(s_ref, mod_ref, g_ref, cos_ref, sin_ref, whg_ref, wda_ref, wrw_ref,
                    uhg_ref, uda_ref, urw_ref, *, da_width, q_scale):
    x = s_ref[...]
    ms = jnp.mean(x * x, axis=-1, keepdims=True)
    y = x * lax.rsqrt(ms + RMS_EPS) * g_ref[...]
    h = (y * (1.0 + mod_ref[1:2, :]) + mod_ref[0:1, :]).astype(BF16)
    uhg_ref[...] = _dot(h, whg_ref[...])
    urw_ref[...] = _dot(h, wrw_ref[...])
    uda = _dot(h, wda_ref[...])
    cos = cos_ref[...]
    sin = sin_ref[...]
    dqk = da_width // (2 * N_HEADS)
    for j in range(2 * da_width // LANE):
        t = _rope_cols(uda[:, j * LANE:(j + 1) * LANE], cos, sin, dqk // 2)
        if j < da_width // LANE:
            t = t * q_scale
        uda_ref[:, j * LANE:(j + 1) * LANE] = t.astype(BF16)
    uda_ref[:, 2 * da_width:] = uda[:, 2 * da_width:].astype(BF16)


def _in_proj(s, mods_l, g, cos_t, sin_t, whg, wda, wrw, n_ctx_tiles):
    b, n, d = s.shape
    tm = ROW_TILE
    da_width = wda.shape[1] // 3
    dqk = da_width // (2 * N_HEADS)
    sel = lambda bi, i: (jnp.where(i < n_ctx_tiles, b, bi), 0, 0)
    full = lambda a: pl.BlockSpec(a.shape, lambda bi, i: (0, 0))
    row = lambda w: pl.BlockSpec((None, tm, w), lambda bi, i: (bi, i, 0))
    kern = functools.partial(_in_proj_kernel, da_width=da_width, q_scale=dqk ** -0.5 * math.log2(math.e))
    return pl.pallas_call(
        kern,
        grid=(b, n // tm),
        in_specs=[row(d), pl.BlockSpec((None, 8, d), sel), full(g),
                  pl.BlockSpec((tm, LANE), lambda bi, i: (i, 0)),
                  pl.BlockSpec((tm, LANE), lambda bi, i: (i, 0)),
                  full(whg), full(wda), full(wrw)],
        out_specs=[row(whg.shape[1]), row(wda.shape[1]), row(wrw.shape[1])],
        out_shape=[jax.ShapeDtypeStruct((b, n, whg.shape[1]), F32),
                   jax.ShapeDtypeStruct((b, n, wda.shape[1]), BF16),
                   jax.ShapeDtypeStruct((b, n, wrw.shape[1]), F32)],
        compiler_params=_cparams(("parallel", "parallel")),
        name="in_proj",
    )(s, mods_l, g, cos_t, sin_t, whg, wda, wrw)


def _chunk_of_step(j, n_chunks, n_ctx_chunks, rev):
    if not rev:
        return j
    return jnp.where(j < n_ctx_chunks, n_ctx_chunks - 1 - j, n_chunks - 1 - (j - n_ctx_chunks))


def _hgrn_levels(c):
    return [1 << i for i in range(int(math.log2(c)))]


def _hgrn_const(c, rev):
    t = np.arange(c)[:, None]
    s = np.arange(c)[None, :]
    tri = (s >= t) if rev else (s <= t)
    mats = [tri.astype(np.float32)]
    for m in _hgrn_levels(c):
        p = (t // (2 * m)) * (2 * m) + (m if rev else m - 1)
        ref = (s >= p) if rev else (s <= p)
        mats.append(tri.astype(np.float32) - ref.astype(np.float32))
    return np.concatenate(mats, axis=0)


def _hgrn_kernel(*refs, rev, finish, dh):
    if finish:
        const_ref, q_ref, i_ref, f_ref, lb_ref, g_ref, other_ref, ng_ref, out_ref, st_ref = refs
    else:
        const_ref, q_ref, i_ref, f_ref, lb_ref, out_ref, st_ref = refs
    nb, c, w = q_ref.shape

    @pl.when(pl.program_id(1) == 0)
    def _():
        st_ref[...] = jnp.zeros_like(st_ref)

    nh = w // dh
    t_idx = _iota((c, 1), 0)
    s_idx = _mod(_iota((1, nh * c), 1), c)
    bd = _same_head(w, dh)
    lb = lb_ref[...]
    rows = lambda ref: ref[...].reshape(nb * c, w)
    q_all = rows(q_ref)
    v_all = rows(i_ref)
    f_all = lb + (1.0 - lb) * _sigmoid(rows(f_ref))
    kx_all = 1.0 - f_all
    logf_all = jnp.log(f_all)
    (diag_all,) = _head_sums([q_all * kx_all], dh)
    bf = lambda t: t.astype(BF16)
    cs = []
    for bb in range(nb):
        sl = slice(bb * c, (bb + 1) * c)
        seg = _dot_sel(const_ref[...], logf_all[sl])
        bcum = seg[0:c]
        cs.append(dict(bb=bb, q=q_all[sl], v=v_all[sl], kx=kx_all[sl], diag=diag_all[sl], seg=seg, bcum=bcum,
                       bend=bcum[0:1] if rev else bcum[c - 1:c], att=jnp.zeros((c, nh * c), F32)))
    for li, m in enumerate(_hgrn_levels(c)):
        t_up = (t_idx & m) != 0
        s_up = (s_idx & m) != 0
        t_isq = jnp.logical_not(t_up) if rev else t_up
        s_isk = s_up if rev else jnp.logical_not(s_up)
        valid = (_div(t_idx, 2 * m) == _div(s_idx, 2 * m)) & t_isq & s_isk
        for s in cs:
            e = jnp.exp(-jnp.abs(s["seg"][(li + 1) * c:(li + 2) * c]))
            xm = bf(jnp.where(t_isq, s["q"], s["kx"]) * e)
            sc = _dot_nt(xm, _expand_heads(xm, dh))
            s["att"] = s["att"] + jnp.where(valid, sc, 0.0)
    outs = []
    for s in cs:
        q, v, kx, bcum, bend = s["q"], s["v"], s["kx"], s["bcum"], s["bend"]
        o = _dot(bf(s["att"]), _expand_heads(bf(v), dh))
        o = o + s["diag"] * v
        st = st_ref[s["bb"]]
        o = o + _dot_nt(q * jnp.exp(bcum), st, P_HG)
        upd = _dot_tn(v, kx * jnp.exp(bend - bcum), P_HG)
        st_ref[s["bb"]] = st * jnp.exp(bend) + jnp.where(bd, upd, 0.0)
        outs.append(o)
    o_all = jnp.concatenate(outs, axis=0)
    if finish:
        o_all = o_all + rows(other_ref)
        (ms,) = _head_sums([o_all * o_all], dh)
        g = rows(g_ref)
        o_all = o_all * lax.rsqrt(ms * (1.0 / dh) + RMS_EPS) * ng_ref[...] * (g * _sigmoid(g))
    out_ref[...] = o_all.reshape(nb, c, w)


def _hgrn_pass(u_hg, lb_row, n_ctx_chunks, rev, other=None, norm_g_row=None):
    b, n, _ = u_hg.shape
    c = CHUNK
    w = lb_row.shape[-1]
    dh = w // N_HEADS
    nc = n // c
    finish = other is not None
    const = jnp.asarray(np.tile(_hgrn_const(c, rev), (1, 3)), BF16)
    chunk = lambda j: _chunk_of_step(j, nc, n_ctx_chunks, rev)
    nb = math.gcd(b, REC_BATCH)
    col = lambda k: pl.BlockSpec((nb, c, w), lambda bi, j: (bi, chunk(j), k))
    vec = pl.BlockSpec((1, w), lambda bi, j: (0, 0))
    in_specs = [pl.BlockSpec(const.shape, lambda bi, j: (0, 0)), col(0), col(1), col(3 if rev else 2), vec]
    args = [const, u_hg, u_hg, u_hg, lb_row]
    if finish:
        in_specs += [col(4), col(0), vec]
        args += [u_hg, other, norm_g_row]
    return pl.pallas_call(
        functools.partial(_hgrn_kernel, rev=rev, finish=finish, dh=dh),
        grid=(b // nb, nc),
        in_specs=in_specs,
        out_specs=col(0),
        out_shape=jax.ShapeDtypeStruct((b, n, w), F32),
        scratch_shapes=[pltpu.VMEM((nb, w, w), F32)],
        compiler_params=_cparams(("parallel", "arbitrary")),
        name="hgrn_bwd" if rev else "hgrn_fwd",
    )(*args)


def _rwkv_kernel(*refs, rev, finish, dh, n_chunks, n_ctx_chunks):
    if finish:
        (tri_ref, z_ref, zp_ref, zn_ref, mup_ref, mun_ref, w0_ref, w2_ref, a0_ref, a2_ref, g2_ref,
         kk_ref, ka_ref, rk_ref, oy_ref, ob_ref, lnw_ref, lnb_ref, out_ref, st_ref) = refs
    else:
        (tri_ref, z_ref, zp_ref, zn_ref, mup_ref, mun_ref, w0_ref, w2_ref, a0_ref, a2_ref, g2_ref,
         kk_ref, ka_ref, rk_ref, y_ref, bonus_ref, st_ref) = refs
    nb, c, _ = z_ref.shape
    w = w0_ref.shape[-1]
    j = pl.program_id(1)
    chunk = _chunk_of_step(j, n_chunks, n_ctx_chunks, rev)

    @pl.when(j == 0)
    def _():
        st_ref[...] = jnp.zeros_like(st_ref)

    row = _iota((c, 1), 0)
    seg_first = (chunk == 0) | (chunk == n_ctx_chunks)
    seg_last = (chunk == n_ctx_chunks - 1) | (chunk == n_chunks - 1)
    nr = (w // dh) * c
    si = _mod(_iota((1, nr), 1), c)
    strict = (si > row) if rev else (si < row)
    incl = (si >= row) if rev else (si <= row)
    eye_cat = (si == row).astype(F32)
    blk = _same_head(nr, c)
    bd = _same_head(w, dh)
    eye_w = (_iota((w, w), 0) == _iota((w, w), 1)).astype(F32)

    def prelude(group):
        us = []
        for bb in group:
            z = z_ref[bb]
            prev_row = jnp.where(seg_first, 0.0, zp_ref[bb, 7:8, :])
            next_row = jnp.where(seg_last, 0.0, zn_ref[bb, 0:1, :])
            z_prev = jnp.where(row == 0, prev_row, pltpu.roll(z, 1, 0))
            z_next = jnp.where(row == c - 1, next_row, pltpu.roll(z, c - 1, 0))
            us.append(z + mup_ref[...] * (z_prev - z) + mun_ref[...] * (z_next - z))
        u = jnp.concatenate(us, axis=0)
        r, k, v = u[:, 0:w], u[:, w:2 * w], u[:, 2 * w:3 * w]
        lora = u[:, 3 * w:3 * w + LANE]
        gd = u[:, 3 * w + LANE:3 * w + 2 * LANE]
        w_log = -_softplus(-(w0_ref[...] + _dot_x3(jnp.tanh(lora), w2_ref[0], w2_ref[1]))) - 0.5
        lw = -jnp.exp(w_log)
        a_lr = _sigmoid(a0_ref[...] + _dot_x3(lora, a2_ref[0], a2_ref[1]))
        kk = k * kk_ref[...]
        kd = k * (1.0 + (a_lr - 1.0) * ka_ref[...])
        kk_sq, bonus_dot = _head_sums([kk * kk, r * kd * rk_ref[...]], dh)
        kk = kk * lax.rsqrt(jnp.maximum(kk_sq, 1e-24))
        cols = dict(r=r, v=v, kd=kd, lw=lw, a=-kk, b=kk * a_lr, bonus=bonus_dot * v, gd=gd)
        return [{name: t[i * c:(i + 1) * c] for name, t in cols.items()} for i in range(len(group))]

    chains = list(range(nb))
    pre = prelude(chains)

    bf = lambda t: t.astype(BF16)
    ex = lambda t: _expand_heads(bf(t), dh)
    st = []
    for d in pre:
        v, kd, lw = d["v"], d["kd"], d["lw"]
        cum = _dot_sel(tri_ref[...], lw)
        tot = cum[0:1] if rev else cum[c - 1:c]
        inv_gam = jnp.exp(-cum)
        rest = jnp.exp(tot - cum)
        r_t = d["r"] * jnp.exp(cum)
        a_t = d["a"] * jnp.exp(cum - lw)
        ea, eb, ek, ev = ex(a_t), ex(d["b"] * inv_gam), ex(kd * inv_gam), ex(v)
        sc = _dot_nt(jnp.concatenate([bf(a_t), bf(r_t)], axis=0), jnp.concatenate([eb, ek], axis=0))
        st.append(dict(
            v=v, r_t=r_t, tot=tot, ea=ea, ev=ev, b_h=d["b"] * rest, k_h=kd * rest,
            a_ab=jnp.where(strict, sc[0:c, 0:nr], 0.0), a_ak=jnp.where(strict, sc[0:c, nr:2 * nr], 0.0),
            a_rb=jnp.where(incl, sc[c:2 * c, 0:nr], 0.0), a_rk=jnp.where(incl, sc[c:2 * c, nr:2 * nr], 0.0)))

    bd_cat = lambda t: jnp.where(blk, jnp.concatenate([bf(t)] * (nr // c), axis=0), 0.0)
    for s in st:
        s["pw"] = _dot(bf(s["a_ab"]), bd_cat(s["a_ab"]))
        s["t_cat"] = eye_cat + s["a_ab"]
    n_iter = int(math.log2(c)) - 1
    for it in range(n_iter):
        for s in st:
            pw = bf(s["pw"])
            if it + 1 < n_iter:
                both = _dot(pw, jnp.concatenate([bd_cat(s["t_cat"]), bd_cat(s["pw"])], axis=1))
                s["t_cat"], s["pw"] = s["t_cat"] + both[:, 0:nr], both[:, nr:2 * nr]
            else:
                s["t_cat"] = s["t_cat"] + _dot(pw, bd_cat(s["t_cat"]))

    for s in st:
        s["t_cat"] = bf(s["t_cat"])
        s["x_c"] = _dot(bf(s["a_ak"]), s["ev"])
    for s in st:
        s["w1c"] = _dot(s["t_cat"], s["ea"])
        s["w2c"] = _dot(s["t_cat"], ex(s["x_c"]))
    for s in st:
        a_rb = bf(s["a_rb"])
        s["p_mat"] = s["r_t"] + _dot(a_rb, ex(s["w1c"]))
        s["y0"] = _dot(jnp.concatenate([a_rb, bf(s["a_rk"])], axis=1),
                       jnp.concatenate([ex(s["w2c"]), s["ev"]], axis=0))
    ys = []
    for bb, s in zip(chains, st):
        lhs = jnp.concatenate([s["w1c"], s["p_mat"], eye_w * jnp.exp(s["tot"])], axis=0)
        big = _dot(lhs, st_ref[bb], P_ST)
        ys.append(big[c:2 * c] + s["y0"])
        upd = _dot_tn(jnp.concatenate([s["b_h"], s["k_h"]], axis=0),
                      jnp.concatenate([big[0:c] + s["w2c"], s["v"]], axis=0), P_ST)
        st_ref[bb] = big[2 * c:] + jnp.where(bd, upd, 0.0)

    y = jnp.concatenate(ys, axis=0)
    bonus_all = jnp.concatenate([d["bonus"] for d in pre], axis=0)
    if finish:
        gd = jnp.concatenate([d["gd"] for d in pre], axis=0)
        y = y + oy_ref[...].reshape(nb * c, w)
        (mean,) = _head_sums([y], dh)
        yc = y - mean * (1.0 / dh)
        (var,) = _head_sums([yc * yc], dh)
        yn = yc * lax.rsqrt(var * (1.0 / dh) + RW_GN_EPS) * lnw_ref[...] + lnb_ref[...]
        gate = _dot_x3(_sigmoid(gd), g2_ref[0], g2_ref[1])
        out = (yn + bonus_all + ob_ref[...].reshape(nb * c, w)) * gate
        out_ref[...] = out.reshape(nb, c, w)
    else:
        y_ref[...] = y.reshape(nb, c, w)
        bonus_ref[...] = bonus_all.reshape(nb, c, w)


def _rwkv_pass(u_rw, p, n_ctx_chunks, rev, other=None):
    b, n, wu = u_rw.shape
    c = CHUNK
    w = p["w0"].shape[-1]
    dh = w // N_HEADS
    nc = n // c
    finish = other is not None
    t = np.arange(c)
    tri = (t[None, :] >= t[:, None]) if rev else (t[None, :] <= t[:, None])
    tri = jnp.asarray(np.tile(tri.astype(np.float32), (1, 3)), BF16)
    hi_lo = lambda m: jnp.stack(_split2(m))
    d = 1 if rev else 0
    chunk = lambda j: _chunk_of_step(j, nc, n_ctx_chunks, rev)
    sub = c // 8
    nb = math.gcd(b, REC_BATCH)
    const = lambda a: pl.BlockSpec(a.shape, lambda bi, j: (0,) * a.ndim)
    col = pl.BlockSpec((nb, c, w), lambda bi, j: (bi, chunk(j), 0))
    args = [tri, u_rw, u_rw, u_rw, p["mu_prev"], p["mu_next"], p["w0"][d], hi_lo(p["w2"][d]), p["a0"][d],
            hi_lo(p["a2"][d]), hi_lo(p["g2"]), p["k_k"], p["k_a"], p["r_k"]]
    in_specs = [const(tri),
                pl.BlockSpec((nb, c, wu), lambda bi, j: (bi, chunk(j), 0)),
                pl.BlockSpec((nb, 8, wu), lambda bi, j: (bi, jnp.maximum(chunk(j) * sub - 1, 0), 0)),
                pl.BlockSpec((nb, 8, wu), lambda bi, j: (bi, jnp.minimum((chunk(j) + 1) * sub, n // 8 - 1), 0))]
    in_specs += [const(a) for a in args[4:]]
    if finish:
        args += [other[0], other[1], p["ln_w"], p["ln_b"]]
        in_specs += [col, col, const(p["ln_w"]), const(p["ln_b"])]
        out_specs, out_shape = col, jax.ShapeDtypeStruct((b, n, w), F32)
    else:
        out_specs = [col, col]
        out_shape = [jax.ShapeDtypeStruct((b, n, w), F32)] * 2
    return pl.pallas_call(
        functools.partial(_rwkv_kernel, rev=rev, finish=finish, dh=dh, n_chunks=nc, n_ctx_chunks=n_ctx_chunks),
        grid=(b // nb, nc),
        in_specs=in_specs,
        out_specs=out_specs,
        out_shape=out_shape,
        scratch_shapes=[pltpu.VMEM((nb, w, w), F32)],
        compiler_params=_cparams(("parallel", "arbitrary")),
        name="rwkv_bwd" if rev else "rwkv_fwd",
    )(*args)


def _attn_lam(lam_ref, lam_init):
    lp = lam_ref[...]
    return (jnp.exp(jnp.sum(lp[0:1] * lp[1:2], axis=-1, keepdims=True))
            - jnp.exp(jnp.sum(lp[2:3] * lp[3:4], axis=-1, keepdims=True)) + lam_init)


def _stack_maps(q):
    half = q.shape[-1] // 2
    lane = _iota((1, q.shape[-1]), 1)
    return jnp.concatenate([jnp.where(lane < half, q, 0), jnp.where(lane >= half, q, 0)], axis=0)


def _attn_finish(o, ng, lam_init):
    ms = jnp.mean(o * o, axis=-1, keepdims=True)
    return o * lax.rsqrt(ms + RMS_EPS) * ng * (1.0 - lam_init)


def _attn_kernel(lam_ref, q_ref, k_ref, v_ref, ng_ref, o_ref, *, lam_init, n_ctx, n_ctx_tiles):
    lam = _attn_lam(lam_ref, lam_init)

    def attend(k, v):
        ts = ATT_SUB
        subs = range(q_ref.shape[0] // ts)
        s = [_dot_nt(_stack_maps(q_ref[i * ts:(i + 1) * ts, :]), k) for i in subs]
        for i, x in zip(subs, s):
            p = jnp.exp2(x - jnp.max(x, axis=-1, keepdims=True))
            den = jnp.sum(p, axis=-1, keepdims=True)
            pc = p[0:ts] - p[ts:2 * ts] * (lam * den[0:ts] / den[ts:2 * ts])
            o_ref[i * ts:(i + 1) * ts, :] = _attn_finish(_dot(pc.astype(BF16), v) / den[0:ts], ng_ref[...],
                                                         lam_init)

    is_ctx = pl.program_id(2) < n_ctx_tiles

    @pl.when(is_ctx)
    def _():
        attend(k_ref[0:n_ctx, :], v_ref[0:n_ctx, :])

    @pl.when(jnp.logical_not(is_ctx))
    def _():
        attend(k_ref[...], v_ref[...])


def _attention(u_da, lam_rows, norm_g_row, lam_init, n_ctx):
    b, n, w3 = u_da.shape
    wd = w3 // 3
    dv = wd // N_HEADS
    tq = ROW_TILE
    kern = functools.partial(_attn_kernel, lam_init=lam_init, n_ctx=n_ctx, n_ctx_tiles=n_ctx // tq)
    return pl.pallas_call(
        kern,
        grid=(b, N_HEADS, n // tq),
        in_specs=[pl.BlockSpec(lam_rows.shape, lambda bi, h, i: (0, 0)),
                  pl.BlockSpec((None, tq, dv), lambda bi, h, i: (bi, i, h)),
                  pl.BlockSpec((None, n, dv), lambda bi, h, i: (bi, 0, N_HEADS + h)),
                  pl.BlockSpec((None, n, dv), lambda bi, h, i: (bi, 0, 2 * N_HEADS + h)),
                  pl.BlockSpec((1, dv), lambda bi, h, i: (0, 0))],
        out_specs=pl.BlockSpec((None, tq, dv), lambda bi, h, i: (bi, i, h)),
        out_shape=jax.ShapeDtypeStruct((b, n, wd), F32),
        compiler_params=_cparams(("parallel", "parallel", "arbitrary")),
        name="diff_attn",
    )(lam_rows, u_da, u_da, u_da, norm_g_row)


def _out_proj_kernel(s_ref, hg_ref, da_ref, rw_ref, mod_ref, g_ref, whg_ref, wda_ref, wrw_ref,
                     x_ref, h_ref):
    mix = (_dot(hg_ref[...].astype(BF16), whg_ref[...]) + _dot(da_ref[...].astype(BF16), wda_ref[...])
           + _dot(rw_ref[...].astype(BF16), wrw_ref[...]))
    x = s_ref[...] + mod_ref[2:3, :] * mix
    x_ref[...] = x
    ms = jnp.mean(x * x, axis=-1, keepdims=True)
    y = x * lax.rsqrt(ms + RMS_EPS) * g_ref[...]
    h_ref[...] = (y * (1.0 + mod_ref[4:5, :]) + mod_ref[3:4, :]).astype(BF16)


def _out_proj(s, o_hg, o_da, o_rw, mods_l, g, whg, wda, wrw, n_ctx_tiles):
    b, n, d = s.shape
    tm = ROW_TILE
    sel = lambda bi, i: (jnp.where(i < n_ctx_tiles, b, bi), 0, 0)
    full = lambda a: pl.BlockSpec(a.shape, lambda bi, i: (0, 0))
    row = lambda w: pl.BlockSpec((None, tm, w), lambda bi, i: (bi, i, 0))
    return pl.pallas_call(
        _out_proj_kernel,
        grid=(b, n // tm),
        in_specs=[row(d), row(o_hg.shape[-1]), row(o_da.shape[-1]), row(o_rw.shape[-1]),
                  pl.BlockSpec((None, 8, d), sel), full(g), full(whg), full(wda), full(wrw)],
        out_specs=[row(d), row(d)],
        out_shape=[jax.ShapeDtypeStruct((b, n, d), F32), jax.ShapeDtypeStruct((b, n, d), BF16)],
        compiler_params=_cparams(("parallel", "parallel")),
        name="out_proj",
    )(s, o_hg, o_da, o_rw, mods_l, g, whg, wda, wrw)


def _ffn_kernel(*refs, final):
    if final:
        x_ref, h_ref, mod_ref, wg_ref, wu_ref, wd_ref, fg_ref, o_ref = refs
    else:
        x_ref, h_ref, mod_ref, wg_ref, wu_ref, wd_ref, o_ref = refs
    h = h_ref[...]
    gate = _dot(h, wg_ref[...])
    up = _dot(h, wu_ref[...])
    act = (gate * _sigmoid(gate) * up).astype(BF16)
    x = x_ref[...] + mod_ref[5:6, :] * _dot(act, wd_ref[...])
    if final:
        ms = jnp.mean(x * x, axis=-1, keepdims=True)
        x = x * lax.rsqrt(ms + RMS_EPS) * fg_ref[...]
    o_ref[...] = x


def _ffn(x, h, mods_l, wg, wu, wd, n_ctx_tiles, final_g=None):
    b, n, d = x.shape
    tm = ROW_TILE
    final = final_g is not None
    off = n_ctx_tiles if final else 0
    sel = lambda bi, i: (jnp.where(i + off < n_ctx_tiles, b, bi), 0, 0)
    resident = lambda a: pl.BlockSpec(a.shape, lambda bi, i: (0, 0), pipeline_mode=pl.Buffered(1))
    row = pl.BlockSpec((None, tm, d), lambda bi, i: (bi, i + off, 0))
    in_specs = [row, row, pl.BlockSpec((None, 8, d), sel), resident(wg), resident(wu), resident(wd)]
    args = [x, h, mods_l, wg, wu, wd]
    if final:
        in_specs.append(pl.BlockSpec((1, d), lambda bi, i: (0, 0)))
        args.append(final_g)
    return pl.pallas_call(
        functools.partial(_ffn_kernel, final=final),
        grid=(b, n // tm - off),
        in_specs=in_specs,
        out_specs=pl.BlockSpec((None, tm, d), lambda bi, i: (bi, i, 0)),
        out_shape=jax.ShapeDtypeStruct((b, n - off * tm, d), F32),
        compiler_params=_cparams(("parallel", "parallel")),
        name="ffn",
    )(*args)


def _rope_tables(n_ctx, n_lat, dqk):
    n_freq = dqk // 4
    t = np.arange(n_lat)
    inv_freq = ROPE_BASE ** (-np.arange(n_freq, dtype=np.float32) / n_freq)
    ang = np.concatenate([(t // GRID_W)[:, None] * inv_freq, (t % GRID_W)[:, None] * inv_freq], axis=-1)
    reps = LANE // dqk
    cos = np.tile(np.concatenate([np.cos(ang), np.cos(ang)], axis=-1), (1, reps))
    sin = np.tile(np.concatenate([-np.sin(ang), np.sin(ang)], axis=-1), (1, reps))
    cos = np.concatenate([np.ones((n_ctx, LANE)), cos], axis=0)
    sin = np.concatenate([np.zeros((n_ctx, LANE)), sin], axis=0)
    return jnp.asarray(cos, F32), jnp.asarray(sin, F32)


def _pad_rows(a, start, total):
    return jnp.zeros((total, a.shape[-1]), a.dtype).at[start:start + a.shape[0]].set(a)


def kernel(x, c, ctx, c_ctx, ada_w, ada_b, norm1_g, norm2_g, w_in, w_out, hg_lb_logits, hg_norm_g, da_lam_q1, da_lam_k1, da_lam_q2, da_lam_k2, da_norm_g, rw_mu_prev, rw_mu_next, rw_w0, rw_w2, rw_a0, rw_a2, rw_g2, rw_k_k, rw_k_a, rw_r_k, rw_ln_w, rw_ln_b, ffn_w_gate, ffn_w_up, ffn_w_down, final_norm_g):
    b, n_lat, d = x.shape
    n_ctx = ctx.shape[1]
    depth = ada_w.shape[0]
    hg_w = hg_lb_logits.shape[-1]
    rw_w = rw_w0.shape[-1]
    da_w = w_out.shape[1] - hg_w - rw_w
    hg_cols, da_cols = 5 * hg_w, 3 * da_w
    rw_cols = w_in.shape[-1] - hg_cols - da_cols
    rw_pad = 3 * rw_w + 2 * LANE
    r_dec, r_icl, r_gate = rw_w2.shape[2], rw_a2.shape[2], rw_g2.shape[1]
    assert n_ctx % ROW_TILE == 0 and n_lat % ROW_TILE == 0 and n_lat % GRID_W == 0
    assert 2 * (r_dec + r_icl) == LANE and r_gate <= LANE and rw_cols == 3 * rw_w + LANE + r_gate
    n_ctx_tiles, n_ctx_chunks = n_ctx // ROW_TILE, n_ctx // CHUNK

    cond = jnp.zeros((16, d), F32).at[:b].set(c).at[b].set(c_ctx)
    mods = _ada_mod(cond, ada_w, ada_b).reshape(depth, 16, 6, d)
    mods = jnp.pad(mods, ((0, 0), (0, 0), (0, 2), (0, 0)))

    p_lb = jax.nn.softmax(hg_lb_logits.astype(F32), axis=1)
    lower_bounds = jnp.cumsum(p_lb, axis=1) - p_lb[:, :1]
    cos_t, sin_t = _rope_tables(n_ctx, n_lat, da_w // (2 * N_HEADS))

    s = jnp.concatenate([ctx, x], axis=1)
    for l in range(depth):
        w_l = w_in[l]
        whg = w_l[:, :hg_cols].astype(BF16)
        wda = w_l[:, hg_cols:hg_cols + da_cols].astype(BF16)
        wrw = jnp.pad(w_l[:, hg_cols + da_cols:], ((0, 0), (0, rw_pad - rw_cols))).astype(BF16)
        u_hg, u_da, u_rw = _in_proj(s, mods[l], norm1_g[l][None], cos_t, sin_t, whg, wda, wrw, n_ctx_tiles)

        o_b = _hgrn_pass(u_hg, lower_bounds[1, l][None], n_ctx_chunks, True)
        o_hg = _hgrn_pass(u_hg, lower_bounds[0, l][None], n_ctx_chunks, False, other=o_b,
                          norm_g_row=jnp.tile(hg_norm_g[l], N_HEADS)[None])

        lam_init = 0.8 - 0.6 * math.exp(-0.3 * l)
        lam_rows = jnp.zeros((8, LANE), F32).at[0:4, :da_lam_q1.shape[-1]].set(
            jnp.stack([da_lam_q1[l], da_lam_k1[l], da_lam_q2[l], da_lam_k2[l]]))
        o_da = _attention(u_da, lam_rows, da_norm_g[l][None], lam_init, n_ctx)

        mu_pad = lambda m: jnp.pad(m, (0, rw_pad - rw_cols))[None]
        off_dec, off_icl = 0, 2 * r_dec
        rw_p = {
            "mu_prev": mu_pad(rw_mu_prev[l]), "mu_next": mu_pad(rw_mu_next[l]),
            "w0": rw_w0[:, l][:, None, :], "a0": rw_a0[:, l][:, None, :],
            "w2": jnp.stack([_pad_rows(rw_w2[i, l], off_dec + i * r_dec, LANE) for i in range(2)]),
            "a2": jnp.stack([_pad_rows(rw_a2[i, l], off_icl + i * r_icl, LANE) for i in range(2)]),
            "g2": _pad_rows(rw_g2[l], 0, LANE),
            "k_k": rw_k_k[l][None], "k_a": rw_k_a[l][None], "r_k": rw_r_k[l][None],
            "ln_w": rw_ln_w[l][None], "ln_b": rw_ln_b[l][None],
        }
        yb = _rwkv_pass(u_rw, rw_p, n_ctx_chunks, True)
        o_rw = _rwkv_pass(u_rw, rw_p, n_ctx_chunks, False, other=yb)

        wo = w_out[l].astype(BF16)
        x_mid, h2 = _out_proj(s, o_hg, o_da, o_rw, mods[l], norm2_g[l][None],
                              wo[:hg_w], wo[hg_w:hg_w + da_w], wo[hg_w + da_w:], n_ctx_tiles)
        s = _ffn(x_mid, h2, mods[l], ffn_w_gate[l].astype(BF16), ffn_w_up[l].astype(BF16),
                 ffn_w_down[l].astype(BF16), n_ctx_tiles,
                 final_g=final_norm_g[None] if l == depth - 1 else None)
    return s
```

```python
import functools
import math

import numpy as np
import jax
import jax.numpy as jnp
from jax import lax
from jax.experimental import pallas as pl
from jax.experimental.pallas import tpu as pltpu

F32 = jnp.float32
BF16 = jnp.bfloat16
HI = lax.Precision.HIGHEST

N_HEADS = 4
GRID_W = 64
ROPE_BASE = 10000.0
RMS_EPS = 1e-6
RW_GN_EPS = 64e-5

ROW_TILE = 256
ATT_SUB = 128
CHUNK = 64
REC_BATCH = 8
LANE = 128
VMEM_LIMIT = 56 * 1024 * 1024


def _cparams(sem):
    return pltpu.CompilerParams(dimension_semantics=sem, vmem_limit_bytes=VMEM_LIMIT)


def _dot_dims(a, b, dims, prec):
    dg = lambda x, y, p=None: lax.dot_general(x, y, (dims, ((), ())), preferred_element_type=F32, precision=p)
    if prec == "bf16":
        return dg(a.astype(BF16), b.astype(BF16))
    if prec == "x3":
        a_hi, b_hi = a.astype(BF16), b.astype(BF16)
        a_lo = (a - a_hi.astype(F32)).astype(BF16)
        b_lo = (b - b_hi.astype(F32)).astype(BF16)
        return dg(a_hi, b_hi) + (dg(a_lo, b_hi) + dg(a_hi, b_lo))
    return dg(a, b, prec)


def _dot(a, b, prec=None):
    return _dot_dims(a, b, ((1,), (0,)), prec)


def _dot_nt(a, b, prec=None):
    return _dot_dims(a, b, ((1,), (1,)), prec)


def _dot_tn(a, b, prec=None):
    return _dot_dims(a, b, ((0,), (0,)), prec)


P_HG = "bf16"
P_ST = "bf16"


def _iota(shape, dim):
    return lax.broadcasted_iota(jnp.int32, shape, dim)


def _div(x, d):
    assert d & (d - 1) == 0
    return x >> (d.bit_length() - 1)


def _mod(x, d):
    assert d & (d - 1) == 0
    return x & (d - 1)


def _softplus(x):
    return jnp.maximum(x, 0.0) + jnp.log(1.0 + jnp.exp(-jnp.abs(x)))


def _same_head(w, dh):
    return _div(_iota((w, w), 0), dh) == _div(_iota((w, w), 1), dh)


def _expand_heads(x, dh):
    w = x.shape[-1]
    head = _div(_iota((1, w), 1), dh)
    return jnp.concatenate([jnp.where(head == h, x, 0.0) for h in range(w // dh)], axis=0)


def _collapse_heads(x, c):
    out = x[0:c]
    for h in range(1, x.shape[0] // c):
        out = out + x[h * c:(h + 1) * c]
    return out


def _split3(x):
    hi = x.astype(BF16)
    r = x - hi.astype(F32)
    mid = r.astype(BF16)
    lo = (r - mid.astype(F32)).astype(BF16)
    return hi, mid, lo


def _split2(x):
    hi = x.astype(BF16)
    return hi, (x - hi.astype(F32)).astype(BF16)


def _head_sums(xs, dh):
    w = xs[0].shape[-1]
    ones_bd = _same_head(w, dh).astype(BF16)
    out = _dot(jnp.concatenate([p for x in xs for p in _split3(x)], axis=0), ones_bd)
    res, off = [], 0
    for x in xs:
        r = x.shape[0]
        res.append(out[off:off + r] + (out[off + r:off + 2 * r] + out[off + 2 * r:off + 3 * r]))
        off += 3 * r
    return res


def _dot_sel(sel3, x):
    return _dot(sel3, jnp.concatenate(_split3(x), axis=0))


def _dot_x3(x, w_hi, w_lo):
    x_hi, x_lo = _split2(x)
    r = x.shape[0]
    t = _dot(jnp.concatenate([x_hi, x_lo], axis=0), w_hi)
    return t[0:r] + t[r:2 * r] + _dot(x_hi, w_lo)


def _sigmoid(x):
    return 1.0 / (1.0 + jnp.exp(-x))


def _ada_kernel(c_ref, w_ref, b_ref, o_ref):
    c = c_ref[...]
    sc = c * _sigmoid(c)
    o_ref[...] = _dot(sc, w_ref[...], HI) + b_ref[...]


def _ada_mod(cond, ada_w, ada_b):
    depth, d, d6 = ada_w.shape
    rows = cond.shape[0]
    tn = 1536
    return pl.pallas_call(
        _ada_kernel,
        grid=(depth, d6 // tn),
        in_specs=[pl.BlockSpec((rows, d), lambda l, j: (0, 0)),
                  pl.BlockSpec((None, d, tn), lambda l, j: (l, 0, j)),
                  pl.BlockSpec((None, 1, tn), lambda l, j: (l, 0, j))],
        out_specs=pl.BlockSpec((None, rows, tn), lambda l, j: (l, 0, j)),
        out_shape=jax.ShapeDtypeStruct((depth, rows, d6), F32),
        compiler_params=_cparams(("arbitrary", "arbitrary")),
        name="ada_mod",
    )(cond, ada_w, ada_b.reshape(depth, 1, d6))


def _rope_cols(t, cos, sin_signed, half):
    lane = _iota((1, t.shape[-1]), 1)
    first = _mod(lane, 2 * half) < half
    swapped = jnp.where(first, pltpu.roll(t, t.shape[-1] - half, 1), pltpu.roll(t, half, 1))
    return t * cos + swapped * sin_signed


def _in_proj_kernel(s_ref, mod_ref, g_ref, cos_ref, sin_ref, whg_ref, wda_ref, wrw_ref,
                    uhg_ref, uda_ref, urw_ref, *, da_width, q_scale):
    x = s_ref[...]
    ms = jnp.mean(x * x, axis=-1, keepdims=True)
    y = x * lax.rsqrt(ms + RMS_EPS) * g_ref[...]
    h = (y * (1.0 + mod_ref[1:2, :]) + mod_ref[0:1, :]).astype(BF16)
    uhg_ref[...] = _dot(h, whg_ref[...])
    urw_ref[...] = _dot(h, wrw_ref[...])
    uda = _dot(h, wda_ref[...])
    cos = cos_ref[...]
    sin = sin_ref[...]
    dqk = da_width // (2 * N_HEADS)
    for j in range(2 * da_width // LANE):
        t = _rope_cols(uda[:, j * LANE:(j + 1) * LANE], cos, sin, dqk // 2)
        if j < da_width // LANE:
            t = t * q_scale
        uda_ref[:, j * LANE:(j + 1) * LANE] = t.astype(BF16)
    uda_ref[:, 2 * da_width:] = uda[:, 2 * da_width:].astype(BF16)


def _in_proj(s, mods_l, g, cos_t, sin_t, whg, wda, wrw, n_ctx_tiles):
    b, n, d = s.shape
    tm = ROW_TILE
    da_width = wda.shape[1] // 3
    dqk = da_width // (2 * N_HEADS)
    sel = lambda bi, i: (jnp.where(i < n_ctx_tiles, b, bi), 0, 0)
    full = lambda a: pl.BlockSpec(a.shape, lambda bi, i: (0, 0))
    row = lambda w: pl.BlockSpec((None, tm, w), lambda bi, i: (bi, i, 0))
    kern = functools.partial(_in_proj_kernel, da_width=da_width, q_scale=dqk ** -0.5 * math.log2(math.e))
    return pl.pallas_call(
        kern,
        grid=(b, n // tm),
        in_specs=[row(d), pl.BlockSpec((None, 8, d), sel), full(g),
                  pl.BlockSpec((tm, LANE), lambda bi, i: (i, 0)),
                  pl.BlockSpec((tm, LANE), lambda bi, i: (i, 0)),
                  full(whg), full(wda), full(wrw)],
        out_specs=[row(whg.shape[1]), row(wda.shape[1]), row(wrw.shape[1])],
        out_shape=[jax.ShapeDtypeStruct((b, n, whg.shape[1]), F32),
                   jax.ShapeDtypeStruct((b, n, wda.shape[1]), BF16),
                   jax.ShapeDtypeStruct((b, n, wrw.shape[1]), F32)],
        compiler_params=_cparams(("parallel", "parallel")),
        name="in_proj",
    )(s, mods_l, g, cos_t, sin_t, whg, wda, wrw)


def _chunk_of_step(j, n_chunks, n_ctx_chunks, rev):
    if not rev:
        return j
    return jnp.where(j < n_ctx_chunks, n_ctx_chunks - 1 - j, n_chunks - 1 - (j - n_ctx_chunks))


def _hgrn_levels(c):
    return [1 << i for i in range(int(math.log2(c)))]


def _hgrn_const(c, rev):
    t = np.arange(c)[:, None]
    s = np.arange(c)[None, :]
    tri = (s >= t) if rev else (s <= t)
    mats = [tri.astype(np.float32)]
    for m in _hgrn_levels(c):
        p = (t // (2 * m)) * (2 * m) + (m if rev else m - 1)
        ref = (s >= p) if rev else (s <= p)
        mats.append(tri.astype(np.float32) - ref.astype(np.float32))
    return np.concatenate(mats, axis=0)


def _hgrn_kernel(*refs, rev, finish, dh):
    if finish:
        const_ref, q_ref, i_ref, f_ref, lb_ref, g_ref, other_ref, ng_ref, out_ref, st_ref = refs
    else:
        const_ref, q_ref, i_ref, f_ref, lb_ref, out_ref, st_ref = refs
    nb, c, w = q_ref.shape

    @pl.when(pl.program_id(1) == 0)
    def _():
        st_ref[...] = jnp.zeros_like(st_ref)

    nh = w // dh
    t_idx = _iota((c, 1), 0)
    s_idx = _mod(_iota((1, nh * c), 1), c)
    bd = _same_head(w, dh)
    lb = lb_ref[...]
    rows = lambda ref: ref[...].reshape(nb * c, w)
    q_all = rows(q_ref)
    v_all = rows(i_ref)
    f_all = lb + (1.0 - lb) * _sigmoid(rows(f_ref))
    kx_all = 1.0 - f_all
    logf_all = jnp.log(f_all)
    (diag_all,) = _head_sums([q_all * kx_all], dh)
    bf = lambda t: t.astype(BF16)
    cs = []
    for bb in range(nb):
        sl = slice(bb * c, (bb + 1) * c)
        seg = _dot_sel(const_ref[...], logf_all[sl])
        bcum = seg[0:c]
        cs.append(dict(bb=bb, q=q_all[sl], v=v_all[sl], kx=kx_all[sl], diag=diag_all[sl], seg=seg, bcum=bcum,
                       bend=bcum[0:1] if rev else bcum[c - 1:c], att=jnp.zeros((c, nh * c), F32)))
    for li, m in enumerate(_hgrn_levels(c)):
        t_up = (t_idx & m) != 0
        s_up = (s_idx & m) != 0
        t_isq = jnp.logical_not(t_up) if rev else t_up
        s_isk = s_up if rev else jnp.logical_not(s_up)
        valid = (_div(t_idx, 2 * m) == _div(s_idx, 2 * m)) & t_isq & s_isk
        for s in cs:
            e = jnp.exp(-jnp.abs(s["seg"][(li + 1) * c:(li + 2) * c]))
            xm = bf(jnp.where(t_isq, s["q"], s["kx"]) * e)
            sc = _dot_nt(xm, _expand_heads(xm, dh))
            s["att"] = s["att"] + jnp.where(valid, sc, 0.0)
    outs = []
    for s in cs:
        q, v, kx, bcum, bend = s["q"], s["v"], s["kx"], s["bcum"], s["bend"]
        o = _dot(bf(s["att"]), _expand_heads(bf(v), dh))
        o = o + s["diag"] * v
        st = st_ref[s["bb"]]
        o = o + _dot_nt(q * jnp.exp(bcum), st, P_HG)
        upd = _dot_tn(v, kx * jnp.exp(bend - bcum), P_HG)
        st_ref[s["bb"]] = st * jnp.exp(bend) + jnp.where(bd, upd, 0.0)
        outs.append(o)
    o_all = jnp.concatenate(outs, axis=0)
    if finish:
        o_all = o_all + rows(other_ref)
        (ms,) = _head_sums([o_all * o_all], dh)
        g = rows(g_ref)
        o_all = o_all * lax.rsqrt(ms * (1.0 / dh) + RMS_EPS) * ng_ref[...] * (g * _sigmoid(g))
    out_ref[...] = o_all.reshape(nb, c, w).astype(out_ref.dtype)


def _hgrn_pass(u_hg, lb_row, n_ctx_chunks, rev, other=None, norm_g_row=None):
    b, n, _ = u_hg.shape
    c = CHUNK
    w = lb_row.shape[-1]
    dh = w // N_HEADS
    nc = n // c
    finish = other is not None
    const = jnp.asarray(np.tile(_hgrn_const(c, rev), (1, 3)), BF16)
    chunk = lambda j: _chunk_of_step(j, nc, n_ctx_chunks, rev)
    nb = math.gcd(b, REC_BATCH)
    col = lambda k: pl.BlockSpec((nb, c, w), lambda bi, j: (bi, chunk(j), k))
    vec = pl.BlockSpec((1, w), lambda bi, j: (0, 0))
    in_specs = [pl.BlockSpec(const.shape, lambda bi, j: (0, 0)), col(0), col(1), col(3 if rev else 2), vec]
    args = [const, u_hg, u_hg, u_hg, lb_row]
    if finish:
        in_specs += [col(4), col(0), vec]
        args += [u_hg, other, norm_g_row]
    return pl.pallas_call(
        functools.partial(_hgrn_kernel, rev=rev, finish=finish, dh=dh),
        grid=(b // nb, nc),
        in_specs=in_specs,
        out_specs=col(0),
        out_shape=jax.ShapeDtypeStruct((b, n, w), BF16 if finish else F32),
        scratch_shapes=[pltpu.VMEM((nb, w, w), F32)],
        compiler_params=_cparams(("parallel", "arbitrary")),
        name="hgrn_bwd" if rev else "hgrn_fwd",
    )(*args)


def _rwkv_kernel(*refs, rev, finish, dh, n_chunks, n_ctx_chunks):
    if finish:
        (tri_ref, z_ref, zp_ref, zn_ref, mup_ref, mun_ref, w0_ref, w2_ref, a0_ref, a2_ref, g2_ref,
         kk_ref, ka_ref, rk_ref, oy_ref, ob_ref, lnw_ref, lnb_ref, out_ref, st_ref) = refs
    else:
        (tri_ref, z_ref, zp_ref, zn_ref, mup_ref, mun_ref, w0_ref, w2_ref, a0_ref, a2_ref, g2_ref,
         kk_ref, ka_ref, rk_ref, y_ref, bonus_ref, st_ref) = refs
    nb, c, _ = z_ref.shape
    w = w0_ref.shape[-1]
    j = pl.program_id(1)
    chunk = _chunk_of_step(j, n_chunks, n_ctx_chunks, rev)

    @pl.when(j == 0)
    def _():
        st_ref[...] = jnp.zeros_like(st_ref)

    row = _iota((c, 1), 0)
    seg_first = (chunk == 0) | (chunk == n_ctx_chunks)
    seg_last = (chunk == n_ctx_chunks - 1) | (chunk == n_chunks - 1)
    nr = (w // dh) * c
    si = _mod(_iota((1, nr), 1), c)
    strict = (si > row) if rev else (si < row)
    incl = (si >= row) if rev else (si <= row)
    eye_cat = (si == row).astype(F32)
    blk = _same_head(nr, c)
    bd = _same_head(w, dh)
    eye_w = (_iota((w, w), 0) == _iota((w, w), 1)).astype(F32)

    def prelude(group):
        us = []
        for bb in group:
            z = z_ref[bb]
            prev_row = jnp.where(seg_first, 0.0, zp_ref[bb, 7:8, :])
            next_row = jnp.where(seg_last, 0.0, zn_ref[bb, 0:1, :])
            z_prev = jnp.where(row == 0, prev_row, pltpu.roll(z, 1, 0))
            z_next = jnp.where(row == c - 1, next_row, pltpu.roll(z, c - 1, 0))
            us.append(z + mup_ref[...] * (z_prev - z) + mun_ref[...] * (z_next - z))
        u = jnp.concatenate(us, axis=0)
        r, k, v = u[:, 0:w], u[:, w:2 * w], u[:, 2 * w:3 * w]
        lora = u[:, 3 * w:3 * w + LANE]
        gd = u[:, 3 * w + LANE:3 * w + 2 * LANE]
        w_log = -_softplus(-(w0_ref[...] + _dot_x3(jnp.tanh(lora), w2_ref[0], w2_ref[1]))) - 0.5
        lw = -jnp.exp(w_log)
        a_lr = _sigmoid(a0_ref[...] + _dot_x3(lora, a2_ref[0], a2_ref[1]))
        kk = k * kk_ref[...]
        kd = k * (1.0 + (a_lr - 1.0) * ka_ref[...])
        kk_sq, bonus_dot = _head_sums([kk * kk, r * kd * rk_ref[...]], dh)
        kk = kk * lax.rsqrt(jnp.maximum(kk_sq, 1e-24))
        cols = dict(r=r, v=v, kd=kd, lw=lw, a=-kk, b=kk * a_lr, bonus=bonus_dot * v, gd=gd)
        return [{name: t[i * c:(i + 1) * c] for name, t in cols.items()} for i in range(len(group))]

    chains = list(range(nb))
    pre = prelude(chains)

    bf = lambda t: t.astype(BF16)
    ex = lambda t: _expand_heads(bf(t), dh)
    st = []
    for d in pre:
        v, kd, lw = d["v"], d["kd"], d["lw"]
        cum = _dot_sel(tri_ref[...], lw)
        tot = cum[0:1] if rev else cum[c - 1:c]
        inv_gam = jnp.exp(-cum)
        rest = jnp.exp(tot - cum)
        r_t = d["r"] * jnp.exp(cum)
        a_t = d["a"] * jnp.exp(cum - lw)
        ea, eb, ek, ev = ex(a_t), ex(d["b"] * inv_gam), ex(kd * inv_gam), ex(v)
        sc = _dot_nt(jnp.concatenate([bf(a_t), bf(r_t)], axis=0), jnp.concatenate([eb, ek], axis=0))
        st.append(dict(
            v=v, r_t=r_t, tot=tot, ea=ea, ev=ev, b_h=d["b"] * rest, k_h=kd * rest,
            a_ab=jnp.where(strict, sc[0:c, 0:nr], 0.0), a_ak=jnp.where(strict, sc[0:c, nr:2 * nr], 0.0),
            a_rb=jnp.where(incl, sc[c:2 * c, 0:nr], 0.0), a_rk=jnp.where(incl, sc[c:2 * c, nr:2 * nr], 0.0)))

    bd_cat = lambda t: jnp.where(blk, jnp.concatenate([bf(t)] * (nr // c), axis=0), 0.0)
    for s in st:
        s["pw"] = _dot(bf(s["a_ab"]), bd_cat(s["a_ab"]))
        s["t_cat"] = eye_cat + s["a_ab"]
    n_iter = int(math.log2(c)) - 1
    for it in range(n_iter):
        for s in st:
            pw = bf(s["pw"])
            if it + 1 < n_iter:
                both = _dot(pw, jnp.concatenate([bd_cat(s["t_cat"]), bd_cat(s["pw"])], axis=1))
                s["t_cat"], s["pw"] = s["t_cat"] + both[:, 0:nr], both[:, nr:2 * nr]
            else:
                s["t_cat"] = s["t_cat"] + _dot(pw, bd_cat(s["t_cat"]))

    for s in st:
        s["t_cat"] = bf(s["t_cat"])
        s["x_c"] = _dot(bf(s["a_ak"]), s["ev"])
    for s in st:
        s["w1c"] = _dot(s["t_cat"], s["ea"])
        s["w2c"] = _dot(s["t_cat"], ex(s["x_c"]))
    for s in st:
        a_rb = bf(s["a_rb"])
        s["p_mat"] = s["r_t"] + _dot(a_rb, ex(s["w1c"]))
        s["y0"] = _dot(jnp.concatenate([a_rb, bf(s["a_rk"])], axis=1),
                       jnp.concatenate([ex(s["w2c"]), s["ev"]], axis=0))
    ys = []
    for bb, s in zip(chains, st):
        lhs = jnp.concatenate([s["w1c"], s["p_mat"], eye_w * jnp.exp(s["tot"])], axis=0)
        big = _dot(lhs, st_ref[bb], P_ST)
        ys.append(big[c:2 * c] + s["y0"])
        upd = _dot_tn(jnp.concatenate([s["b_h"], s["k_h"]], axis=0),
                      jnp.concatenate([big[0:c] + s["w2c"], s["v"]], axis=0), P_ST)
        st_ref[bb] = big[2 * c:] + jnp.where(bd, upd, 0.0)

    y = jnp.concatenate(ys, axis=0)
    bonus_all = jnp.concatenate([d["bonus"] for d in pre], axis=0)
    if finish:
        gd = jnp.concatenate([d["gd"] for d in pre], axis=0)
        y = y + oy_ref[...].reshape(nb * c, w)
        (mean,) = _head_sums([y], dh)
        yc = y - mean * (1.0 / dh)
        (var,) = _head_sums([yc * yc], dh)
        yn = yc * lax.rsqrt(var * (1.0 / dh) + RW_GN_EPS) * lnw_ref[...] + lnb_ref[...]
        gate = _dot_x3(_sigmoid(gd), g2_ref[0], g2_ref[1])
        out = (yn + bonus_all + ob_ref[...].reshape(nb * c, w)) * gate
        out_ref[...] = out.reshape(nb, c, w).astype(out_ref.dtype)
    else:
        y_ref[...] = y.reshape(nb, c, w)
        bonus_ref[...] = bonus_all.reshape(nb, c, w)


def _rwkv_pass(u_rw, p, n_ctx_chunks, rev, other=None):
    b, n, wu = u_rw.shape
    c = CHUNK
    w = p["w0"].shape[-1]
    dh = w // N_HEADS
    nc = n // c
    finish = other is not None
    t = np.arange(c)
    tri = (t[None, :] >= t[:, None]) if rev else (t[None, :] <= t[:, None])
    tri = jnp.asarray(np.tile(tri.astype(np.float32), (1, 3)), BF16)
    hi_lo = lambda m: jnp.stack(_split2(m))
    d = 1 if rev else 0
    chunk = lambda j: _chunk_of_step(j, nc, n_ctx_chunks, rev)
    sub = c // 8
    nb = math.gcd(b, REC_BATCH)
    const = lambda a: pl.BlockSpec(a.shape, lambda bi, j: (0,) * a.ndim)
    col = pl.BlockSpec((nb, c, w), lambda bi, j: (bi, chunk(j), 0))
    args = [tri, u_rw, u_rw, u_rw, p["mu_prev"], p["mu_next"], p["w0"][d], hi_lo(p["w2"][d]), p["a0"][d],
            hi_lo(p["a2"][d]), hi_lo(p["g2"]), p["k_k"], p["k_a"], p["r_k"]]
    in_specs = [const(tri),
                pl.BlockSpec((nb, c, wu), lambda bi, j: (bi, chunk(j), 0)),
                pl.BlockSpec((nb, 8, wu), lambda bi, j: (bi, jnp.maximum(chunk(j) * sub - 1, 0), 0)),
                pl.BlockSpec((nb, 8, wu), lambda bi, j: (bi, jnp.minimum((chunk(j) + 1) * sub, n // 8 - 1), 0))]
    in_specs += [const(a) for a in args[4:]]
    if finish:
        args += [other[0], other[1], p["ln_w"], p["ln_b"]]
        in_specs += [col, col, const(p["ln_w"]), const(p["ln_b"])]
        out_specs, out_shape = col, jax.ShapeDtypeStruct((b, n, w), BF16)
    else:
        out_specs = [col, col]
        out_shape = [jax.ShapeDtypeStruct((b, n, w), F32)] * 2
    return pl.pallas_call(
        functools.partial(_rwkv_kernel, rev=rev, finish=finish, dh=dh, n_chunks=nc, n_ctx_chunks=n_ctx_chunks),
        grid=(b // nb, nc),
        in_specs=in_specs,
        out_specs=out_specs,
        out_shape=out_shape,
        scratch_shapes=[pltpu.VMEM((nb, w, w), F32)],
        compiler_params=_cparams(("parallel", "arbitrary")),
        name="rwkv_bwd" if rev else "rwkv_fwd",
    )(*args)


def _attn_lam(lam_ref, lam_init):
    lp = lam_ref[...]
    return (jnp.exp(jnp.sum(lp[0:1] * lp[1:2], axis=-1, keepdims=True))
            - jnp.exp(jnp.sum(lp[2:3] * lp[3:4], axis=-1, keepdims=True)) + lam_init)


def _stack_maps(q):
    half = q.shape[-1] // 2
    lane = _iota((1, q.shape[-1]), 1)
    return jnp.concatenate([jnp.where(lane < half, q, 0), jnp.where(lane >= half, q, 0)], axis=0)


def _attn_finish(o, ng, lam_init):
    ms = jnp.mean(o * o, axis=-1, keepdims=True)
    return o * lax.rsqrt(ms + RMS_EPS) * ng * (1.0 - lam_init)


def _attn_kernel(lam_ref, q_ref, k_ref, v_ref, ng_ref, o_ref, *, lam_init, n_ctx, n_ctx_tiles):
    lam = _attn_lam(lam_ref, lam_init)

    def attend(k, v):
        ts = ATT_SUB
        subs = range(q_ref.shape[0] // ts)
        s = [_dot_nt(_stack_maps(q_ref[i * ts:(i + 1) * ts, :]), k) for i in subs]
        for i, x in zip(subs, s):
            p = jnp.exp2(x - jnp.max(x, axis=-1, keepdims=True))
            den = jnp.sum(p, axis=-1, keepdims=True)
            pc = p[0:ts] - p[ts:2 * ts] * (lam * den[0:ts] / den[ts:2 * ts])
            o = _attn_finish(_dot(pc.astype(BF16), v) / den[0:ts], ng_ref[...], lam_init)
            o_ref[i * ts:(i + 1) * ts, :] = o.astype(o_ref.dtype)

    is_ctx = pl.program_id(2) < n_ctx_tiles

    @pl.when(is_ctx)
    def _():
        attend(k_ref[0:n_ctx, :], v_ref[0:n_ctx, :])

    @pl.when(jnp.logical_not(is_ctx))
    def _():
        attend(k_ref[...], v_ref[...])


def _attention(u_da, lam_rows, norm_g_row, lam_init, n_ctx):
    b, n, w3 = u_da.shape
    wd = w3 // 3
    dv = wd // N_HEADS
    tq = ROW_TILE
    kern = functools.partial(_attn_kernel, lam_init=lam_init, n_ctx=n_ctx, n_ctx_tiles=n_ctx // tq)
    return pl.pallas_call(
        kern,
        grid=(b, N_HEADS, n // tq),
        in_specs=[pl.BlockSpec(lam_rows.shape, lambda bi, h, i: (0, 0)),
                  pl.BlockSpec((None, tq, dv), lambda bi, h, i: (bi, i, h)),
                  pl.BlockSpec((None, n, dv), lambda bi, h, i: (bi, 0, N_HEADS + h)),
                  pl.BlockSpec((None, n, dv), lambda bi, h, i: (bi, 0, 2 * N_HEADS + h)),
                  pl.BlockSpec((1, dv), lambda bi, h, i: (0, 0))],
        out_specs=pl.BlockSpec((None, tq, dv), lambda bi, h, i: (bi, i, h)),
        out_shape=jax.ShapeDtypeStruct((b, n, wd), BF16),
        compiler_params=_cparams(("parallel", "parallel", "arbitrary")),
        name="diff_attn",
    )(lam_rows, u_da, u_da, u_da, norm_g_row)


def _mlp_kernel(*refs, final):
    if final:
        (s_ref, hg_ref, da_ref, rw_ref, mod_ref, g_ref, whg_ref, wda_ref, wrw_ref, wg_ref, wu_ref, wd_ref,
         fg_ref, o_ref) = refs
    else:
        (s_ref, hg_ref, da_ref, rw_ref, mod_ref, g_ref, whg_ref, wda_ref, wrw_ref, wg_ref, wu_ref, wd_ref,
         o_ref) = refs
    mix = _dot(hg_ref[...], whg_ref[...]) + _dot(da_ref[...], wda_ref[...]) + _dot(rw_ref[...], wrw_ref[...])
    x = s_ref[...] + mod_ref[2:3, :] * mix
    ms = jnp.mean(x * x, axis=-1, keepdims=True)
    y = x * lax.rsqrt(ms + RMS_EPS) * g_ref[...]
    h = (y * (1.0 + mod_ref[4:5, :]) + mod_ref[3:4, :]).astype(BF16)
    gate = _dot(h, wg_ref[...])
    up = _dot(h, wu_ref[...])
    act = (gate * _sigmoid(gate) * up).astype(BF16)
    x = x + mod_ref[5:6, :] * _dot(act, wd_ref[...])
    if final:
        ms = jnp.mean(x * x, axis=-1, keepdims=True)
        x = x * lax.rsqrt(ms + RMS_EPS) * fg_ref[...]
    o_ref[...] = x


def _mlp(s, o_hg, o_da, o_rw, mods_l, g, w_out_parts, wg, wu, wd, n_ctx_tiles, final_g=None):
    b, n, d = s.shape
    tm = ROW_TILE
    final = final_g is not None
    off = n_ctx_tiles if final else 0
    sel = lambda bi, i: (jnp.where(i + off < n_ctx_tiles, b, bi), 0, 0)
    resident = lambda a: pl.BlockSpec(a.shape, lambda bi, i: (0, 0), pipeline_mode=pl.Buffered(1))
    row = lambda a: pl.BlockSpec((None, tm, a.shape[-1]), lambda bi, i: (bi, i + off, 0))
    weights = [*w_out_parts, wg, wu, wd]
    in_specs = [row(s), row(o_hg), row(o_da), row(o_rw), pl.BlockSpec((None, 8, d), sel),
                pl.BlockSpec(g.shape, lambda bi, i: (0, 0))] + [resident(a) for a in weights]
    args = [s, o_hg, o_da, o_rw, mods_l, g, *weights]
    if final:
        in_specs.append(pl.BlockSpec((1, d), lambda bi, i: (0, 0)))
        args.append(final_g)
    return pl.pallas_call(
        functools.partial(_mlp_kernel, final=final),
        grid=(b, n // tm - off),
        in_specs=in_specs,
        out_specs=pl.BlockSpec((None, tm, d), lambda bi, i: (bi, i, 0)),
        out_shape=jax.ShapeDtypeStruct((b, n - off * tm, d), F32),
        compiler_params=_cparams(("parallel", "parallel")),
        name="mlp",
    )(*args)


def _rope_tables(n_ctx, n_lat, dqk):
    n_freq = dqk // 4
    t = np.arange(n_lat)
    inv_freq = ROPE_BASE ** (-np.arange(n_freq, dtype=np.float32) / n_freq)
    ang = np.concatenate([(t // GRID_W)[:, None] * inv_freq, (t % GRID_W)[:, None] * inv_freq], axis=-1)
    reps = LANE // dqk
    cos = np.tile(np.concatenate([np.cos(ang), np.cos(ang)], axis=-1), (1, reps))
    sin = np.tile(np.concatenate([-np.sin(ang), np.sin(ang)], axis=-1), (1, reps))
    cos = np.concatenate([np.ones((n_ctx, LANE)), cos], axis=0)
    sin = np.concatenate([np.zeros((n_ctx, LANE)), sin], axis=0)
    return jnp.asarray(cos, F32), jnp.asarray(sin, F32)


def _pad_rows(a, start, total):
    return jnp.zeros((total, a.shape[-1]), a.dtype).at[start:start + a.shape[0]].set(a)


def kernel(x, c, ctx, c_ctx, ada_w, ada_b, norm1_g, norm2_g, w_in, w_out, hg_lb_logits, hg_norm_g, da_lam_q1, da_lam_k1, da_lam_q2, da_lam_k2, da_norm_g, rw_mu_prev, rw_mu_next, rw_w0, rw_w2, rw_a0, rw_a2, rw_g2, rw_k_k, rw_k_a, rw_r_k, rw_ln_w, rw_ln_b, ffn_w_gate, ffn_w_up, ffn_w_down, final_norm_g):
    b, n_lat, d = x.shape
    n_ctx = ctx.shape[1]
    depth = ada_w.shape[0]
    hg_w = hg_lb_logits.shape[-1]
    rw_w = rw_w0.shape[-1]
    da_w = w_out.shape[1] - hg_w - rw_w
    hg_cols, da_cols = 5 * hg_w, 3 * da_w
    rw_cols = w_in.shape[-1] - hg_cols - da_cols
    rw_pad = 3 * rw_w + 2 * LANE
    r_dec, r_icl, r_gate = rw_w2.shape[2], rw_a2.shape[2], rw_g2.shape[1]
    assert n_ctx % ROW_TILE == 0 and n_lat % ROW_TILE == 0 and n_lat % GRID_W == 0
    assert 2 * (r_dec + r_icl) == LANE and r_gate <= LANE and rw_cols == 3 * rw_w + LANE + r_gate
    n_ctx_tiles, n_ctx_chunks = n_ctx // ROW_TILE, n_ctx // CHUNK

    cond = jnp.zeros((16, d), F32).at[:b].set(c).at[b].set(c_ctx)
    mods = _ada_mod(cond, ada_w, ada_b).reshape(depth, 16, 6, d)
    mods = jnp.pad(mods, ((0, 0), (0, 0), (0, 2), (0, 0)))

    p_lb = jax.nn.softmax(hg_lb_logits.astype(F32), axis=1)
    lower_bounds = jnp.cumsum(p_lb, axis=1) - p_lb[:, :1]
    cos_t, sin_t = _rope_tables(n_ctx, n_lat, da_w // (2 * N_HEADS))

    s = jnp.concatenate([ctx, x], axis=1)
    for l in range(depth):
        w_l = w_in[l]
        whg = w_l[:, :hg_cols].astype(BF16)
        wda = w_l[:, hg_cols:hg_cols + da_cols].astype(BF16)
        wrw = jnp.pad(w_l[:, hg_cols + da_cols:], ((0, 0), (0, rw_pad - rw_cols))).astype(BF16)
        u_hg, u_da, u_rw = _in_proj(s, mods[l], norm1_g[l][None], cos_t, sin_t, whg, wda, wrw, n_ctx_tiles)

        o_b = _hgrn_pass(u_hg, lower_bounds[1, l][None], n_ctx_chunks, True)
        o_hg = _hgrn_pass(u_hg, lower_bounds[0, l][None], n_ctx_chunks, False, other=o_b,
                          norm_g_row=jnp.tile(hg_norm_g[l], N_HEADS)[None])

        lam_init = 0.8 - 0.6 * math.exp(-0.3 * l)
        lam_rows = jnp.zeros((8, LANE), F32).at[0:4, :da_lam_q1.shape[-1]].set(
            jnp.stack([da_lam_q1[l], da_lam_k1[l], da_lam_q2[l], da_lam_k2[l]]))
        o_da = _attention(u_da, lam_rows, da_norm_g[l][None], lam_init, n_ctx)

        mu_pad = lambda m: jnp.pad(m, (0, rw_pad - rw_cols))[None]
        off_dec, off_icl = 0, 2 * r_dec
        rw_p = {
            "mu_prev": mu_pad(rw_mu_prev[l]), "mu_next": mu_pad(rw_mu_next[l]),
            "w0": rw_w0[:, l][:, None, :], "a0": rw_a0[:, l][:, None, :],
            "w2": jnp.stack([_pad_rows(rw_w2[i, l], off_dec + i * r_dec, LANE) for i in range(2)]),
            "a2": jnp.stack([_pad_rows(rw_a2[i, l], off_icl + i * r_icl, LANE) for i in range(2)]),
            "g2": _pad_rows(rw_g2[l], 0, LANE),
            "k_k": rw_k_k[l][None], "k_a": rw_k_a[l][None], "r_k": rw_r_k[l][None],
            "ln_w": rw_ln_w[l][None], "ln_b": rw_ln_b[l][None],
        }
        yb = _rwkv_pass(u_rw, rw_p, n_ctx_chunks, True)
        o_rw = _rwkv_pass(u_rw, rw_p, n_ctx_chunks, False, other=yb)

        wo = w_out[l].astype(BF16)
        s = _mlp(s, o_hg, o_da, o_rw, mods[l], norm2_g[l][None],
                 (wo[:hg_w], wo[hg_w:hg_w + da_w], wo[hg_w + da_w:]),
                 ffn_w_gate[l].astype(BF16), ffn_w_up[l].astype(BF16), ffn_w_down[l].astype(BF16),
                 n_ctx_tiles, final_g=final_norm_g[None] if l == depth - 1 else None)
    return s
```

```python
import functools
import math

import numpy as np
import jax
import jax.numpy as jnp
from jax import lax
from jax.experimental import pallas as pl
from jax.experimental.pallas import tpu as pltpu

F32 = jnp.float32
BF16 = jnp.bfloat16
HI = lax.Precision.HIGHEST

N_HEADS = 4
GRID_W = 64
ROPE_BASE = 10000.0
RMS_EPS = 1e-6
RW_GN_EPS = 64e-5

ROW_TILE = 256
DENSE_BATCH = 2
ATT_SUB = 64
CHUNK = 64
REC_BATCH = 8
LANE = 128
VMEM_LIMIT = 56 * 1024 * 1024


def _cparams(sem):
    return pltpu.CompilerParams(dimension_semantics=sem, vmem_limit_bytes=VMEM_LIMIT)


def _dot_dims(a, b, dims, prec):
    dg = lambda x, y, p=None: lax.dot_general(x, y, (dims, ((), ())), preferred_element_type=F32, precision=p)
    if prec == "bf16":
        return dg(a.astype(BF16), b.astype(BF16))
    if prec == "x3":
        a_hi, b_hi = a.astype(BF16), b.astype(BF16)
        a_lo = (a - a_hi.astype(F32)).astype(BF16)
        b_lo = (b - b_hi.astype(F32)).astype(BF16)
        return dg(a_hi, b_hi) + (dg(a_lo, b_hi) + dg(a_hi, b_lo))
    return dg(a, b, prec)


def _dot(a, b, prec=None):
    return _dot_dims(a, b, ((1,), (0,)), prec)


def _dot_nt(a, b, prec=None):
    return _dot_dims(a, b, ((1,), (1,)), prec)


def _dot_tn(a, b, prec=None):
    return _dot_dims(a, b, ((0,), (0,)), prec)


P_HG = "bf16"
P_ST = "bf16"


def _iota(shape, dim):
    return lax.broadcasted_iota(jnp.int32, shape, dim)


def _div(x, d):
    assert d & (d - 1) == 0
    return x >> (d.bit_length() - 1)


def _mod(x, d):
    assert d & (d - 1) == 0
    return x & (d - 1)


def _softplus(x):
    return jnp.maximum(x, 0.0) + jnp.log(1.0 + jnp.exp(-jnp.abs(x)))


def _same_head(w, dh):
    return _div(_iota((w, w), 0), dh) == _div(_iota((w, w), 1), dh)


def _expand_heads(x, dh):
    w = x.shape[-1]
    head = _div(_iota((1, w), 1), dh)
    return jnp.concatenate([jnp.where(head == h, x, 0.0) for h in range(w // dh)], axis=0)


def _collapse_heads(x, c):
    out = x[0:c]
    for h in range(1, x.shape[0] // c):
        out = out + x[h * c:(h + 1) * c]
    return out


def _split3(x):
    hi = x.astype(BF16)
    r = x - hi.astype(F32)
    mid = r.astype(BF16)
    lo = (r - mid.astype(F32)).astype(BF16)
    return hi, mid, lo


def _split2(x):
    hi = x.astype(BF16)
    return hi, (x - hi.astype(F32)).astype(BF16)


def _head_sums(xs, dh):
    w = xs[0].shape[-1]
    ones_bd = _same_head(w, dh).astype(BF16)
    out = _dot(jnp.concatenate([p for x in xs for p in _split3(x)], axis=0), ones_bd)
    res, off = [], 0
    for x in xs:
        r = x.shape[0]
        res.append(out[off:off + r] + (out[off + r:off + 2 * r] + out[off + 2 * r:off + 3 * r]))
        off += 3 * r
    return res


def _dot_sel(sel3, x):
    return _dot(sel3, jnp.concatenate(_split3(x), axis=0))


def _dot_x3(x, w_hi, w_lo):
    x_hi, x_lo = _split2(x)
    r = x.shape[0]
    t = _dot(jnp.concatenate([x_hi, x_lo], axis=0), w_hi)
    return t[0:r] + t[r:2 * r] + _dot(x_hi, w_lo)


def _sigmoid(x):
    return 1.0 / (1.0 + jnp.exp(-x))


def _ada_kernel(c_ref, w_ref, b_ref, o_ref):
    c = c_ref[...]
    sc = c * _sigmoid(c)
    o_ref[...] = _dot(sc, w_ref[...], HI) + b_ref[...]


def _ada_mod(cond, ada_w, ada_b):
    depth, d, d6 = ada_w.shape
    rows = cond.shape[0]
    tn = 1536
    return pl.pallas_call(
        _ada_kernel,
        grid=(depth, d6 // tn),
        in_specs=[pl.BlockSpec((rows, d), lambda l, j: (0, 0)),
                  pl.BlockSpec((None, d, tn), lambda l, j: (l, 0, j)),
                  pl.BlockSpec((None, 1, tn), lambda l, j: (l, 0, j))],
        out_specs=pl.BlockSpec((None, rows, tn), lambda l, j: (l, 0, j)),
        out_shape=jax.ShapeDtypeStruct((depth, rows, d6), F32),
        compiler_params=_cparams(("arbitrary", "arbitrary")),
        name="ada_mod",
    )(cond, ada_w, ada_b.reshape(depth, 1, d6))


def _rope_cols(t, cos, sin_signed, half):
    lane = _iota((1, t.shape[-1]), 1)
    first = _mod(lane, 2 * half) < half
    swapped = jnp.where(first, pltpu.roll(t, t.shape[-1] - half, 1), pltpu.roll(t, half, 1))
    return t * cos + swapped * sin_signed


def _in_proj_kernel(s_ref, mod_ref, g_ref, cos_ref, sin_ref, whg_ref, wda_ref, wrw_ref,
                    uhg_ref, uda_ref, urw_ref, *, da_width, q_scale):
    nbp, tm, _ = s_ref.shape
    hs = []
    for p in range(nbp):
        x = s_ref[p]
        ms = jnp.mean(x * x, axis=-1, keepdims=True)
        y = x * lax.rsqrt(ms + RMS_EPS) * g_ref[...]
        hs.append((y * (1.0 + mod_ref[p, 1:2, :]) + mod_ref[p, 0:1, :]).astype(BF16))
    h = jnp.concatenate(hs, axis=0)
    uhg = _dot(h, whg_ref[...])
    urw = _dot(h, wrw_ref[...])
    uda = _dot(h, wda_ref[...])
    cos = cos_ref[...]
    sin = sin_ref[...]
    dqk = da_width // (2 * N_HEADS)
    for p in range(nbp):
        rows = slice(p * tm, (p + 1) * tm)
        uhg_ref[p] = uhg[rows]
        urw_ref[p] = urw[rows]
        for j in range(2 * da_width // LANE):
            t = _rope_cols(uda[rows, j * LANE:(j + 1) * LANE], cos, sin, dqk // 2)
            if j < da_width // LANE:
                t = t * q_scale
            uda_ref[p, :, j * LANE:(j + 1) * LANE] = t.astype(BF16)
        uda_ref[p, :, 2 * da_width:] = uda[rows, 2 * da_width:].astype(BF16)


def _dense_batch(b):
    return math.gcd(b, DENSE_BATCH)


def _mod_spec(b, d, n_ctx_tiles, off=0):
    nbp = _dense_batch(b)
    return pl.BlockSpec((nbp, 8, d), lambda bi, i: (jnp.where(i + off < n_ctx_tiles, b // nbp, bi), 0, 0))


def _in_proj(s, mods_l, g, cos_t, sin_t, whg, wda, wrw, n_ctx_tiles):
    b, n, d = s.shape
    tm = ROW_TILE
    nbp = _dense_batch(b)
    da_width = wda.shape[1] // 3
    dqk = da_width // (2 * N_HEADS)
    full = lambda a: pl.BlockSpec(a.shape, lambda bi, i: (0, 0))
    row = lambda w: pl.BlockSpec((nbp, tm, w), lambda bi, i: (bi, i, 0))
    kern = functools.partial(_in_proj_kernel, da_width=da_width, q_scale=dqk ** -0.5 * math.log2(math.e))
    return pl.pallas_call(
        kern,
        grid=(b // nbp, n // tm),
        in_specs=[row(d), _mod_spec(b, d, n_ctx_tiles), full(g),
                  pl.BlockSpec((tm, LANE), lambda bi, i: (i, 0)),
                  pl.BlockSpec((tm, LANE), lambda bi, i: (i, 0)),
                  full(whg), full(wda), full(wrw)],
        out_specs=[row(whg.shape[1]), row(wda.shape[1]), row(wrw.shape[1])],
        out_shape=[jax.ShapeDtypeStruct((b, n, whg.shape[1]), F32),
                   jax.ShapeDtypeStruct((b, n, wda.shape[1]), BF16),
                   jax.ShapeDtypeStruct((b, n, wrw.shape[1]), F32)],
        compiler_params=_cparams(("parallel", "parallel")),
        name="in_proj",
    )(s, mods_l, g, cos_t, sin_t, whg, wda, wrw)


def _chunk_of_step(j, n_chunks, n_ctx_chunks, rev):
    if not rev:
        return j
    return jnp.where(j < n_ctx_chunks, n_ctx_chunks - 1 - j, n_chunks - 1 - (j - n_ctx_chunks))


def _hgrn_levels(c):
    return [1 << i for i in range(int(math.log2(c)))]


def _hgrn_const(c, rev):
    t = np.arange(c)[:, None]
    s = np.arange(c)[None, :]
    tri = (s >= t) if rev else (s <= t)
    mats = [tri.astype(np.float32)]
    for m in _hgrn_levels(c):
        p = (t // (2 * m)) * (2 * m) + (m if rev else m - 1)
        ref = (s >= p) if rev else (s <= p)
        mats.append(tri.astype(np.float32) - ref.astype(np.float32))
    return np.concatenate(mats, axis=0)


def _hgrn_kernel(*refs, rev, finish, dh):
    if finish:
        const_ref, q_ref, i_ref, f_ref, lb_ref, g_ref, other_ref, ng_ref, out_ref, st_ref = refs
    else:
        const_ref, q_ref, i_ref, f_ref, lb_ref, out_ref, st_ref = refs
    nb, c, w = q_ref.shape

    @pl.when(pl.program_id(1) == 0)
    def _():
        st_ref[...] = jnp.zeros_like(st_ref)

    nh = w // dh
    t_idx = _iota((c, 1), 0)
    s_idx = _mod(_iota((1, nh * c), 1), c)
    bd = _same_head(w, dh)
    lb = lb_ref[...]
    rows = lambda ref: ref[...].reshape(nb * c, w)
    q_all = rows(q_ref)
    v_all = rows(i_ref)
    f_all = lb + (1.0 - lb) * _sigmoid(rows(f_ref))
    kx_all = 1.0 - f_all
    logf_all = jnp.log(f_all)
    (diag_all,) = _head_sums([q_all * kx_all], dh)
    bf = lambda t: t.astype(BF16)
    cs = []
    for bb in range(nb):
        sl = slice(bb * c, (bb + 1) * c)
        seg = _dot_sel(const_ref[...], logf_all[sl])
        bcum = seg[0:c]
        cs.append(dict(bb=bb, q=q_all[sl], v=v_all[sl], kx=kx_all[sl], diag=diag_all[sl], seg=seg, bcum=bcum,
                       bend=bcum[0:1] if rev else bcum[c - 1:c], att=jnp.zeros((c, nh * c), F32)))
    for li, m in enumerate(_hgrn_levels(c)):
        t_up = (t_idx & m) != 0
        s_up = (s_idx & m) != 0
        t_isq = jnp.logical_not(t_up) if rev else t_up
        s_isk = s_up if rev else jnp.logical_not(s_up)
        valid = (_div(t_idx, 2 * m) == _div(s_idx, 2 * m)) & t_isq & s_isk
        for s in cs:
            e = jnp.exp(-jnp.abs(s["seg"][(li + 1) * c:(li + 2) * c]))
            xm = bf(jnp.where(t_isq, s["q"], s["kx"]) * e)
            sc = _dot_nt(xm, _expand_heads(xm, dh))
            s["att"] = s["att"] + jnp.where(valid, sc, 0.0)
    outs = []
    for s in cs:
        q, v, kx, bcum, bend = s["q"], s["v"], s["kx"], s["bcum"], s["bend"]
        o = _dot(bf(s["att"]), _expand_heads(bf(v), dh))
        o = o + s["diag"] * v
        st = st_ref[s["bb"]]
        o = o + _dot_nt(q * jnp.exp(bcum), st, P_HG)
        upd = _dot_tn(v, kx * jnp.exp(bend - bcum), P_HG)
        st_ref[s["bb"]] = st * jnp.exp(bend) + jnp.where(bd, upd, 0.0)
        outs.append(o)
    o_all = jnp.concatenate(outs, axis=0)
    if finish:
        o_all = o_all + rows(other_ref)
        (ms,) = _head_sums([o_all * o_all], dh)
        g = rows(g_ref)
        o_all = o_all * lax.rsqrt(ms * (1.0 / dh) + RMS_EPS) * ng_ref[...] * (g * _sigmoid(g))
    out_ref[...] = o_all.reshape(nb, c, w).astype(out_ref.dtype)


def _hgrn_pass(u_hg, lb_row, n_ctx_chunks, rev, other=None, norm_g_row=None):
    b, n, _ = u_hg.shape
    c = CHUNK
    w = lb_row.shape[-1]
    dh = w // N_HEADS
    nc = n // c
    finish = other is not None
    const = jnp.asarray(np.tile(_hgrn_const(c, rev), (1, 3)), BF16)
    chunk = lambda j: _chunk_of_step(j, nc, n_ctx_chunks, rev)
    nb = math.gcd(b, REC_BATCH)
    col = lambda k: pl.BlockSpec((nb, c, w), lambda bi, j: (bi, chunk(j), k))
    vec = pl.BlockSpec((1, w), lambda bi, j: (0, 0))
    in_specs = [pl.BlockSpec(const.shape, lambda bi, j: (0, 0)), col(0), col(1), col(3 if rev else 2), vec]
    args = [const, u_hg, u_hg, u_hg, lb_row]
    if finish:
        in_specs += [col(4), col(0), vec]
        args += [u_hg, other, norm_g_row]
    return pl.pallas_call(
        functools.partial(_hgrn_kernel, rev=rev, finish=finish, dh=dh),
        grid=(b // nb, nc),
        in_specs=in_specs,
        out_specs=col(0),
        out_shape=jax.ShapeDtypeStruct((b, n, w), BF16 if finish else F32),
        scratch_shapes=[pltpu.VMEM((nb, w, w), F32)],
        compiler_params=_cparams(("parallel", "arbitrary")),
        name="hgrn_bwd" if rev else "hgrn_fwd",
    )(*args)


def _rwkv_kernel(*refs, rev, finish, dh, n_chunks, n_ctx_chunks):
    if finish:
        (tri_ref, z_ref, zp_ref, zn_ref, mup_ref, mun_ref, w0_ref, w2_ref, a0_ref, a2_ref, g2_ref,
         kk_ref, ka_ref, rk_ref, oy_ref, ob_ref, lnw_ref, lnb_ref, out_ref, st_ref) = refs
    else:
        (tri_ref, z_ref, zp_ref, zn_ref, mup_ref, mun_ref, w0_ref, w2_ref, a0_ref, a2_ref, g2_ref,
         kk_ref, ka_ref, rk_ref, y_ref, bonus_ref, st_ref) = refs
    nb, c, _ = z_ref.shape
    w = w0_ref.shape[-1]
    j = pl.program_id(1)
    chunk = _chunk_of_step(j, n_chunks, n_ctx_chunks, rev)

    @pl.when(j == 0)
    def _():
        st_ref[...] = jnp.zeros_like(st_ref)

    row = _iota((c, 1), 0)
    seg_first = (chunk == 0) | (chunk == n_ctx_chunks)
    seg_last = (chunk == n_ctx_chunks - 1) | (chunk == n_chunks - 1)
    nr = (w // dh) * c
    si = _mod(_iota((1, nr), 1), c)
    strict = (si > row) if rev else (si < row)
    incl = (si >= row) if rev else (si <= row)
    eye_cat = (si == row).astype(F32)
    blk = _same_head(nr, c)
    bd = _same_head(w, dh)
    eye_w = (_iota((w, w), 0) == _iota((w, w), 1)).astype(F32)

    def prelude(group):
        us = []
        for bb in group:
            z = z_ref[bb]
            prev_row = jnp.where(seg_first, 0.0, zp_ref[bb, 7:8, :])
            next_row = jnp.where(seg_last, 0.0, zn_ref[bb, 0:1, :])
            z_prev = jnp.where(row == 0, prev_row, pltpu.roll(z, 1, 0))
            z_next = jnp.where(row == c - 1, next_row, pltpu.roll(z, c - 1, 0))
            us.append(z + mup_ref[...] * (z_prev - z) + mun_ref[...] * (z_next - z))
        u = jnp.concatenate(us, axis=0)
        r, k, v = u[:, 0:w], u[:, w:2 * w], u[:, 2 * w:3 * w]
        lora = u[:, 3 * w:3 * w + LANE]
        gd = u[:, 3 * w + LANE:3 * w + 2 * LANE]
        w_log = -_softplus(-(w0_ref[...] + _dot_x3(jnp.tanh(lora), w2_ref[0], w2_ref[1]))) - 0.5
        lw = -jnp.exp(w_log)
        a_lr = _sigmoid(a0_ref[...] + _dot_x3(lora, a2_ref[0], a2_ref[1]))
        kk = k * kk_ref[...]
        kd = k * (1.0 + (a_lr - 1.0) * ka_ref[...])
        kk_sq, bonus_dot = _head_sums([kk * kk, r * kd * rk_ref[...]], dh)
        kk = kk * lax.rsqrt(jnp.maximum(kk_sq, 1e-24))
        cols = dict(r=r, v=v, kd=kd, lw=lw, a=-kk, b=kk * a_lr, bonus=bonus_dot * v, gd=gd)
        return [{name: t[i * c:(i + 1) * c] for name, t in cols.items()} for i in range(len(group))]

    chains = list(range(nb))
    pre = prelude(chains)

    bf = lambda t: t.astype(BF16)
    ex = lambda t: _expand_heads(bf(t), dh)
    st = []
    for d in pre:
        v, kd, lw = d["v"], d["kd"], d["lw"]
        cum = _dot_sel(tri_ref[...], lw)
        tot = cum[0:1] if rev else cum[c - 1:c]
        inv_gam = jnp.exp(-cum)
        rest = jnp.exp(tot - cum)
        r_t = d["r"] * jnp.exp(cum)
        a_t = d["a"] * jnp.exp(cum - lw)
        ea, eb, ek, ev = ex(a_t), ex(d["b"] * inv_gam), ex(kd * inv_gam), ex(v)
        sc = _dot_nt(jnp.concatenate([bf(a_t), bf(r_t)], axis=0), jnp.concatenate([eb, ek], axis=0))
        st.append(dict(
            v=v, r_t=r_t, tot=tot, ea=ea, ev=ev, b_h=d["b"] * rest, k_h=kd * rest,
            a_ab=jnp.where(strict, sc[0:c, 0:nr], 0.0), a_ak=jnp.where(strict, sc[0:c, nr:2 * nr], 0.0),
            a_rb=jnp.where(incl, sc[c:2 * c, 0:nr], 0.0), a_rk=jnp.where(incl, sc[c:2 * c, nr:2 * nr], 0.0)))

    bd_cat = lambda t: jnp.where(blk, jnp.concatenate([bf(t)] * (nr // c), axis=0), 0.0)
    for s in st:
        s["pw"] = _dot(bf(s["a_ab"]), bd_cat(s["a_ab"]))
        s["t_cat"] = eye_cat + s["a_ab"]
    n_iter = int(math.log2(c)) - 1
    for it in range(n_iter):
        for s in st:
            pw = bf(s["pw"])
            if it + 1 < n_iter:
                both = _dot(pw, jnp.concatenate([bd_cat(s["t_cat"]), bd_cat(s["pw"])], axis=1))
                s["t_cat"], s["pw"] = s["t_cat"] + both[:, 0:nr], both[:, nr:2 * nr]
            else:
                s["t_cat"] = s["t_cat"] + _dot(pw, bd_cat(s["t_cat"]))

    for s in st:
        s["t_cat"] = bf(s["t_cat"])
        s["x_c"] = _dot(bf(s["a_ak"]), s["ev"])
    for s in st:
        s["w1c"] = _dot(s["t_cat"], s["ea"])
        s["w2c"] = _dot(s["t_cat"], ex(s["x_c"]))
    for s in st:
        a_rb = bf(s["a_rb"])
        s["p_mat"] = s["r_t"] + _dot(a_rb, ex(s["w1c"]))
        s["y0"] = _dot(jnp.concatenate([a_rb, bf(s["a_rk"])], axis=1),
                       jnp.concatenate([ex(s["w2c"]), s["ev"]], axis=0))
    ys = []
    for bb, s in zip(chains, st):
        lhs = jnp.concatenate([s["w1c"], s["p_mat"], eye_w * jnp.exp(s["tot"])], axis=0)
        big = _dot(lhs, st_ref[bb], P_ST)
        ys.append(big[c:2 * c] + s["y0"])
        upd = _dot_tn(jnp.concatenate([s["b_h"], s["k_h"]], axis=0),
                      jnp.concatenate([big[0:c] + s["w2c"], s["v"]], axis=0), P_ST)
        st_ref[bb] = big[2 * c:] + jnp.where(bd, upd, 0.0)

    y = jnp.concatenate(ys, axis=0)
    bonus_all = jnp.concatenate([d["bonus"] for d in pre], axis=0)
    if finish:
        gd = jnp.concatenate([d["gd"] for d in pre], axis=0)
        y = y + oy_ref[...].reshape(nb * c, w)
        (mean,) = _head_sums([y], dh)
        yc = y - mean * (1.0 / dh)
        (var,) = _head_sums([yc * yc], dh)
        yn = yc * lax.rsqrt(var * (1.0 / dh) + RW_GN_EPS) * lnw_ref[...] + lnb_ref[...]
        gate = _dot_x3(_sigmoid(gd), g2_ref[0], g2_ref[1])
        out = (yn + bonus_all + ob_ref[...].reshape(nb * c, w)) * gate
        out_ref[...] = out.reshape(nb, c, w).astype(out_ref.dtype)
    else:
        y_ref[...] = y.reshape(nb, c, w)
        bonus_ref[...] = bonus_all.reshape(nb, c, w)


def _rwkv_pass(u_rw, p, n_ctx_chunks, rev, other=None):
    b, n, wu = u_rw.shape
    c = CHUNK
    w = p["w0"].shape[-1]
    dh = w // N_HEADS
    nc = n // c
    finish = other is not None
    t = np.arange(c)
    tri = (t[None, :] >= t[:, None]) if rev else (t[None, :] <= t[:, None])
    tri = jnp.asarray(np.tile(tri.astype(np.float32), (1, 3)), BF16)
    hi_lo = lambda m: jnp.stack(_split2(m))
    d = 1 if rev else 0
    chunk = lambda j: _chunk_of_step(j, nc, n_ctx_chunks, rev)
    sub = c // 8
    nb = math.gcd(b, REC_BATCH)
    const = lambda a: pl.BlockSpec(a.shape, lambda bi, j: (0,) * a.ndim)
    col = pl.BlockSpec((nb, c, w), lambda bi, j: (bi, chunk(j), 0))
    args = [tri, u_rw, u_rw, u_rw, p["mu_prev"], p["mu_next"], p["w0"][d], hi_lo(p["w2"][d]), p["a0"][d],
            hi_lo(p["a2"][d]), hi_lo(p["g2"]), p["k_k"], p["k_a"], p["r_k"]]
    in_specs = [const(tri),
                pl.BlockSpec((nb, c, wu), lambda bi, j: (bi, chunk(j), 0)),
                pl.BlockSpec((nb, 8, wu), lambda bi, j: (bi, jnp.maximum(chunk(j) * sub - 1, 0), 0)),
                pl.BlockSpec((nb, 8, wu), lambda bi, j: (bi, jnp.minimum((chunk(j) + 1) * sub, n // 8 - 1), 0))]
    in_specs += [const(a) for a in args[4:]]
    if finish:
        args += [other[0], other[1], p["ln_w"], p["ln_b"]]
        in_specs += [col, col, const(p["ln_w"]), const(p["ln_b"])]
        out_specs, out_shape = col, jax.ShapeDtypeStruct((b, n, w), BF16)
    else:
        out_specs = [col, col]
        out_shape = [jax.ShapeDtypeStruct((b, n, w), F32)] * 2
    return pl.pallas_call(
        functools.partial(_rwkv_kernel, rev=rev, finish=finish, dh=dh, n_chunks=nc, n_ctx_chunks=n_ctx_chunks),
        grid=(b // nb, nc),
        in_specs=in_specs,
        out_specs=out_specs,
        out_shape=out_shape,
        scratch_shapes=[pltpu.VMEM((nb, w, w), F32)],
        compiler_params=_cparams(("parallel", "arbitrary")),
        name="rwkv_bwd" if rev else "rwkv_fwd",
    )(*args)


def _attn_lam(lam_ref, lam_init):
    lp = lam_ref[...]
    return (jnp.exp(jnp.sum(lp[0:1] * lp[1:2], axis=-1, keepdims=True))
            - jnp.exp(jnp.sum(lp[2:3] * lp[3:4], axis=-1, keepdims=True)) + lam_init)


def _stack_maps(q):
    half = q.shape[-1] // 2
    lane = _iota((1, q.shape[-1]), 1)
    return jnp.concatenate([jnp.where(lane < half, q, 0), jnp.where(lane >= half, q, 0)], axis=0)


def _attn_finish(o, ng, lam_init):
    ms = jnp.mean(o * o, axis=-1, keepdims=True)
    return o * lax.rsqrt(ms + RMS_EPS) * ng * (1.0 - lam_init)


def _attn_kernel(lam_ref, q_ref, k_ref, v_ref, ng_ref, o_ref, *, lam_init, n_ctx, n_ctx_tiles):
    lam = _attn_lam(lam_ref, lam_init)

    def attend(k, v):
        ts = ATT_SUB
        subs = range(q_ref.shape[0] // ts)
        s = [_dot_nt(_stack_maps(q_ref[i * ts:(i + 1) * ts, :]), k) for i in subs]
        for i, x in zip(subs, s):
            p = jnp.exp2(x - jnp.max(x, axis=-1, keepdims=True))
            den = jnp.sum(p, axis=-1, keepdims=True)
            pc = p[0:ts] - p[ts:2 * ts] * (lam * den[0:ts] / den[ts:2 * ts])
            o = _attn_finish(_dot(pc.astype(BF16), v) / den[0:ts], ng_ref[...], lam_init)
            o_ref[i * ts:(i + 1) * ts, :] = o.astype(o_ref.dtype)

    is_ctx = pl.program_id(2) < n_ctx_tiles

    @pl.when(is_ctx)
    def _():
        attend(k_ref[0:n_ctx, :], v_ref[0:n_ctx, :])

    @pl.when(jnp.logical_not(is_ctx))
    def _():
        attend(k_ref[...], v_ref[...])


def _attention(u_da, lam_rows, norm_g_row, lam_init, n_ctx):
    b, n, w3 = u_da.shape
    wd = w3 // 3
    dv = wd // N_HEADS
    tq = ROW_TILE
    kern = functools.partial(_attn_kernel, lam_init=lam_init, n_ctx=n_ctx, n_ctx_tiles=n_ctx // tq)
    return pl.pallas_call(
        kern,
        grid=(b, N_HEADS, n // tq),
        in_specs=[pl.BlockSpec(lam_rows.shape, lambda bi, h, i: (0, 0)),
                  pl.BlockSpec((None, tq, dv), lambda bi, h, i: (bi, i, h)),
                  pl.BlockSpec((None, n, dv), lambda bi, h, i: (bi, 0, N_HEADS + h)),
                  pl.BlockSpec((None, n, dv), lambda bi, h, i: (bi, 0, 2 * N_HEADS + h)),
                  pl.BlockSpec((1, dv), lambda bi, h, i: (0, 0))],
        out_specs=pl.BlockSpec((None, tq, dv), lambda bi, h, i: (bi, i, h)),
        out_shape=jax.ShapeDtypeStruct((b, n, wd), BF16),
        compiler_params=_cparams(("parallel", "parallel", "arbitrary")),
        name="diff_attn",
    )(lam_rows, u_da, u_da, u_da, norm_g_row)


def _mlp_kernel(*refs, final):
    if final:
        (s_ref, hg_ref, da_ref, rw_ref, mod_ref, g_ref, whg_ref, wda_ref, wrw_ref, wg_ref, wu_ref, wd_ref,
         fg_ref, o_ref) = refs
    else:
        (s_ref, hg_ref, da_ref, rw_ref, mod_ref, g_ref, whg_ref, wda_ref, wrw_ref, wg_ref, wu_ref, wd_ref,
         o_ref) = refs
    nbp, tm, d = s_ref.shape
    stack = lambda ref: ref[...].reshape(nbp * tm, ref.shape[-1])
    mix = _dot(stack(hg_ref), whg_ref[...]) + _dot(stack(da_ref), wda_ref[...]) + _dot(stack(rw_ref), wrw_ref[...])
    xs, hs = [], []
    for p in range(nbp):
        x = s_ref[p] + mod_ref[p, 2:3, :] * mix[p * tm:(p + 1) * tm]
        ms = jnp.mean(x * x, axis=-1, keepdims=True)
        y = x * lax.rsqrt(ms + RMS_EPS) * g_ref[...]
        xs.append(x)
        hs.append((y * (1.0 + mod_ref[p, 4:5, :]) + mod_ref[p, 3:4, :]).astype(BF16))
    h = jnp.concatenate(hs, axis=0)
    gate = _dot(h, wg_ref[...])
    up = _dot(h, wu_ref[...])
    down = _dot((gate * _sigmoid(gate) * up).astype(BF16), wd_ref[...])
    for p in range(nbp):
        x = xs[p] + mod_ref[p, 5:6, :] * down[p * tm:(p + 1) * tm]
        if final:
            ms = jnp.mean(x * x, axis=-1, keepdims=True)
            x = x * lax.rsqrt(ms + RMS_EPS) * fg_ref[...]
        o_ref[p] = x


def _mlp(s, o_hg, o_da, o_rw, mods_l, g, w_out_parts, wg, wu, wd, n_ctx_tiles, final_g=None):
    b, n, d = s.shape
    tm = ROW_TILE
    final = final_g is not None
    off = n_ctx_tiles if final else 0
    nbp = _dense_batch(b)
    resident = lambda a: pl.BlockSpec(a.shape, lambda bi, i: (0, 0), pipeline_mode=pl.Buffered(1))
    row = lambda a: pl.BlockSpec((nbp, tm, a.shape[-1]), lambda bi, i: (bi, i + off, 0))
    weights = [*w_out_parts, wg, wu, wd]
    in_specs = [row(s), row(o_hg), row(o_da), row(o_rw), _mod_spec(b, d, n_ctx_tiles, off),
                pl.BlockSpec(g.shape, lambda bi, i: (0, 0))] + [resident(a) for a in weights]
    args = [s, o_hg, o_da, o_rw, mods_l, g, *weights]
    if final:
        in_specs.append(pl.BlockSpec((1, d), lambda bi, i: (0, 0)))
        args.append(final_g)
    return pl.pallas_call(
        functools.partial(_mlp_kernel, final=final),
        grid=(b // nbp, n // tm - off),
        in_specs=in_specs,
        out_specs=pl.BlockSpec((nbp, tm, d), lambda bi, i: (bi, i, 0)),
        out_shape=jax.ShapeDtypeStruct((b, n - off * tm, d), F32),
        compiler_params=_cparams(("parallel", "parallel")),
        name="mlp",
    )(*args)


def _rope_tables(n_ctx, n_lat, dqk):
    n_freq = dqk // 4
    t = np.arange(n_lat)
    inv_freq = ROPE_BASE ** (-np.arange(n_freq, dtype=np.float32) / n_freq)
    ang = np.concatenate([(t // GRID_W)[:, None] * inv_freq, (t % GRID_W)[:, None] * inv_freq], axis=-1)
    reps = LANE // dqk
    cos = np.tile(np.concatenate([np.cos(ang), np.cos(ang)], axis=-1), (1, reps))
    sin = np.tile(np.concatenate([-np.sin(ang), np.sin(ang)], axis=-1), (1, reps))
    cos = np.concatenate([np.ones((n_ctx, LANE)), cos], axis=0)
    sin = np.concatenate([np.zeros((n_ctx, LANE)), sin], axis=0)
    return jnp.asarray(cos, F32), jnp.asarray(sin, F32)


def _pad_rows(a, start, total):
    return jnp.zeros((total, a.shape[-1]), a.dtype).at[start:start + a.shape[0]].set(a)


def kernel(x, c, ctx, c_ctx, ada_w, ada_b, norm1_g, norm2_g, w_in, w_out, hg_lb_logits, hg_norm_g, da_lam_q1, da_lam_k1, da_lam_q2, da_lam_k2, da_norm_g, rw_mu_prev, rw_mu_next, rw_w0, rw_w2, rw_a0, rw_a2, rw_g2, rw_k_k, rw_k_a, rw_r_k, rw_ln_w, rw_ln_b, ffn_w_gate, ffn_w_up, ffn_w_down, final_norm_g):
    b, n_lat, d = x.shape
    n_ctx = ctx.shape[1]
    depth = ada_w.shape[0]
    hg_w = hg_lb_logits.shape[-1]
    rw_w = rw_w0.shape[-1]
    da_w = w_out.shape[1] - hg_w - rw_w
    hg_cols, da_cols = 5 * hg_w, 3 * da_w
    rw_cols = w_in.shape[-1] - hg_cols - da_cols
    rw_pad = 3 * rw_w + 2 * LANE
    r_dec, r_icl, r_gate = rw_w2.shape[2], rw_a2.shape[2], rw_g2.shape[1]
    assert n_ctx % ROW_TILE == 0 and n_lat % ROW_TILE == 0 and n_lat % GRID_W == 0
    assert 2 * (r_dec + r_icl) == LANE and r_gate <= LANE and rw_cols == 3 * rw_w + LANE + r_gate
    n_ctx_tiles, n_ctx_chunks = n_ctx // ROW_TILE, n_ctx // CHUNK

    n_cond = -(-(b + _dense_batch(b)) // 8) * 8
    cond = jnp.zeros((n_cond, d), F32).at[:b].set(c).at[b:b + _dense_batch(b)].set(c_ctx)
    mods = _ada_mod(cond, ada_w, ada_b).reshape(depth, n_cond, 6, d)
    mods = jnp.pad(mods, ((0, 0), (0, 0), (0, 2), (0, 0)))

    p_lb = jax.nn.softmax(hg_lb_logits.astype(F32), axis=1)
    lower_bounds = jnp.cumsum(p_lb, axis=1) - p_lb[:, :1]
    cos_t, sin_t = _rope_tables(n_ctx, n_lat, da_w // (2 * N_HEADS))

    s = jnp.concatenate([ctx, x], axis=1)
    for l in range(depth):
        w_l = w_in[l]
        whg = w_l[:, :hg_cols].astype(BF16)
        wda = w_l[:, hg_cols:hg_cols + da_cols].astype(BF16)
        wrw = jnp.pad(w_l[:, hg_cols + da_cols:], ((0, 0), (0, rw_pad - rw_cols))).astype(BF16)
        u_hg, u_da, u_rw = _in_proj(s, mods[l], norm1_g[l][None], cos_t, sin_t, whg, wda, wrw, n_ctx_tiles)

        o_b = _hgrn_pass(u_hg, lower_bounds[1, l][None], n_ctx_chunks, True)
        o_hg = _hgrn_pass(u_hg, lower_bounds[0, l][None], n_ctx_chunks, False, other=o_b,
                          norm_g_row=jnp.tile(hg_norm_g[l], N_HEADS)[None])

        lam_init = 0.8 - 0.6 * math.exp(-0.3 * l)
        lam_rows = jnp.zeros((8, LANE), F32).at[0:4, :da_lam_q1.shape[-1]].set(
            jnp.stack([da_lam_q1[l], da_lam_k1[l], da_lam_q2[l], da_lam_k2[l]]))
        o_da = _attention(u_da, lam_rows, da_norm_g[l][None], lam_init, n_ctx)

        mu_pad = lambda m: jnp.pad(m, (0, rw_pad - rw_cols))[None]
        off_dec, off_icl = 0, 2 * r_dec
        rw_p = {
            "mu_prev": mu_pad(rw_mu_prev[l]), "mu_next": mu_pad(rw_mu_next[l]),
            "w0": rw_w0[:, l][:, None, :], "a0": rw_a0[:, l][:, None, :],
            "w2": jnp.stack([_pad_rows(rw_w2[i, l], off_dec + i * r_dec, LANE) for i in range(2)]),
            "a2": jnp.stack([_pad_rows(rw_a2[i, l], off_icl + i * r_icl, LANE) for i in range(2)]),
            "g2": _pad_rows(rw_g2[l], 0, LANE),
            "k_k": rw_k_k[l][None], "k_a": rw_k_a[l][None], "r_k": rw_r_k[l][None],
            "ln_w": rw_ln_w[l][None], "ln_b": rw_ln_b[l][None],
        }
        yb = _rwkv_pass(u_rw, rw_p, n_ctx_chunks, True)
        o_rw = _rwkv_pass(u_rw, rw_p, n_ctx_chunks, False, other=yb)

        wo = w_out[l].astype(BF16)
        s = _mlp(s, o_hg, o_da, o_rw, mods[l], norm2_g[l][None],
                 (wo[:hg_w], wo[hg_w:hg_w + da_w], wo[hg_w + da_w:]),
                 ffn_w_gate[l].astype(BF16), ffn_w_up[l].astype(BF16), ffn_w_down[l].astype(BF16),
                 n_ctx_tiles, final_g=final_norm_g[None] if l == depth - 1 else None)
    return s
```

```python
import functools
import math

import numpy as np
import jax
import jax.numpy as jnp
from jax import lax
from jax.experimental import pallas as pl
from jax.experimental.pallas import tpu as pltpu

F32 = jnp.float32
BF16 = jnp.bfloat16
HI = lax.Precision.HIGHEST

N_HEADS = 4
GRID_W = 64
ROPE_BASE = 10000.0
RMS_EPS = 1e-6
RW_GN_EPS = 64e-5

ROW_TILE = 256
DENSE_BATCH = 2
ATT_SUB = 64
CHUNK = 64
REC_BATCH = 8
LANE = 128
VMEM_LIMIT = 56 * 1024 * 1024


def _cparams(sem):
    return pltpu.CompilerParams(dimension_semantics=sem, vmem_limit_bytes=VMEM_LIMIT)


def _dot_dims(a, b, dims, prec):
    dg = lambda x, y, p=None: lax.dot_general(x, y, (dims, ((), ())), preferred_element_type=F32, precision=p)
    if prec == "bf16":
        return dg(a.astype(BF16), b.astype(BF16))
    if prec == "x3":
        a_hi, b_hi = a.astype(BF16), b.astype(BF16)
        a_lo = (a - a_hi.astype(F32)).astype(BF16)
        b_lo = (b - b_hi.astype(F32)).astype(BF16)
        return dg(a_hi, b_hi) + (dg(a_lo, b_hi) + dg(a_hi, b_lo))
    return dg(a, b, prec)


def _dot(a, b, prec=None):
    return _dot_dims(a, b, ((1,), (0,)), prec)


def _dot_nt(a, b, prec=None):
    return _dot_dims(a, b, ((1,), (1,)), prec)


def _dot_tn(a, b, prec=None):
    return _dot_dims(a, b, ((0,), (0,)), prec)


P_HG = "bf16"
P_ST = "bf16"


def _iota(shape, dim):
    return lax.broadcasted_iota(jnp.int32, shape, dim)


def _div(x, d):
    assert d & (d - 1) == 0
    return x >> (d.bit_length() - 1)


def _mod(x, d):
    assert d & (d - 1) == 0
    return x & (d - 1)


def _softplus(x):
    return jnp.maximum(x, 0.0) + jnp.log(1.0 + jnp.exp(-jnp.abs(x)))


def _same_head(w, dh):
    return _div(_iota((w, w), 0), dh) == _div(_iota((w, w), 1), dh)


def _expand_heads(x, dh):
    w = x.shape[-1]
    head = _div(_iota((1, w), 1), dh)
    return jnp.concatenate([jnp.where(head == h, x, 0.0) for h in range(w // dh)], axis=0)


def _collapse_heads(x, c):
    out = x[0:c]
    for h in range(1, x.shape[0] // c):
        out = out + x[h * c:(h + 1) * c]
    return out


def _split3(x):
    hi = x.astype(BF16)
    r = x - hi.astype(F32)
    mid = r.astype(BF16)
    lo = (r - mid.astype(F32)).astype(BF16)
    return hi, mid, lo


def _split2(x):
    hi = x.astype(BF16)
    return hi, (x - hi.astype(F32)).astype(BF16)


def _head_sums(xs, dh):
    w = xs[0].shape[-1]
    ones_bd = _same_head(w, dh).astype(BF16)
    out = _dot(jnp.concatenate([p for x in xs for p in _split3(x)], axis=0), ones_bd)
    res, off = [], 0
    for x in xs:
        r = x.shape[0]
        res.append(out[off:off + r] + (out[off + r:off + 2 * r] + out[off + 2 * r:off + 3 * r]))
        off += 3 * r
    return res


def _dot_sel(sel3, x):
    return _dot(sel3, jnp.concatenate(_split3(x), axis=0))


def _dot_x3(x, w_hi, w_lo):
    x_hi, x_lo = _split2(x)
    r = x.shape[0]
    t = _dot(jnp.concatenate([x_hi, x_lo], axis=0), w_hi)
    return t[0:r] + t[r:2 * r] + _dot(x_hi, w_lo)


def _sigmoid(x):
    return 1.0 / (1.0 + jnp.exp(-x))


def _ada_kernel(c_ref, w_ref, b_ref, o_ref):
    c = c_ref[...]
    sc = c * _sigmoid(c)
    o_ref[...] = _dot(sc, w_ref[...], HI) + b_ref[...]


def _ada_mod(cond, ada_w, ada_b):
    depth, d, d6 = ada_w.shape
    rows = cond.shape[0]
    tn = 1536
    return pl.pallas_call(
        _ada_kernel,
        grid=(depth, d6 // tn),
        in_specs=[pl.BlockSpec((rows, d), lambda l, j: (0, 0)),
                  pl.BlockSpec((None, d, tn), lambda l, j: (l, 0, j)),
                  pl.BlockSpec((None, 1, tn), lambda l, j: (l, 0, j))],
        out_specs=pl.BlockSpec((None, rows, tn), lambda l, j: (l, 0, j)),
        out_shape=jax.ShapeDtypeStruct((depth, rows, d6), F32),
        compiler_params=_cparams(("arbitrary", "arbitrary")),
        name="ada_mod",
    )(cond, ada_w, ada_b.reshape(depth, 1, d6))


def _rope_cols(t, cos, sin_signed, half):
    lane = _iota((1, t.shape[-1]), 1)
    first = _mod(lane, 2 * half) < half
    swapped = jnp.where(first, pltpu.roll(t, t.shape[-1] - half, 1), pltpu.roll(t, half, 1))
    return t * cos + swapped * sin_signed


def _in_proj_kernel(s_ref, sp_ref, sn_ref, mod_ref, g_ref, cos_ref, sin_ref, mup_ref, mun_ref,
                    whg_ref, wda_ref, wrw_ref, uhg_ref, uda_ref, urw_ref, *, da_width, q_scale, n_ctx_tiles):
    nbp, tm, _ = s_ref.shape

    def norm_mod(x, p):
        ms = jnp.mean(x * x, axis=-1, keepdims=True)
        y = x * lax.rsqrt(ms + RMS_EPS) * g_ref[...]
        return y * (1.0 + mod_ref[p, 1:2, :]) + mod_ref[p, 0:1, :]

    h = jnp.concatenate([norm_mod(s_ref[p], p).astype(BF16) for p in range(nbp)], axis=0)
    uhg = _dot(h, whg_ref[...])
    uda = _dot(h, wda_ref[...])
    halo = jnp.concatenate([norm_mod(ref[p], p) for p in range(nbp) for ref in (sp_ref, sn_ref)], axis=0)
    urw = _dot(jnp.concatenate([h, halo.astype(BF16)], axis=0), wrw_ref[...])
    urw_halo = urw[nbp * tm:]
    i = pl.program_id(1)
    seg_first = (i == 0) | (i == n_ctx_tiles)
    seg_last = (i == n_ctx_tiles - 1) | (i == pl.num_programs(1) - 1)
    row = _iota((tm, 1), 0)
    cos = cos_ref[...]
    sin = sin_ref[...]
    dqk = da_width // (2 * N_HEADS)
    for p in range(nbp):
        rows = slice(p * tm, (p + 1) * tm)
        uhg_ref[p] = uhg[rows]
        z = urw[rows]
        prev_row = jnp.where(seg_first, 0.0, urw_halo[16 * p + 7:16 * p + 8])
        next_row = jnp.where(seg_last, 0.0, urw_halo[16 * p + 8:16 * p + 9])
        z_prev = jnp.where(row == 0, prev_row, pltpu.roll(z, 1, 0))
        z_next = jnp.where(row == tm - 1, next_row, pltpu.roll(z, tm - 1, 0))
        urw_ref[p] = z + mup_ref[...] * (z_prev - z) + mun_ref[...] * (z_next - z)
        for j in range(2 * da_width // LANE):
            t = _rope_cols(uda[rows, j * LANE:(j + 1) * LANE], cos, sin, dqk // 2)
            if j < da_width // LANE:
                t = t * q_scale
            uda_ref[p, :, j * LANE:(j + 1) * LANE] = t.astype(BF16)
        uda_ref[p, :, 2 * da_width:] = uda[rows, 2 * da_width:].astype(BF16)


def _dense_batch(b):
    return math.gcd(b, DENSE_BATCH)


def _mod_spec(b, d, n_ctx_tiles, off=0):
    nbp = _dense_batch(b)
    return pl.BlockSpec((nbp, 8, d), lambda bi, i: (jnp.where(i + off < n_ctx_tiles, b // nbp, bi), 0, 0))


def _in_proj(s, mods_l, g, cos_t, sin_t, mu_prev, mu_next, whg, wda, wrw, n_ctx_tiles):
    b, n, d = s.shape
    tm = ROW_TILE
    nbp = _dense_batch(b)
    da_width = wda.shape[1] // 3
    dqk = da_width // (2 * N_HEADS)
    sub = tm // 8
    full = lambda a: pl.BlockSpec(a.shape, lambda bi, i: (0, 0))
    row = lambda w: pl.BlockSpec((nbp, tm, w), lambda bi, i: (bi, i, 0))
    kern = functools.partial(_in_proj_kernel, da_width=da_width, q_scale=dqk ** -0.5 * math.log2(math.e),
                             n_ctx_tiles=n_ctx_tiles)
    return pl.pallas_call(
        kern,
        grid=(b // nbp, n // tm),
        in_specs=[row(d),
                  pl.BlockSpec((nbp, 8, d), lambda bi, i: (bi, jnp.maximum(i * sub - 1, 0), 0)),
                  pl.BlockSpec((nbp, 8, d), lambda bi, i: (bi, jnp.minimum((i + 1) * sub, n // 8 - 1), 0)),
                  _mod_spec(b, d, n_ctx_tiles), full(g),
                  pl.BlockSpec((tm, LANE), lambda bi, i: (i, 0)),
                  pl.BlockSpec((tm, LANE), lambda bi, i: (i, 0)),
                  full(mu_prev), full(mu_next), full(whg), full(wda), full(wrw)],
        out_specs=[row(whg.shape[1]), row(wda.shape[1]), row(wrw.shape[1])],
        out_shape=[jax.ShapeDtypeStruct((b, n, whg.shape[1]), F32),
                   jax.ShapeDtypeStruct((b, n, wda.shape[1]), BF16),
                   jax.ShapeDtypeStruct((b, n, wrw.shape[1]), F32)],
        compiler_params=_cparams(("parallel", "parallel")),
        name="in_proj",
    )(s, s, s, mods_l, g, cos_t, sin_t, mu_prev, mu_next, whg, wda, wrw)


def _chunk_of_step(j, n_chunks, n_ctx_chunks, rev):
    if not rev:
        return j
    return jnp.where(j < n_ctx_chunks, n_ctx_chunks - 1 - j, n_chunks - 1 - (j - n_ctx_chunks))


def _hgrn_levels(c):
    return [1 << i for i in range(int(math.log2(c)))]


def _hgrn_const(c, rev):
    t = np.arange(c)[:, None]
    s = np.arange(c)[None, :]
    tri = (s >= t) if rev else (s <= t)
    mats = [tri.astype(np.float32)]
    for m in _hgrn_levels(c):
        p = (t // (2 * m)) * (2 * m) + (m if rev else m - 1)
        ref = (s >= p) if rev else (s <= p)
        mats.append(tri.astype(np.float32) - ref.astype(np.float32))
    return np.concatenate(mats, axis=0)


def _hgrn_kernel(*refs, rev, finish, dh):
    if finish:
        const_ref, q_ref, i_ref, f_ref, lb_ref, g_ref, other_ref, ng_ref, out_ref, st_ref = refs
    else:
        const_ref, q_ref, i_ref, f_ref, lb_ref, out_ref, st_ref = refs
    nb, c, w = q_ref.shape

    @pl.when(pl.program_id(1) == 0)
    def _():
        st_ref[...] = jnp.zeros_like(st_ref)

    nh = w // dh
    t_idx = _iota((c, 1), 0)
    s_idx = _mod(_iota((1, nh * c), 1), c)
    bd = _same_head(w, dh)
    lb = lb_ref[...]
    rows = lambda ref: ref[...].reshape(nb * c, w)
    q_all = rows(q_ref)
    v_all = rows(i_ref)
    f_all = lb + (1.0 - lb) * _sigmoid(rows(f_ref))
    kx_all = 1.0 - f_all
    logf_all = jnp.log(f_all)
    (diag_all,) = _head_sums([q_all * kx_all], dh)
    bf = lambda t: t.astype(BF16)
    cs = []
    for bb in range(nb):
        sl = slice(bb * c, (bb + 1) * c)
        seg = _dot_sel(const_ref[...], logf_all[sl])
        bcum = seg[0:c]
        cs.append(dict(bb=bb, q=q_all[sl], v=v_all[sl], kx=kx_all[sl], diag=diag_all[sl], seg=seg, bcum=bcum,
                       bend=bcum[0:1] if rev else bcum[c - 1:c], att=jnp.zeros((c, nh * c), F32)))
    for li, m in enumerate(_hgrn_levels(c)):
        t_up = (t_idx & m) != 0
        s_up = (s_idx & m) != 0
        t_isq = jnp.logical_not(t_up) if rev else t_up
        s_isk = s_up if rev else jnp.logical_not(s_up)
        valid = (_div(t_idx, 2 * m) == _div(s_idx, 2 * m)) & t_isq & s_isk
        for s in cs:
            e = jnp.exp(-jnp.abs(s["seg"][(li + 1) * c:(li + 2) * c]))
            xm = bf(jnp.where(t_isq, s["q"], s["kx"]) * e)
            sc = _dot_nt(xm, _expand_heads(xm, dh))
            s["att"] = s["att"] + jnp.where(valid, sc, 0.0)
    outs = []
    for s in cs:
        q, v, kx, bcum, bend = s["q"], s["v"], s["kx"], s["bcum"], s["bend"]
        o = _dot(bf(s["att"]), _expand_heads(bf(v), dh))
        o = o + s["diag"] * v
        st = st_ref[s["bb"]]
        o = o + _dot_nt(q * jnp.exp(bcum), st, P_HG)
        upd = _dot_tn(v, kx * jnp.exp(bend - bcum), P_HG)
        st_ref[s["bb"]] = st * jnp.exp(bend) + jnp.where(bd, upd, 0.0)
        outs.append(o)
    o_all = jnp.concatenate(outs, axis=0)
    if finish:
        o_all = o_all + rows(other_ref)
        (ms,) = _head_sums([o_all * o_all], dh)
        g = rows(g_ref)
        o_all = o_all * lax.rsqrt(ms * (1.0 / dh) + RMS_EPS) * ng_ref[...] * (g * _sigmoid(g))
    out_ref[...] = o_all.reshape(nb, c, w).astype(out_ref.dtype)


def _hgrn_pass(u_hg, lb_row, n_ctx_chunks, rev, other=None, norm_g_row=None):
    b, n, _ = u_hg.shape
    c = CHUNK
    w = lb_row.shape[-1]
    dh = w // N_HEADS
    nc = n // c
    finish = other is not None
    const = jnp.asarray(np.tile(_hgrn_const(c, rev), (1, 3)), BF16)
    chunk = lambda j: _chunk_of_step(j, nc, n_ctx_chunks, rev)
    nb = math.gcd(b, REC_BATCH)
    col = lambda k: pl.BlockSpec((nb, c, w), lambda bi, j: (bi, chunk(j), k))
    vec = pl.BlockSpec((1, w), lambda bi, j: (0, 0))
    in_specs = [pl.BlockSpec(const.shape, lambda bi, j: (0, 0)), col(0), col(1), col(3 if rev else 2), vec]
    args = [const, u_hg, u_hg, u_hg, lb_row]
    if finish:
        in_specs += [col(4), col(0), vec]
        args += [u_hg, other, norm_g_row]
    return pl.pallas_call(
        functools.partial(_hgrn_kernel, rev=rev, finish=finish, dh=dh),
        grid=(b // nb, nc),
        in_specs=in_specs,
        out_specs=col(0),
        out_shape=jax.ShapeDtypeStruct((b, n, w), BF16 if finish else F32),
        scratch_shapes=[pltpu.VMEM((nb, w, w), F32)],
        compiler_params=_cparams(("parallel", "arbitrary")),
        name="hgrn_bwd" if rev else "hgrn_fwd",
    )(*args)


def _rwkv_kernel(*refs, rev, finish, dh):
    if finish:
        (tri_ref, z_ref, w0_ref, w2_ref, a0_ref, a2_ref, g2_ref,
         kk_ref, ka_ref, rk_ref, oy_ref, ob_ref, lnw_ref, lnb_ref, out_ref, st_ref) = refs
    else:
        (tri_ref, z_ref, w0_ref, w2_ref, a0_ref, a2_ref, g2_ref,
         kk_ref, ka_ref, rk_ref, y_ref, bonus_ref, st_ref) = refs
    nb, c, _ = z_ref.shape
    w = w0_ref.shape[-1]

    @pl.when(pl.program_id(1) == 0)
    def _():
        st_ref[...] = jnp.zeros_like(st_ref)

    row = _iota((c, 1), 0)
    nr = (w // dh) * c
    si = _mod(_iota((1, nr), 1), c)
    strict = (si > row) if rev else (si < row)
    incl = (si >= row) if rev else (si <= row)
    eye_cat = (si == row).astype(F32)
    blk = _same_head(nr, c)
    bd = _same_head(w, dh)
    eye_w = (_iota((w, w), 0) == _iota((w, w), 1)).astype(F32)

    def prelude(group):
        u = jnp.concatenate([z_ref[bb] for bb in group], axis=0)
        r, k, v = u[:, 0:w], u[:, w:2 * w], u[:, 2 * w:3 * w]
        lora = u[:, 3 * w:3 * w + LANE]
        gd = u[:, 3 * w + LANE:3 * w + 2 * LANE]
        w_log = -_softplus(-(w0_ref[...] + _dot_x3(jnp.tanh(lora), w2_ref[0], w2_ref[1]))) - 0.5
        lw = -jnp.exp(w_log)
        a_lr = _sigmoid(a0_ref[...] + _dot_x3(lora, a2_ref[0], a2_ref[1]))
        kk = k * kk_ref[...]
        kd = k * (1.0 + (a_lr - 1.0) * ka_ref[...])
        kk_sq, bonus_dot = _head_sums([kk * kk, r * kd * rk_ref[...]], dh)
        kk = kk * lax.rsqrt(jnp.maximum(kk_sq, 1e-24))
        cols = dict(r=r, v=v, kd=kd, lw=lw, a=-kk, b=kk * a_lr, bonus=bonus_dot * v, gd=gd)
        return [{name: t[i * c:(i + 1) * c] for name, t in cols.items()} for i in range(len(group))]

    chains = list(range(nb))
    pre = prelude(chains)

    bf = lambda t: t.astype(BF16)
    ex = lambda t: _expand_heads(bf(t), dh)
    st = []
    for d in pre:
        v, kd, lw = d["v"], d["kd"], d["lw"]
        cum = _dot_sel(tri_ref[...], lw)
        tot = cum[0:1] if rev else cum[c - 1:c]
        inv_gam = jnp.exp(-cum)
        rest = jnp.exp(tot - cum)
        r_t = d["r"] * jnp.exp(cum)
        a_t = d["a"] * jnp.exp(cum - lw)
        ea, eb, ek, ev = ex(a_t), ex(d["b"] * inv_gam), ex(kd * inv_gam), ex(v)
        sc = _dot_nt(jnp.concatenate([bf(a_t), bf(r_t)], axis=0), jnp.concatenate([eb, ek], axis=0))
        st.append(dict(
            v=v, r_t=r_t, tot=tot, ea=ea, ev=ev, b_h=d["b"] * rest, k_h=kd * rest,
            a_ab=jnp.where(strict, sc[0:c, 0:nr], 0.0), a_ak=jnp.where(strict, sc[0:c, nr:2 * nr], 0.0),
            a_rb=jnp.where(incl, sc[c:2 * c, 0:nr], 0.0), a_rk=jnp.where(incl, sc[c:2 * c, nr:2 * nr], 0.0)))

    bd_cat = lambda t: jnp.where(blk, jnp.concatenate([bf(t)] * (nr // c), axis=0), 0.0)
    for s in st:
        s["pw"] = _dot(bf(s["a_ab"]), bd_cat(s["a_ab"]))
        s["t_cat"] = eye_cat + s["a_ab"]
    n_iter = int(math.log2(c)) - 1
    for it in range(n_iter):
        for s in st:
            pw = bf(s["pw"])
            if it + 1 < n_iter:
                both = _dot(pw, jnp.concatenate([bd_cat(s["t_cat"]), bd_cat(s["pw"])], axis=1))
                s["t_cat"], s["pw"] = s["t_cat"] + both[:, 0:nr], both[:, nr:2 * nr]
            else:
                s["t_cat"] = s["t_cat"] + _dot(pw, bd_cat(s["t_cat"]))

    for s in st:
        s["t_cat"] = bf(s["t_cat"])
        s["x_c"] = _dot(bf(s["a_ak"]), s["ev"])
    for s in st:
        s["w1c"] = _dot(s["t_cat"], s["ea"])
        s["w2c"] = _dot(s["t_cat"], ex(s["x_c"]))
    for s in st:
        a_rb = bf(s["a_rb"])
        s["p_mat"] = s["r_t"] + _dot(a_rb, ex(s["w1c"]))
        s["y0"] = _dot(jnp.concatenate([a_rb, bf(s["a_rk"])], axis=1),
                       jnp.concatenate([ex(s["w2c"]), s["ev"]], axis=0))
    ys = []
    for bb, s in zip(chains, st):
        lhs = jnp.concatenate([s["w1c"], s["p_mat"], eye_w * jnp.exp(s["tot"])], axis=0)
        big = _dot(lhs, st_ref[bb], P_ST)
        ys.append(big[c:2 * c] + s["y0"])
        upd = _dot_tn(jnp.concatenate([s["b_h"], s["k_h"]], axis=0),
                      jnp.concatenate([big[0:c] + s["w2c"], s["v"]], axis=0), P_ST)
        st_ref[bb] = big[2 * c:] + jnp.where(bd, upd, 0.0)

    y = jnp.concatenate(ys, axis=0)
    bonus_all = jnp.concatenate([d["bonus"] for d in pre], axis=0)
    if finish:
        gd = jnp.concatenate([d["gd"] for d in pre], axis=0)
        y = y + oy_ref[...].reshape(nb * c, w)
        (mean,) = _head_sums([y], dh)
        yc = y - mean * (1.0 / dh)
        (var,) = _head_sums([yc * yc], dh)
        yn = yc * lax.rsqrt(var * (1.0 / dh) + RW_GN_EPS) * lnw_ref[...] + lnb_ref[...]
        gate = _dot_x3(_sigmoid(gd), g2_ref[0], g2_ref[1])
        out = (yn + bonus_all + ob_ref[...].reshape(nb * c, w)) * gate
        out_ref[...] = out.reshape(nb, c, w).astype(out_ref.dtype)
    else:
        y_ref[...] = y.reshape(nb, c, w)
        bonus_ref[...] = bonus_all.reshape(nb, c, w)


def _rwkv_pass(u_rw, p, n_ctx_chunks, rev, other=None):
    b, n, wu = u_rw.shape
    c = CHUNK
    w = p["w0"].shape[-1]
    dh = w // N_HEADS
    nc = n // c
    finish = other is not None
    t = np.arange(c)
    tri = (t[None, :] >= t[:, None]) if rev else (t[None, :] <= t[:, None])
    tri = jnp.asarray(np.tile(tri.astype(np.float32), (1, 3)), BF16)
    hi_lo = lambda m: jnp.stack(_split2(m))
    d = 1 if rev else 0
    chunk = lambda j: _chunk_of_step(j, nc, n_ctx_chunks, rev)
    nb = math.gcd(b, REC_BATCH)
    const = lambda a: pl.BlockSpec(a.shape, lambda bi, j: (0,) * a.ndim)
    col = pl.BlockSpec((nb, c, w), lambda bi, j: (bi, chunk(j), 0))
    args = [tri, u_rw, p["w0"][d], hi_lo(p["w2"][d]), p["a0"][d],
            hi_lo(p["a2"][d]), hi_lo(p["g2"]), p["k_k"], p["k_a"], p["r_k"]]
    in_specs = [const(tri), pl.BlockSpec((nb, c, wu), lambda bi, j: (bi, chunk(j), 0))]
    in_specs += [const(a) for a in args[2:]]
    if finish:
        args += [other[0], other[1], p["ln_w"], p["ln_b"]]
        in_specs += [col, col, const(p["ln_w"]), const(p["ln_b"])]
        out_specs, out_shape = col, jax.ShapeDtypeStruct((b, n, w), BF16)
    else:
        out_specs = [col, col]
        out_shape = [jax.ShapeDtypeStruct((b, n, w), F32)] * 2
    return pl.pallas_call(
        functools.partial(_rwkv_kernel, rev=rev, finish=finish, dh=dh),
        grid=(b // nb, nc),
        in_specs=in_specs,
        out_specs=out_specs,
        out_shape=out_shape,
        scratch_shapes=[pltpu.VMEM((nb, w, w), F32)],
        compiler_params=_cparams(("parallel", "arbitrary")),
        name="rwkv_bwd" if rev else "rwkv_fwd",
    )(*args)


def _attn_lam(lam_ref, lam_init):
    lp = lam_ref[...]
    return (jnp.exp(jnp.sum(lp[0:1] * lp[1:2], axis=-1, keepdims=True))
            - jnp.exp(jnp.sum(lp[2:3] * lp[3:4], axis=-1, keepdims=True)) + lam_init)


def _stack_maps(q):
    half = q.shape[-1] // 2
    lane = _iota((1, q.shape[-1]), 1)
    return jnp.concatenate([jnp.where(lane < half, q, 0), jnp.where(lane >= half, q, 0)], axis=0)


def _attn_finish(o, ng, lam_init):
    ms = jnp.mean(o * o, axis=-1, keepdims=True)
    return o * lax.rsqrt(ms + RMS_EPS) * ng * (1.0 - lam_init)


def _attn_kernel(lam_ref, q_ref, k_ref, v_ref, ng_ref, o_ref, *, lam_init, n_ctx, n_ctx_tiles):
    lam = _attn_lam(lam_ref, lam_init)

    def attend(k, v):
        ts = ATT_SUB
        subs = range(q_ref.shape[0] // ts)
        s = [_dot_nt(_stack_maps(q_ref[i * ts:(i + 1) * ts, :]), k) for i in subs]
        for i, x in zip(subs, s):
            p = jnp.exp2(x - jnp.max(x, axis=-1, keepdims=True))
            den = jnp.sum(p, axis=-1, keepdims=True)
            pc = p[0:ts] - p[ts:2 * ts] * (lam * den[0:ts] / den[ts:2 * ts])
            o = _attn_finish(_dot(pc.astype(BF16), v) / den[0:ts], ng_ref[...], lam_init)
            o_ref[i * ts:(i + 1) * ts, :] = o.astype(o_ref.dtype)

    is_ctx = pl.program_id(2) < n_ctx_tiles

    @pl.when(is_ctx)
    def _():
        attend(k_ref[0:n_ctx, :], v_ref[0:n_ctx, :])

    @pl.when(jnp.logical_not(is_ctx))
    def _():
        attend(k_ref[...], v_ref[...])


def _attention(u_da, lam_rows, norm_g_row, lam_init, n_ctx):
    b, n, w3 = u_da.shape
    wd = w3 // 3
    dv = wd // N_HEADS
    tq = ROW_TILE
    kern = functools.partial(_attn_kernel, lam_init=lam_init, n_ctx=n_ctx, n_ctx_tiles=n_ctx // tq)
    return pl.pallas_call(
        kern,
        grid=(b, N_HEADS, n // tq),
        in_specs=[pl.BlockSpec(lam_rows.shape, lambda bi, h, i: (0, 0)),
                  pl.BlockSpec((None, tq, dv), lambda bi, h, i: (bi, i, h)),
                  pl.BlockSpec((None, n, dv), lambda bi, h, i: (bi, 0, N_HEADS + h)),
                  pl.BlockSpec((None, n, dv), lambda bi, h, i: (bi, 0, 2 * N_HEADS + h)),
                  pl.BlockSpec((1, dv), lambda bi, h, i: (0, 0))],
        out_specs=pl.BlockSpec((None, tq, dv), lambda bi, h, i: (bi, i, h)),
        out_shape=jax.ShapeDtypeStruct((b, n, wd), BF16),
        compiler_params=_cparams(("parallel", "parallel", "arbitrary")),
        name="diff_attn",
    )(lam_rows, u_da, u_da, u_da, norm_g_row)


def _mlp_kernel(*refs, final):
    if final:
        (s_ref, hg_ref, da_ref, rw_ref, mod_ref, g_ref, whg_ref, wda_ref, wrw_ref, wg_ref, wu_ref, wd_ref,
         fg_ref, o_ref) = refs
    else:
        (s_ref, hg_ref, da_ref, rw_ref, mod_ref, g_ref, whg_ref, wda_ref, wrw_ref, wg_ref, wu_ref, wd_ref,
         o_ref) = refs
    nbp, tm, d = s_ref.shape
    stack = lambda ref: ref[...].reshape(nbp * tm, ref.shape[-1])
    mix = _dot(stack(hg_ref), whg_ref[...]) + _dot(stack(da_ref), wda_ref[...]) + _dot(stack(rw_ref), wrw_ref[...])
    xs, hs = [], []
    for p in range(nbp):
        x = s_ref[p] + mod_ref[p, 2:3, :] * mix[p * tm:(p + 1) * tm]
        ms = jnp.mean(x * x, axis=-1, keepdims=True)
        y = x * lax.rsqrt(ms + RMS_EPS) * g_ref[...]
        xs.append(x)
        hs.append((y * (1.0 + mod_ref[p, 4:5, :]) + mod_ref[p, 3:4, :]).astype(BF16))
    h = jnp.concatenate(hs, axis=0)
    gate = _dot(h, wg_ref[...])
    up = _dot(h, wu_ref[...])
    down = _dot((gate * _sigmoid(gate) * up).astype(BF16), wd_ref[...])
    for p in range(nbp):
        x = xs[p] + mod_ref[p, 5:6, :] * down[p * tm:(p + 1) * tm]
        if final:
            ms = jnp.mean(x * x, axis=-1, keepdims=True)
            x = x * lax.rsqrt(ms + RMS_EPS) * fg_ref[...]
        o_ref[p] = x


def _mlp(s, o_hg, o_da, o_rw, mods_l, g, w_out_parts, wg, wu, wd, n_ctx_tiles, final_g=None):
    b, n, d = s.shape
    tm = ROW_TILE
    final = final_g is not None
    off = n_ctx_tiles if final else 0
    nbp = _dense_batch(b)
    resident = lambda a: pl.BlockSpec(a.shape, lambda bi, i: (0, 0), pipeline_mode=pl.Buffered(1))
    row = lambda a: pl.BlockSpec((nbp, tm, a.shape[-1]), lambda bi, i: (bi, i + off, 0))
    weights = [*w_out_parts, wg, wu, wd]
    in_specs = [row(s), row(o_hg), row(o_da), row(o_rw), _mod_spec(b, d, n_ctx_tiles, off),
                pl.BlockSpec(g.shape, lambda bi, i: (0, 0))] + [resident(a) for a in weights]
    args = [s, o_hg, o_da, o_rw, mods_l, g, *weights]
    if final:
        in_specs.append(pl.BlockSpec((1, d), lambda bi, i: (0, 0)))
        args.append(final_g)
    return pl.pallas_call(
        functools.partial(_mlp_kernel, final=final),
        grid=(b // nbp, n // tm - off),
        in_specs=in_specs,
        out_specs=pl.BlockSpec((nbp, tm, d), lambda bi, i: (bi, i, 0)),
        out_shape=jax.ShapeDtypeStruct((b, n - off * tm, d), F32),
        compiler_params=_cparams(("parallel", "parallel")),
        name="mlp",
    )(*args)


def _rope_tables(n_ctx, n_lat, dqk):
    n_freq = dqk // 4
    t = np.arange(n_lat)
    inv_freq = ROPE_BASE ** (-np.arange(n_freq, dtype=np.float32) / n_freq)
    ang = np.concatenate([(t // GRID_W)[:, None] * inv_freq, (t % GRID_W)[:, None] * inv_freq], axis=-1)
    reps = LANE // dqk
    cos = np.tile(np.concatenate([np.cos(ang), np.cos(ang)], axis=-1), (1, reps))
    sin = np.tile(np.concatenate([-np.sin(ang), np.sin(ang)], axis=-1), (1, reps))
    cos = np.concatenate([np.ones((n_ctx, LANE)), cos], axis=0)
    sin = np.concatenate([np.zeros((n_ctx, LANE)), sin], axis=0)
    return jnp.asarray(cos, F32), jnp.asarray(sin, F32)


def _pad_rows(a, start, total):
    return jnp.zeros((total, a.shape[-1]), a.dtype).at[start:start + a.shape[0]].set(a)


def kernel(x, c, ctx, c_ctx, ada_w, ada_b, norm1_g, norm2_g, w_in, w_out, hg_lb_logits, hg_norm_g, da_lam_q1, da_lam_k1, da_lam_q2, da_lam_k2, da_norm_g, rw_mu_prev, rw_mu_next, rw_w0, rw_w2, rw_a0, rw_a2, rw_g2, rw_k_k, rw_k_a, rw_r_k, rw_ln_w, rw_ln_b, ffn_w_gate, ffn_w_up, ffn_w_down, final_norm_g):
    b, n_lat, d = x.shape
    n_ctx = ctx.shape[1]
    depth = ada_w.shape[0]
    hg_w = hg_lb_logits.shape[-1]
    rw_w = rw_w0.shape[-1]
    da_w = w_out.shape[1] - hg_w - rw_w
    hg_cols, da_cols = 5 * hg_w, 3 * da_w
    rw_cols = w_in.shape[-1] - hg_cols - da_cols
    rw_pad = 3 * rw_w + 2 * LANE
    r_dec, r_icl, r_gate = rw_w2.shape[2], rw_a2.shape[2], rw_g2.shape[1]
    assert n_ctx % ROW_TILE == 0 and n_lat % ROW_TILE == 0 and n_lat % GRID_W == 0
    assert 2 * (r_dec + r_icl) == LANE and r_gate <= LANE and rw_cols == 3 * rw_w + LANE + r_gate
    n_ctx_tiles, n_ctx_chunks = n_ctx // ROW_TILE, n_ctx // CHUNK

    n_cond = -(-(b + _dense_batch(b)) // 8) * 8
    cond = jnp.zeros((n_cond, d), F32).at[:b].set(c).at[b:b + _dense_batch(b)].set(c_ctx)
    mods = _ada_mod(cond, ada_w, ada_b).reshape(depth, n_cond, 6, d)
    mods = jnp.pad(mods, ((0, 0), (0, 0), (0, 2), (0, 0)))

    p_lb = jax.nn.softmax(hg_lb_logits.astype(F32), axis=1)
    lower_bounds = jnp.cumsum(p_lb, axis=1) - p_lb[:, :1]
    cos_t, sin_t = _rope_tables(n_ctx, n_lat, da_w // (2 * N_HEADS))

    s = jnp.concatenate([ctx, x], axis=1)
    for l in range(depth):
        w_l = w_in[l]
        whg = w_l[:, :hg_cols].astype(BF16)
        wda = w_l[:, hg_cols:hg_cols + da_cols].astype(BF16)
        wrw = jnp.pad(w_l[:, hg_cols + da_cols:], ((0, 0), (0, rw_pad - rw_cols))).astype(BF16)
        mu_pad = lambda m: jnp.pad(m, (0, rw_pad - rw_cols))[None]
        u_hg, u_da, u_rw = _in_proj(s, mods[l], norm1_g[l][None], cos_t, sin_t, mu_pad(rw_mu_prev[l]),
                                    mu_pad(rw_mu_next[l]), whg, wda, wrw, n_ctx_tiles)

        o_b = _hgrn_pass(u_hg, lower_bounds[1, l][None], n_ctx_chunks, True)
        o_hg = _hgrn_pass(u_hg, lower_bounds[0, l][None], n_ctx_chunks, False, other=o_b,
                          norm_g_row=jnp.tile(hg_norm_g[l], N_HEADS)[None])

        lam_init = 0.8 - 0.6 * math.exp(-0.3 * l)
        lam_rows = jnp.zeros((8, LANE), F32).at[0:4, :da_lam_q1.shape[-1]].set(
            jnp.stack([da_lam_q1[l], da_lam_k1[l], da_lam_q2[l], da_lam_k2[l]]))
        o_da = _attention(u_da, lam_rows, da_norm_g[l][None], lam_init, n_ctx)

        off_dec, off_icl = 0, 2 * r_dec
        rw_p = {
            "w0": rw_w0[:, l][:, None, :], "a0": rw_a0[:, l][:, None, :],
            "w2": jnp.stack([_pad_rows(rw_w2[i, l], off_dec + i * r_dec, LANE) for i in range(2)]),
            "a2": jnp.stack([_pad_rows(rw_a2[i, l], off_icl + i * r_icl, LANE) for i in range(2)]),
            "g2": _pad_rows(rw_g2[l], 0, LANE),
            "k_k": rw_k_k[l][None], "k_a": rw_k_a[l][None], "r_k": rw_r_k[l][None],
            "ln_w": rw_ln_w[l][None], "ln_b": rw_ln_b[l][None],
        }
        yb = _rwkv_pass(u_rw, rw_p, n_ctx_chunks, True)
        o_rw = _rwkv_pass(u_rw, rw_p, n_ctx_chunks, False, other=yb)

        wo = w_out[l].astype(BF16)
        s = _mlp(s, o_hg, o_da, o_rw, mods[l], norm2_g[l][None],
                 (wo[:hg_w], wo[hg_w:hg_w + da_w], wo[hg_w + da_w:]),
                 ffn_w_gate[l].astype(BF16), ffn_w_up[l].astype(BF16), ffn_w_down[l].astype(BF16),
                 n_ctx_tiles, final_g=final_norm_g[None] if l == depth - 1 else None)
    return s
```

```python
import functools
import math

import numpy as np
import jax
import jax.numpy as jnp
from jax import lax
from jax.experimental import pallas as pl
from jax.experimental.pallas import tpu as pltpu

F32 = jnp.float32
BF16 = jnp.bfloat16
HI = lax.Precision.HIGHEST

N_HEADS = 4
GRID_W = 64
ROPE_BASE = 10000.0
RMS_EPS = 1e-6
RW_GN_EPS = 64e-5

ROW_TILE = 256
DENSE_BATCH = 2
ATT_SUB = 64
CHUNK = 64
REC_BATCH = 8
LANE = 128
VMEM_LIMIT = 56 * 1024 * 1024


def _cparams(sem):
    return pltpu.CompilerParams(dimension_semantics=sem, vmem_limit_bytes=VMEM_LIMIT)


def _dot_dims(a, b, dims, prec):
    dg = lambda x, y, p=None: lax.dot_general(x, y, (dims, ((), ())), preferred_element_type=F32, precision=p)
    if prec == "bf16":
        return dg(a.astype(BF16), b.astype(BF16))
    if prec == "x3":
        a_hi, b_hi = a.astype(BF16), b.astype(BF16)
        a_lo = (a - a_hi.astype(F32)).astype(BF16)
        b_lo = (b - b_hi.astype(F32)).astype(BF16)
        return dg(a_hi, b_hi) + (dg(a_lo, b_hi) + dg(a_hi, b_lo))
    return dg(a, b, prec)


def _dot(a, b, prec=None):
    return _dot_dims(a, b, ((1,), (0,)), prec)


def _dot_nt(a, b, prec=None):
    return _dot_dims(a, b, ((1,), (1,)), prec)


def _dot_tn(a, b, prec=None):
    return _dot_dims(a, b, ((0,), (0,)), prec)


P_HG = "bf16"
P_ST = "bf16"


def _iota(shape, dim):
    return lax.broadcasted_iota(jnp.int32, shape, dim)


def _div(x, d):
    assert d & (d - 1) == 0
    return x >> (d.bit_length() - 1)


def _mod(x, d):
    assert d & (d - 1) == 0
    return x & (d - 1)


def _softplus(x):
    return jnp.maximum(x, 0.0) + jnp.log(1.0 + jnp.exp(-jnp.abs(x)))


def _same_head(w, dh):
    return _div(_iota((w, w), 0), dh) == _div(_iota((w, w), 1), dh)


def _expand_heads(x, dh):
    w = x.shape[-1]
    head = _div(_iota((1, w), 1), dh)
    return jnp.concatenate([jnp.where(head == h, x, 0.0) for h in range(w // dh)], axis=0)


def _collapse_heads(x, c):
    out = x[0:c]
    for h in range(1, x.shape[0] // c):
        out = out + x[h * c:(h + 1) * c]
    return out


def _split3(x):
    hi = x.astype(BF16)
    r = x - hi.astype(F32)
    mid = r.astype(BF16)
    lo = (r - mid.astype(F32)).astype(BF16)
    return hi, mid, lo


def _split2(x):
    hi = x.astype(BF16)
    return hi, (x - hi.astype(F32)).astype(BF16)


def _head_sums(xs, dh):
    w = xs[0].shape[-1]
    ones_bd = _same_head(w, dh).astype(BF16)
    out = _dot(jnp.concatenate([p for x in xs for p in _split3(x)], axis=0), ones_bd)
    res, off = [], 0
    for x in xs:
        r = x.shape[0]
        res.append(out[off:off + r] + (out[off + r:off + 2 * r] + out[off + 2 * r:off + 3 * r]))
        off += 3 * r
    return res


def _dot_sel(sel3, x):
    return _dot(sel3, jnp.concatenate(_split3(x), axis=0))


def _dot_x3(x, w_hi, w_lo):
    x_hi, x_lo = _split2(x)
    r = x.shape[0]
    t = _dot(jnp.concatenate([x_hi, x_lo], axis=0), w_hi)
    return t[0:r] + t[r:2 * r] + _dot(x_hi, w_lo)


def _sigmoid(x):
    return 1.0 / (1.0 + jnp.exp(-x))


def _ada_kernel(c_ref, w_ref, b_ref, o_ref):
    c = c_ref[...]
    sc = c * _sigmoid(c)
    o_ref[...] = _dot(sc, w_ref[...], HI) + b_ref[...]


def _ada_mod(cond, ada_w, ada_b):
    depth, d, d6 = ada_w.shape
    rows = cond.shape[0]
    tn = 1536
    return pl.pallas_call(
        _ada_kernel,
        grid=(depth, d6 // tn),
        in_specs=[pl.BlockSpec((rows, d), lambda l, j: (0, 0)),
                  pl.BlockSpec((None, d, tn), lambda l, j: (l, 0, j)),
                  pl.BlockSpec((None, 1, tn), lambda l, j: (l, 0, j))],
        out_specs=pl.BlockSpec((None, rows, tn), lambda l, j: (l, 0, j)),
        out_shape=jax.ShapeDtypeStruct((depth, rows, d6), F32),
        compiler_params=_cparams(("arbitrary", "arbitrary")),
        name="ada_mod",
    )(cond, ada_w, ada_b.reshape(depth, 1, d6))


def _rope_cols(t, cos, sin_signed, half):
    lane = _iota((1, t.shape[-1]), 1)
    first = _mod(lane, 2 * half) < half
    swapped = jnp.where(first, pltpu.roll(t, t.shape[-1] - half, 1), pltpu.roll(t, half, 1))
    return t * cos + swapped * sin_signed


def _in_proj_kernel(s_ref, sp_ref, sn_ref, mod_ref, g_ref, cos_ref, sin_ref, mup_ref, mun_ref,
                    whg_ref, wda_ref, wrw_ref, uhg_ref, uda_ref, ukt_ref, urw_ref, *, da_width, q_scale,
                    n_ctx_tiles):
    nbp, tm, _ = s_ref.shape

    def norm_mod(x, p):
        ms = jnp.mean(x * x, axis=-1, keepdims=True)
        y = x * lax.rsqrt(ms + RMS_EPS) * g_ref[...]
        return y * (1.0 + mod_ref[p, 1:2, :]) + mod_ref[p, 0:1, :]

    h = jnp.concatenate([norm_mod(s_ref[p], p).astype(BF16) for p in range(nbp)], axis=0)
    uhg = _dot(h, whg_ref[...])
    uda = _dot(h, wda_ref[...])
    halo = jnp.concatenate([norm_mod(ref[p], p) for p in range(nbp) for ref in (sp_ref, sn_ref)], axis=0)
    urw = _dot(jnp.concatenate([h, halo.astype(BF16)], axis=0), wrw_ref[...])
    urw_halo = urw[nbp * tm:]
    i = pl.program_id(1)
    seg_first = (i == 0) | (i == n_ctx_tiles)
    seg_last = (i == n_ctx_tiles - 1) | (i == pl.num_programs(1) - 1)
    row = _iota((tm, 1), 0)
    cos = cos_ref[...]
    sin = sin_ref[...]
    dqk = da_width // (2 * N_HEADS)
    for p in range(nbp):
        rows = slice(p * tm, (p + 1) * tm)
        uhg_ref[p] = uhg[rows]
        z = urw[rows]
        prev_row = jnp.where(seg_first, 0.0, urw_halo[16 * p + 7:16 * p + 8])
        next_row = jnp.where(seg_last, 0.0, urw_halo[16 * p + 8:16 * p + 9])
        z_prev = jnp.where(row == 0, prev_row, pltpu.roll(z, 1, 0))
        z_next = jnp.where(row == tm - 1, next_row, pltpu.roll(z, tm - 1, 0))
        urw_ref[p] = z + mup_ref[...] * (z_prev - z) + mun_ref[...] * (z_next - z)
        for j in range(2 * da_width // LANE):
            t = _rope_cols(uda[rows, j * LANE:(j + 1) * LANE], cos, sin, dqk // 2)
            if j < da_width // LANE:
                t = t * q_scale
            else:
                jk = j - da_width // LANE
                ukt_ref[p, jk * LANE:(jk + 1) * LANE, :] = t.T.astype(BF16)
            uda_ref[p, :, j * LANE:(j + 1) * LANE] = t.astype(BF16)
        uda_ref[p, :, 2 * da_width:] = uda[rows, 2 * da_width:].astype(BF16)


def _dense_batch(b):
    return math.gcd(b, DENSE_BATCH)


def _mod_spec(b, d, n_ctx_tiles, off=0):
    nbp = _dense_batch(b)
    return pl.BlockSpec((nbp, 8, d), lambda bi, i: (jnp.where(i + off < n_ctx_tiles, b // nbp, bi), 0, 0))


def _in_proj(s, mods_l, g, cos_t, sin_t, mu_prev, mu_next, whg, wda, wrw, n_ctx_tiles):
    b, n, d = s.shape
    tm = ROW_TILE
    nbp = _dense_batch(b)
    da_width = wda.shape[1] // 3
    dqk = da_width // (2 * N_HEADS)
    sub = tm // 8
    full = lambda a: pl.BlockSpec(a.shape, lambda bi, i: (0, 0))
    row = lambda w: pl.BlockSpec((nbp, tm, w), lambda bi, i: (bi, i, 0))
    kern = functools.partial(_in_proj_kernel, da_width=da_width, q_scale=dqk ** -0.5 * math.log2(math.e),
                             n_ctx_tiles=n_ctx_tiles)
    return pl.pallas_call(
        kern,
        grid=(b // nbp, n // tm),
        in_specs=[row(d),
                  pl.BlockSpec((nbp, 8, d), lambda bi, i: (bi, jnp.maximum(i * sub - 1, 0), 0)),
                  pl.BlockSpec((nbp, 8, d), lambda bi, i: (bi, jnp.minimum((i + 1) * sub, n // 8 - 1), 0)),
                  _mod_spec(b, d, n_ctx_tiles), full(g),
                  pl.BlockSpec((tm, LANE), lambda bi, i: (i, 0)),
                  pl.BlockSpec((tm, LANE), lambda bi, i: (i, 0)),
                  full(mu_prev), full(mu_next), full(whg), full(wda), full(wrw)],
        out_specs=[row(whg.shape[1]), row(wda.shape[1]),
                   pl.BlockSpec((nbp, da_width, tm), lambda bi, i: (bi, 0, i)), row(wrw.shape[1])],
        out_shape=[jax.ShapeDtypeStruct((b, n, whg.shape[1]), F32),
                   jax.ShapeDtypeStruct((b, n, wda.shape[1]), BF16),
                   jax.ShapeDtypeStruct((b, da_width, n), BF16),
                   jax.ShapeDtypeStruct((b, n, wrw.shape[1]), F32)],
        compiler_params=_cparams(("parallel", "parallel")),
        name="in_proj",
    )(s, s, s, mods_l, g, cos_t, sin_t, mu_prev, mu_next, whg, wda, wrw)


def _chunk_of_step(j, n_chunks, n_ctx_chunks, rev):
    if not rev:
        return j
    return jnp.where(j < n_ctx_chunks, n_ctx_chunks - 1 - j, n_chunks - 1 - (j - n_ctx_chunks))


def _hgrn_levels(c):
    return [1 << i for i in range(int(math.log2(c)))]


def _hgrn_const(c, rev):
    t = np.arange(c)[:, None]
    s = np.arange(c)[None, :]
    tri = (s >= t) if rev else (s <= t)
    mats = [tri.astype(np.float32)]
    for m in _hgrn_levels(c):
        p = (t // (2 * m)) * (2 * m) + (m if rev else m - 1)
        ref = (s >= p) if rev else (s <= p)
        mats.append(tri.astype(np.float32) - ref.astype(np.float32))
    return np.concatenate(mats, axis=0)


def _hgrn_kernel(*refs, rev, finish, dh):
    if finish:
        const_ref, q_ref, i_ref, f_ref, lb_ref, g_ref, other_ref, ng_ref, out_ref, st_ref = refs
    else:
        const_ref, q_ref, i_ref, f_ref, lb_ref, out_ref, st_ref = refs
    nb, c, w = q_ref.shape

    @pl.when(pl.program_id(1) == 0)
    def _():
        st_ref[...] = jnp.zeros_like(st_ref)

    nh = w // dh
    t_idx = _iota((c, 1), 0)
    s_idx = _mod(_iota((1, nh * c), 1), c)
    bd = _same_head(w, dh)
    lb = lb_ref[...]
    rows = lambda ref: ref[...].reshape(nb * c, w)
    q_all = rows(q_ref)
    v_all = rows(i_ref)
    f_all = lb + (1.0 - lb) * _sigmoid(rows(f_ref))
    kx_all = 1.0 - f_all
    logf_all = jnp.log(f_all)
    (diag_all,) = _head_sums([q_all * kx_all], dh)
    bf = lambda t: t.astype(BF16)
    cs = []
    for bb in range(nb):
        sl = slice(bb * c, (bb + 1) * c)
        seg = _dot_sel(const_ref[...], logf_all[sl])
        bcum = seg[0:c]
        cs.append(dict(bb=bb, q=q_all[sl], v=v_all[sl], kx=kx_all[sl], diag=diag_all[sl], seg=seg, bcum=bcum,
                       bend=bcum[0:1] if rev else bcum[c - 1:c], att=jnp.zeros((c, nh * c), F32)))
    for li, m in enumerate(_hgrn_levels(c)):
        t_up = (t_idx & m) != 0
        s_up = (s_idx & m) != 0
        t_isq = jnp.logical_not(t_up) if rev else t_up
        s_isk = s_up if rev else jnp.logical_not(s_up)
        valid = (_div(t_idx, 2 * m) == _div(s_idx, 2 * m)) & t_isq & s_isk
        for s in cs:
            e = jnp.exp(-jnp.abs(s["seg"][(li + 1) * c:(li + 2) * c]))
            xm = bf(jnp.where(t_isq, s["q"], s["kx"]) * e)
            sc = _dot_nt(xm, _expand_heads(xm, dh))
            s["att"] = s["att"] + jnp.where(valid, sc, 0.0)
    outs = []
    for s in cs:
        q, v, kx, bcum, bend = s["q"], s["v"], s["kx"], s["bcum"], s["bend"]
        o = _dot(bf(s["att"]), _expand_heads(bf(v), dh))
        o = o + s["diag"] * v
        st = st_ref[s["bb"]]
        o = o + _dot_nt(q * jnp.exp(bcum), st, P_HG)
        upd = _dot_tn(v, kx * jnp.exp(bend - bcum), P_HG)
        st_ref[s["bb"]] = st * jnp.exp(bend) + jnp.where(bd, upd, 0.0)
        outs.append(o)
    o_all = jnp.concatenate(outs, axis=0)
    if finish:
        o_all = o_all + rows(other_ref)
        (ms,) = _head_sums([o_all * o_all], dh)
        g = rows(g_ref)
        o_all = o_all * lax.rsqrt(ms * (1.0 / dh) + RMS_EPS) * ng_ref[...] * (g * _sigmoid(g))
    out_ref[...] = o_all.reshape(nb, c, w).astype(out_ref.dtype)


def _hgrn_pass(u_hg, lb_row, n_ctx_chunks, rev, other=None, norm_g_row=None):
    b, n, _ = u_hg.shape
    c = CHUNK
    w = lb_row.shape[-1]
    dh = w // N_HEADS
    nc = n // c
    finish = other is not None
    const = jnp.asarray(np.tile(_hgrn_const(c, rev), (1, 3)), BF16)
    chunk = lambda j: _chunk_of_step(j, nc, n_ctx_chunks, rev)
    nb = math.gcd(b, REC_BATCH)
    col = lambda k: pl.BlockSpec((nb, c, w), lambda bi, j: (bi, chunk(j), k))
    vec = pl.BlockSpec((1, w), lambda bi, j: (0, 0))
    in_specs = [pl.BlockSpec(const.shape, lambda bi, j: (0, 0)), col(0), col(1), col(3 if rev else 2), vec]
    args = [const, u_hg, u_hg, u_hg, lb_row]
    if finish:
        in_specs += [col(4), col(0), vec]
        args += [u_hg, other, norm_g_row]
    return pl.pallas_call(
        functools.partial(_hgrn_kernel, rev=rev, finish=finish, dh=dh),
        grid=(b // nb, nc),
        in_specs=in_specs,
        out_specs=col(0),
        out_shape=jax.ShapeDtypeStruct((b, n, w), BF16 if finish else F32),
        scratch_shapes=[pltpu.VMEM((nb, w, w), F32)],
        compiler_params=_cparams(("parallel", "arbitrary")),
        name="hgrn_bwd" if rev else "hgrn_fwd",
    )(*args)


def _rwkv_kernel(*refs, rev, finish, dh):
    if finish:
        (tri_ref, z_ref, w0_ref, w2_ref, a0_ref, a2_ref, g2_ref,
         kk_ref, ka_ref, rk_ref, oy_ref, ob_ref, lnw_ref, lnb_ref, out_ref, st_ref) = refs
    else:
        (tri_ref, z_ref, w0_ref, w2_ref, a0_ref, a2_ref, g2_ref,
         kk_ref, ka_ref, rk_ref, y_ref, bonus_ref, st_ref) = refs
    nb, c, _ = z_ref.shape
    w = w0_ref.shape[-1]

    @pl.when(pl.program_id(1) == 0)
    def _():
        st_ref[...] = jnp.zeros_like(st_ref)

    row = _iota((c, 1), 0)
    nr = (w // dh) * c
    si = _mod(_iota((1, nr), 1), c)
    strict = (si > row) if rev else (si < row)
    incl = (si >= row) if rev else (si <= row)
    eye_cat = (si == row).astype(F32)
    blk = _same_head(nr, c)
    bd = _same_head(w, dh)
    eye_w = (_iota((w, w), 0) == _iota((w, w), 1)).astype(F32)

    def prelude(group):
        u = jnp.concatenate([z_ref[bb] for bb in group], axis=0)
        r, k, v = u[:, 0:w], u[:, w:2 * w], u[:, 2 * w:3 * w]
        lora = u[:, 3 * w:3 * w + LANE]
        gd = u[:, 3 * w + LANE:3 * w + 2 * LANE]
        w_log = -_softplus(-(w0_ref[...] + _dot_x3(jnp.tanh(lora), w2_ref[0], w2_ref[1]))) - 0.5
        lw = -jnp.exp(w_log)
        a_lr = _sigmoid(a0_ref[...] + _dot_x3(lora, a2_ref[0], a2_ref[1]))
        kk = k * kk_ref[...]
        kd = k * (1.0 + (a_lr - 1.0) * ka_ref[...])
        kk_sq, bonus_dot = _head_sums([kk * kk, r * kd * rk_ref[...]], dh)
        kk = kk * lax.rsqrt(jnp.maximum(kk_sq, 1e-24))
        cols = dict(r=r, v=v, kd=kd, lw=lw, a=-kk, b=kk * a_lr, bonus=bonus_dot * v, gd=gd)
        return [{name: t[i * c:(i + 1) * c] for name, t in cols.items()} for i in range(len(group))]

    chains = list(range(nb))
    pre = prelude(chains)

    bf = lambda t: t.astype(BF16)
    ex = lambda t: _expand_heads(bf(t), dh)
    st = []
    for d in pre:
        v, kd, lw = d["v"], d["kd"], d["lw"]
        cum = _dot_sel(tri_ref[...], lw)
        tot = cum[0:1] if rev else cum[c - 1:c]
        inv_gam = jnp.exp(-cum)
        rest = jnp.exp(tot - cum)
        r_t = d["r"] * jnp.exp(cum)
        a_t = d["a"] * jnp.exp(cum - lw)
        ea, eb, ek, ev = ex(a_t), ex(d["b"] * inv_gam), ex(kd * inv_gam), ex(v)
        sc = _dot_nt(jnp.concatenate([bf(a_t), bf(r_t)], axis=0), jnp.concatenate([eb, ek], axis=0))
        st.append(dict(
            v=v, r_t=r_t, tot=tot, ea=ea, ev=ev, b_h=d["b"] * rest, k_h=kd * rest,
            a_ab=jnp.where(strict, sc[0:c, 0:nr], 0.0), a_ak=jnp.where(strict, sc[0:c, nr:2 * nr], 0.0),
            a_rb=jnp.where(incl, sc[c:2 * c, 0:nr], 0.0), a_rk=jnp.where(incl, sc[c:2 * c, nr:2 * nr], 0.0)))

    bd_cat = lambda t: jnp.where(blk, jnp.concatenate([bf(t)] * (nr // c), axis=0), 0.0)
    for s in st:
        s["pw"] = _dot(bf(s["a_ab"]), bd_cat(s["a_ab"]))
        s["t_cat"] = eye_cat + s["a_ab"]
    n_iter = int(math.log2(c)) - 1
    for it in range(n_iter):
        for s in st:
            pw = bf(s["pw"])
            if it + 1 < n_iter:
                both = _dot(pw, jnp.concatenate([bd_cat(s["t_cat"]), bd_cat(s["pw"])], axis=1))
                s["t_cat"], s["pw"] = s["t_cat"] + both[:, 0:nr], both[:, nr:2 * nr]
            else:
                s["t_cat"] = s["t_cat"] + _dot(pw, bd_cat(s["t_cat"]))

    for s in st:
        s["t_cat"] = bf(s["t_cat"])
        s["x_c"] = _dot(bf(s["a_ak"]), s["ev"])
    for s in st:
        s["w1c"] = _dot(s["t_cat"], s["ea"])
        s["w2c"] = _dot(s["t_cat"], ex(s["x_c"]))
    for s in st:
        a_rb = bf(s["a_rb"])
        s["p_mat"] = s["r_t"] + _dot(a_rb, ex(s["w1c"]))
        s["y0"] = _dot(jnp.concatenate([a_rb, bf(s["a_rk"])], axis=1),
                       jnp.concatenate([ex(s["w2c"]), s["ev"]], axis=0))
    ys = []
    for bb, s in zip(chains, st):
        lhs = jnp.concatenate([s["w1c"], s["p_mat"], eye_w * jnp.exp(s["tot"])], axis=0)
        big = _dot(lhs, st_ref[bb], P_ST)
        ys.append(big[c:2 * c] + s["y0"])
        upd = _dot_tn(jnp.concatenate([s["b_h"], s["k_h"]], axis=0),
                      jnp.concatenate([big[0:c] + s["w2c"], s["v"]], axis=0), P_ST)
        st_ref[bb] = big[2 * c:] + jnp.where(bd, upd, 0.0)

    y = jnp.concatenate(ys, axis=0)
    bonus_all = jnp.concatenate([d["bonus"] for d in pre], axis=0)
    if finish:
        gd = jnp.concatenate([d["gd"] for d in pre], axis=0)
        y = y + oy_ref[...].reshape(nb * c, w)
        (mean,) = _head_sums([y], dh)
        yc = y - mean * (1.0 / dh)
        (var,) = _head_sums([yc * yc], dh)
        yn = yc * lax.rsqrt(var * (1.0 / dh) + RW_GN_EPS) * lnw_ref[...] + lnb_ref[...]
        gate = _dot_x3(_sigmoid(gd), g2_ref[0], g2_ref[1])
        out = (yn + bonus_all + ob_ref[...].reshape(nb * c, w)) * gate
        out_ref[...] = out.reshape(nb, c, w).astype(out_ref.dtype)
    else:
        y_ref[...] = y.reshape(nb, c, w)
        bonus_ref[...] = bonus_all.reshape(nb, c, w)


def _rwkv_pass(u_rw, p, n_ctx_chunks, rev, other=None):
    b, n, wu = u_rw.shape
    c = CHUNK
    w = p["w0"].shape[-1]
    dh = w // N_HEADS
    nc = n // c
    finish = other is not None
    t = np.arange(c)
    tri = (t[None, :] >= t[:, None]) if rev else (t[None, :] <= t[:, None])
    tri = jnp.asarray(np.tile(tri.astype(np.float32), (1, 3)), BF16)
    hi_lo = lambda m: jnp.stack(_split2(m))
    d = 1 if rev else 0
    chunk = lambda j: _chunk_of_step(j, nc, n_ctx_chunks, rev)
    nb = math.gcd(b, REC_BATCH)
    const = lambda a: pl.BlockSpec(a.shape, lambda bi, j: (0,) * a.ndim)
    col = pl.BlockSpec((nb, c, w), lambda bi, j: (bi, chunk(j), 0))
    args = [tri, u_rw, p["w0"][d], hi_lo(p["w2"][d]), p["a0"][d],
            hi_lo(p["a2"][d]), hi_lo(p["g2"]), p["k_k"], p["k_a"], p["r_k"]]
    in_specs = [const(tri), pl.BlockSpec((nb, c, wu), lambda bi, j: (bi, chunk(j), 0))]
    in_specs += [const(a) for a in args[2:]]
    if finish:
        args += [other[0], other[1], p["ln_w"], p["ln_b"]]
        in_specs += [col, col, const(p["ln_w"]), const(p["ln_b"])]
        out_specs, out_shape = col, jax.ShapeDtypeStruct((b, n, w), BF16)
    else:
        out_specs = [col, col]
        out_shape = [jax.ShapeDtypeStruct((b, n, w), F32)] * 2
    return pl.pallas_call(
        functools.partial(_rwkv_kernel, rev=rev, finish=finish, dh=dh),
        grid=(b // nb, nc),
        in_specs=in_specs,
        out_specs=out_specs,
        out_shape=out_shape,
        scratch_shapes=[pltpu.VMEM((nb, w, w), F32)],
        compiler_params=_cparams(("parallel", "arbitrary")),
        name="rwkv_bwd" if rev else "rwkv_fwd",
    )(*args)


def _attn_lam(lam_ref, lam_init):
    lp = lam_ref[...]
    return (jnp.exp(jnp.sum(lp[0:1] * lp[1:2], axis=-1, keepdims=True))
            - jnp.exp(jnp.sum(lp[2:3] * lp[3:4], axis=-1, keepdims=True)) + lam_init)


def _stack_maps(q):
    half = q.shape[-1] // 2
    lane = _iota((1, q.shape[-1]), 1)
    return jnp.concatenate([jnp.where(lane < half, q, 0), jnp.where(lane >= half, q, 0)], axis=0)


def _attn_finish(o, ng, lam_init):
    ms = jnp.mean(o * o, axis=-1, keepdims=True)
    return o * lax.rsqrt(ms + RMS_EPS) * ng * (1.0 - lam_init)


def _attn_kernel(lam_ref, q_ref, k_ref, v_ref, ng_ref, o_ref, *, lam_init, n_ctx, n_ctx_tiles):
    lam = _attn_lam(lam_ref, lam_init)

    def attend(k, v):
        ts = ATT_SUB
        subs = range(q_ref.shape[0] // ts)
        s = [_dot(_stack_maps(q_ref[i * ts:(i + 1) * ts, :]), k) for i in subs]
        for i, x in zip(subs, s):
            p = jnp.exp2(x - jnp.max(x, axis=-1, keepdims=True))
            den = jnp.sum(p, axis=-1, keepdims=True)
            pc = p[0:ts] - p[ts:2 * ts] * (lam * den[0:ts] / den[ts:2 * ts])
            o = _attn_finish(_dot(pc.astype(BF16), v) / den[0:ts], ng_ref[...], lam_init)
            o_ref[i * ts:(i + 1) * ts, :] = o.astype(o_ref.dtype)

    is_ctx = pl.program_id(2) < n_ctx_tiles

    @pl.when(is_ctx)
    def _():
        attend(k_ref[:, 0:n_ctx], v_ref[0:n_ctx, :])

    @pl.when(jnp.logical_not(is_ctx))
    def _():
        attend(k_ref[...], v_ref[...])


def _attention(u_da, u_kt, lam_rows, norm_g_row, lam_init, n_ctx):
    b, n, w3 = u_da.shape
    wd = w3 // 3
    dv = wd // N_HEADS
    tq = ROW_TILE
    kern = functools.partial(_attn_kernel, lam_init=lam_init, n_ctx=n_ctx, n_ctx_tiles=n_ctx // tq)
    return pl.pallas_call(
        kern,
        grid=(b, N_HEADS, n // tq),
        in_specs=[pl.BlockSpec(lam_rows.shape, lambda bi, h, i: (0, 0)),
                  pl.BlockSpec((None, tq, dv), lambda bi, h, i: (bi, i, h)),
                  pl.BlockSpec((None, dv, n), lambda bi, h, i: (bi, h, 0)),
                  pl.BlockSpec((None, n, dv), lambda bi, h, i: (bi, 0, 2 * N_HEADS + h)),
                  pl.BlockSpec((1, dv), lambda bi, h, i: (0, 0))],
        out_specs=pl.BlockSpec((None, tq, dv), lambda bi, h, i: (bi, i, h)),
        out_shape=jax.ShapeDtypeStruct((b, n, wd), BF16),
        compiler_params=_cparams(("parallel", "parallel", "arbitrary")),
        name="diff_attn",
    )(lam_rows, u_da, u_kt, u_da, norm_g_row)


def _mlp_kernel(*refs, final):
    if final:
        (s_ref, hg_ref, da_ref, rw_ref, mod_ref, g_ref, whg_ref, wda_ref, wrw_ref, wg_ref, wu_ref, wd_ref,
         fg_ref, o_ref) = refs
    else:
        (s_ref, hg_ref, da_ref, rw_ref, mod_ref, g_ref, whg_ref, wda_ref, wrw_ref, wg_ref, wu_ref, wd_ref,
         o_ref) = refs
    nbp, tm, d = s_ref.shape
    stack = lambda ref: ref[...].reshape(nbp * tm, ref.shape[-1])
    mix = _dot(stack(hg_ref), whg_ref[...]) + _dot(stack(da_ref), wda_ref[...]) + _dot(stack(rw_ref), wrw_ref[...])
    xs, hs = [], []
    for p in range(nbp):
        x = s_ref[p] + mod_ref[p, 2:3, :] * mix[p * tm:(p + 1) * tm]
        ms = jnp.mean(x * x, axis=-1, keepdims=True)
        y = x * lax.rsqrt(ms + RMS_EPS) * g_ref[...]
        xs.append(x)
        hs.append((y * (1.0 + mod_ref[p, 4:5, :]) + mod_ref[p, 3:4, :]).astype(BF16))
    h = jnp.concatenate(hs, axis=0)
    gate = _dot(h, wg_ref[...])
    up = _dot(h, wu_ref[...])
    down = _dot((gate * _sigmoid(gate) * up).astype(BF16), wd_ref[...])
    for p in range(nbp):
        x = xs[p] + mod_ref[p, 5:6, :] * down[p * tm:(p + 1) * tm]
        if final:
            ms = jnp.mean(x * x, axis=-1, keepdims=True)
            x = x * lax.rsqrt(ms + RMS_EPS) * fg_ref[...]
        o_ref[p] = x


def _mlp(s, o_hg, o_da, o_rw, mods_l, g, w_out_parts, wg, wu, wd, n_ctx_tiles, final_g=None):
    b, n, d = s.shape
    tm = ROW_TILE
    final = final_g is not None
    off = n_ctx_tiles if final else 0
    nbp = _dense_batch(b)
    resident = lambda a: pl.BlockSpec(a.shape, lambda bi, i: (0, 0), pipeline_mode=pl.Buffered(1))
    row = lambda a: pl.BlockSpec((nbp, tm, a.shape[-1]), lambda bi, i: (bi, i + off, 0))
    weights = [*w_out_parts, wg, wu, wd]
    in_specs = [row(s), row(o_hg), row(o_da), row(o_rw), _mod_spec(b, d, n_ctx_tiles, off),
                pl.BlockSpec(g.shape, lambda bi, i: (0, 0))] + [resident(a) for a in weights]
    args = [s, o_hg, o_da, o_rw, mods_l, g, *weights]
    if final:
        in_specs.append(pl.BlockSpec((1, d), lambda bi, i: (0, 0)))
        args.append(final_g)
    return pl.pallas_call(
        functools.partial(_mlp_kernel, final=final),
        grid=(b // nbp, n // tm - off),
        in_specs=in_specs,
        out_specs=pl.BlockSpec((nbp, tm, d), lambda bi, i: (bi, i, 0)),
        out_shape=jax.ShapeDtypeStruct((b, n - off * tm, d), F32),
        compiler_params=_cparams(("parallel", "parallel")),
        name="mlp",
    )(*args)


def _rope_tables(n_ctx, n_lat, dqk):
    n_freq = dqk // 4
    t = np.arange(n_lat)
    inv_freq = ROPE_BASE ** (-np.arange(n_freq, dtype=np.float32) / n_freq)
    ang = np.concatenate([(t // GRID_W)[:, None] * inv_freq, (t % GRID_W)[:, None] * inv_freq], axis=-1)
    reps = LANE // dqk
    cos = np.tile(np.concatenate([np.cos(ang), np.cos(ang)], axis=-1), (1, reps))
    sin = np.tile(np.concatenate([-np.sin(ang), np.sin(ang)], axis=-1), (1, reps))
    cos = np.concatenate([np.ones((n_ctx, LANE)), cos], axis=0)
    sin = np.concatenate([np.zeros((n_ctx, LANE)), sin], axis=0)
    return jnp.asarray(cos, F32), jnp.asarray(sin, F32)


def _pad_rows(a, start, total):
    return jnp.zeros((total, a.shape[-1]), a.dtype).at[start:start + a.shape[0]].set(a)


def kernel(x, c, ctx, c_ctx, ada_w, ada_b, norm1_g, norm2_g, w_in, w_out, hg_lb_logits, hg_norm_g, da_lam_q1, da_lam_k1, da_lam_q2, da_lam_k2, da_norm_g, rw_mu_prev, rw_mu_next, rw_w0, rw_w2, rw_a0, rw_a2, rw_g2, rw_k_k, rw_k_a, rw_r_k, rw_ln_w, rw_ln_b, ffn_w_gate, ffn_w_up, ffn_w_down, final_norm_g):
    b, n_lat, d = x.shape
    n_ctx = ctx.shape[1]
    depth = ada_w.shape[0]
    hg_w = hg_lb_logits.shape[-1]
    rw_w = rw_w0.shape[-1]
    da_w = w_out.shape[1] - hg_w - rw_w
    hg_cols, da_cols = 5 * hg_w, 3 * da_w
    rw_cols = w_in.shape[-1] - hg_cols - da_cols
    rw_pad = 3 * rw_w + 2 * LANE
    r_dec, r_icl, r_gate = rw_w2.shape[2], rw_a2.shape[2], rw_g2.shape[1]
    assert n_ctx % ROW_TILE == 0 and n_lat % ROW_TILE == 0 and n_lat % GRID_W == 0
    assert 2 * (r_dec + r_icl) == LANE and r_gate <= LANE and rw_cols == 3 * rw_w + LANE + r_gate
    n_ctx_tiles, n_ctx_chunks = n_ctx // ROW_TILE, n_ctx // CHUNK

    n_cond = -(-(b + _dense_batch(b)) // 8) * 8
    cond = jnp.zeros((n_cond, d), F32).at[:b].set(c).at[b:b + _dense_batch(b)].set(c_ctx)
    mods = _ada_mod(cond, ada_w, ada_b).reshape(depth, n_cond, 6, d)
    mods = jnp.pad(mods, ((0, 0), (0, 0), (0, 2), (0, 0)))

    p_lb = jax.nn.softmax(hg_lb_logits.astype(F32), axis=1)
    lower_bounds = jnp.cumsum(p_lb, axis=1) - p_lb[:, :1]
    cos_t, sin_t = _rope_tables(n_ctx, n_lat, da_w // (2 * N_HEADS))

    s = jnp.concatenate([ctx, x], axis=1)
    for l in range(depth):
        w_l = w_in[l]
        whg = w_l[:, :hg_cols].astype(BF16)
        wda = w_l[:, hg_cols:hg_cols + da_cols].astype(BF16)
        wrw = jnp.pad(w_l[:, hg_cols + da_cols:], ((0, 0), (0, rw_pad - rw_cols))).astype(BF16)
        mu_pad = lambda m: jnp.pad(m, (0, rw_pad - rw_cols))[None]
        u_hg, u_da, u_kt, u_rw = _in_proj(s, mods[l], norm1_g[l][None], cos_t, sin_t, mu_pad(rw_mu_prev[l]),
                                    mu_pad(rw_mu_next[l]), whg, wda, wrw, n_ctx_tiles)

        o_b = _hgrn_pass(u_hg, lower_bounds[1, l][None], n_ctx_chunks, True)
        o_hg = _hgrn_pass(u_hg, lower_bounds[0, l][None], n_ctx_chunks, False, other=o_b,
                          norm_g_row=jnp.tile(hg_norm_g[l], N_HEADS)[None])

        lam_init = 0.8 - 0.6 * math.exp(-0.3 * l)
        lam_rows = jnp.zeros((8, LANE), F32).at[0:4, :da_lam_q1.shape[-1]].set(
            jnp.stack([da_lam_q1[l], da_lam_k1[l], da_lam_q2[l], da_lam_k2[l]]))
        o_da = _attention(u_da, u_kt, lam_rows, da_norm_g[l][None], lam_init, n_ctx)

        off_dec, off_icl = 0, 2 * r_dec
        rw_p = {
            "w0": rw_w0[:, l][:, None, :], "a0": rw_a0[:, l][:, None, :],
            "w2": jnp.stack([_pad_rows(rw_w2[i, l], off_dec + i * r_dec, LANE) for i in range(2)]),
            "a2": jnp.stack([_pad_rows(rw_a2[i, l], off_icl + i * r_icl, LANE) for i in range(2)]),
            "g2": _pad_rows(rw_g2[l], 0, LANE),
            "k_k": rw_k_k[l][None], "k_a": rw_k_a[l][None], "r_k": rw_r_k[l][None],
            "ln_w": rw_ln_w[l][None], "ln_b": rw_ln_b[l][None],
        }
        yb = _rwkv_pass(u_rw, rw_p, n_ctx_chunks, True)
        o_rw = _rwkv_pass(u_rw, rw_p, n_ctx_chunks, False, other=yb)

        wo = w_out[l].astype(BF16)
        s = _mlp(s, o_hg, o_da, o_rw, mods[l], norm2_g[l][None],
                 (wo[:hg_w], wo[hg_w:hg_w + da_w], wo[hg_w + da_w:]),
                 ffn_w_gate[l].astype(BF16), ffn_w_up[l].astype(BF16), ffn_w_down[l].astype(BF16),
                 n_ctx_tiles, final_g=final_norm_g[None] if l == depth - 1 else None)
    return s
```

```python
import functools
import math

import numpy as np
import jax
import jax.numpy as jnp
from jax import lax
from jax.experimental import pallas as pl
from jax.experimental.pallas import tpu as pltpu

F32 = jnp.float32
BF16 = jnp.bfloat16
HI = lax.Precision.HIGHEST

N_HEADS = 4
GRID_W = 64
ROPE_BASE = 10000.0
RMS_EPS = 1e-6
RW_GN_EPS = 64e-5

ROW_TILE = 256
DENSE_BATCH = 2
ATT_SUB = 64
CHUNK = 64
REC_BATCH = 8
LANE = 128
VMEM_DENSE_MIB = 48
VMEM_MIXER_MIB = 32


def _cparams(sem, vmem_mib=VMEM_MIXER_MIB):
    return pltpu.CompilerParams(dimension_semantics=sem, vmem_limit_bytes=vmem_mib * 1024 * 1024)


def _dot_dims(a, b, dims, prec):
    dg = lambda x, y, p=None: lax.dot_general(x, y, (dims, ((), ())), preferred_element_type=F32, precision=p)
    if prec == "bf16":
        return dg(a.astype(BF16), b.astype(BF16))
    return dg(a, b, prec)


def _dot(a, b, prec=None):
    return _dot_dims(a, b, ((1,), (0,)), prec)


def _dot_nt(a, b, prec=None):
    return _dot_dims(a, b, ((1,), (1,)), prec)


def _dot_tn(a, b, prec=None):
    return _dot_dims(a, b, ((0,), (0,)), prec)


P_HG = "bf16"
P_ST = "bf16"


def _iota(shape, dim):
    return lax.broadcasted_iota(jnp.int32, shape, dim)


def _div(x, d):
    assert d & (d - 1) == 0
    return x >> (d.bit_length() - 1)


def _mod(x, d):
    assert d & (d - 1) == 0
    return x & (d - 1)


def _softplus(x):
    return jnp.maximum(x, 0.0) + jnp.log(1.0 + jnp.exp(-jnp.abs(x)))


def _same_head(w, dh):
    return _div(_iota((w, w), 0), dh) == _div(_iota((w, w), 1), dh)


def _expand_heads(x, dh):
    w = x.shape[-1]
    head = _div(_iota((1, w), 1), dh)
    return jnp.concatenate([jnp.where(head == h, x, 0.0) for h in range(w // dh)], axis=0)


def _split3(x):
    hi = x.astype(BF16)
    r = x - hi.astype(F32)
    mid = r.astype(BF16)
    lo = (r - mid.astype(F32)).astype(BF16)
    return hi, mid, lo


def _split2(x):
    hi = x.astype(BF16)
    return hi, (x - hi.astype(F32)).astype(BF16)


def _head_sums(xs, dh):
    w = xs[0].shape[-1]
    ones_bd = _same_head(w, dh).astype(BF16)
    out = _dot(jnp.concatenate([p for x in xs for p in _split3(x)], axis=0), ones_bd)
    res, off = [], 0
    for x in xs:
        r = x.shape[0]
        res.append(out[off:off + r] + (out[off + r:off + 2 * r] + out[off + 2 * r:off + 3 * r]))
        off += 3 * r
    return res


def _dot_sel(sel3, x):
    return _dot(sel3, jnp.concatenate(_split3(x), axis=0))


def _dot_x3(x, w_hi, w_lo):
    x_hi, x_lo = _split2(x)
    r = x.shape[0]
    t = _dot(jnp.concatenate([x_hi, x_lo], axis=0), w_hi)
    return t[0:r] + t[r:2 * r] + _dot(x_hi, w_lo)


def _sigmoid(x):
    return 1.0 / (1.0 + jnp.exp(-x))


def _ada_kernel(c_ref, w_ref, b_ref, o_ref):
    c = c_ref[...]
    sc = c * _sigmoid(c)
    o_ref[...] = _dot(sc, w_ref[...], HI) + b_ref[...]


def _ada_mod(cond, ada_w, ada_b):
    depth, d, d6 = ada_w.shape
    rows = cond.shape[0]
    tn = 1536
    return pl.pallas_call(
        _ada_kernel,
        grid=(depth, d6 // tn),
        in_specs=[pl.BlockSpec((rows, d), lambda l, j: (0, 0)),
                  pl.BlockSpec((None, d, tn), lambda l, j: (l, 0, j)),
                  pl.BlockSpec((None, 1, tn), lambda l, j: (l, 0, j))],
        out_specs=pl.BlockSpec((None, rows, tn), lambda l, j: (l, 0, j)),
        out_shape=jax.ShapeDtypeStruct((depth, rows, d6), F32),
        compiler_params=_cparams(("arbitrary", "arbitrary")),
        name="ada_mod",
    )(cond, ada_w, ada_b.reshape(depth, 1, d6))


def _rope_cols(t, cos, sin_signed, half):
    lane = _iota((1, t.shape[-1]), 1)
    first = _mod(lane, 2 * half) < half
    swapped = jnp.where(first, pltpu.roll(t, t.shape[-1] - half, 1), pltpu.roll(t, half, 1))
    return t * cos + swapped * sin_signed


def _in_proj_kernel(s_ref, sp_ref, sn_ref, mod_ref, g_ref, cos_ref, sin_ref, mup_ref, mun_ref,
                    whg_ref, wda_ref, wrw_ref, uhg_ref, uda_ref, ukt_ref, urw_ref, *, da_width, q_scale,
                    n_ctx_tiles):
    nbp, tm, _ = s_ref.shape

    def norm_mod(x, p):
        ms = jnp.mean(x * x, axis=-1, keepdims=True)
        y = x * lax.rsqrt(ms + RMS_EPS) * g_ref[...]
        return y * (1.0 + mod_ref[p, 1:2, :]) + mod_ref[p, 0:1, :]

    h = jnp.concatenate([norm_mod(s_ref[p], p).astype(BF16) for p in range(nbp)], axis=0)
    uhg = _dot(h, whg_ref[...])
    uda = _dot(h, wda_ref[...])
    halo = jnp.concatenate([norm_mod(ref[p], p) for p in range(nbp) for ref in (sp_ref, sn_ref)], axis=0)
    urw = _dot(jnp.concatenate([h, halo.astype(BF16)], axis=0), wrw_ref[...])
    urw_halo = urw[nbp * tm:]
    i = pl.program_id(1)
    seg_first = (i == 0) | (i == n_ctx_tiles)
    seg_last = (i == n_ctx_tiles - 1) | (i == pl.num_programs(1) - 1)
    row = _iota((tm, 1), 0)
    cos = cos_ref[...]
    sin = sin_ref[...]
    dqk = da_width // (2 * N_HEADS)
    for p in range(nbp):
        rows = slice(p * tm, (p + 1) * tm)
        uhg_ref[p] = uhg[rows]
        z = urw[rows]
        prev_row = jnp.where(seg_first, 0.0, urw_halo[16 * p + 7:16 * p + 8])
        next_row = jnp.where(seg_last, 0.0, urw_halo[16 * p + 8:16 * p + 9])
        z_prev = jnp.where(row == 0, prev_row, pltpu.roll(z, 1, 0))
        z_next = jnp.where(row == tm - 1, next_row, pltpu.roll(z, tm - 1, 0))
        urw_ref[p] = z + mup_ref[...] * (z_prev - z) + mun_ref[...] * (z_next - z)
        for j in range(2 * da_width // LANE):
            t = _rope_cols(uda[rows, j * LANE:(j + 1) * LANE], cos, sin, dqk // 2)
            if j < da_width // LANE:
                uda_ref[p, :, j * LANE:(j + 1) * LANE] = (t * q_scale).astype(BF16)
            else:
                jk = j - da_width // LANE
                ukt_ref[p, jk * LANE:(jk + 1) * LANE, :] = t.T.astype(BF16)
        uda_ref[p, :, da_width:] = uda[rows, 2 * da_width:].astype(BF16)


def _dense_batch(b):
    return math.gcd(b, DENSE_BATCH)


def _mod_spec(b, d, n_ctx_tiles, off=0):
    nbp = _dense_batch(b)
    return pl.BlockSpec((nbp, 8, d), lambda bi, i: (jnp.where(i + off < n_ctx_tiles, b // nbp, bi), 0, 0))


def _in_proj(s, mods_l, g, cos_t, sin_t, mu_prev, mu_next, whg, wda, wrw, n_ctx_tiles):
    b, n, d = s.shape
    tm = ROW_TILE
    nbp = _dense_batch(b)
    da_width = wda.shape[1] // 3
    dqk = da_width // (2 * N_HEADS)
    sub = tm // 8
    full = lambda a: pl.BlockSpec(a.shape, lambda bi, i: (0, 0))
    row = lambda w: pl.BlockSpec((nbp, tm, w), lambda bi, i: (bi, i, 0))
    kern = functools.partial(_in_proj_kernel, da_width=da_width, q_scale=dqk ** -0.5 * math.log2(math.e),
                             n_ctx_tiles=n_ctx_tiles)
    return pl.pallas_call(
        kern,
        grid=(b // nbp, n // tm),
        in_specs=[row(d),
                  pl.BlockSpec((nbp, 8, d), lambda bi, i: (bi, jnp.maximum(i * sub - 1, 0), 0)),
                  pl.BlockSpec((nbp, 8, d), lambda bi, i: (bi, jnp.minimum((i + 1) * sub, n // 8 - 1), 0)),
                  _mod_spec(b, d, n_ctx_tiles), full(g),
                  pl.BlockSpec((tm, LANE), lambda bi, i: (i, 0)),
                  pl.BlockSpec((tm, LANE), lambda bi, i: (i, 0)),
                  full(mu_prev), full(mu_next), full(whg), full(wda), full(wrw)],
        out_specs=[row(whg.shape[1]), row(2 * da_width),
                   pl.BlockSpec((nbp, da_width, tm), lambda bi, i: (bi, 0, i)), row(wrw.shape[1])],
        out_shape=[jax.ShapeDtypeStruct((b, n, whg.shape[1]), F32),
                   jax.ShapeDtypeStruct((b, n, 2 * da_width), BF16),
                   jax.ShapeDtypeStruct((b, da_width, n), BF16),
                   jax.ShapeDtypeStruct((b, n, wrw.shape[1]), F32)],
        compiler_params=_cparams(("parallel", "parallel"), VMEM_DENSE_MIB),
        name="in_proj",
    )(s, s, s, mods_l, g, cos_t, sin_t, mu_prev, mu_next, whg, wda, wrw)


def _chunk_of_step(j, n_chunks, n_ctx_chunks, rev):
    if not rev:
        return j
    return jnp.where(j < n_ctx_chunks, n_ctx_chunks - 1 - j, n_chunks - 1 - (j - n_ctx_chunks))


def _hgrn_levels(c):
    return [1 << i for i in range(int(math.log2(c)))]


def _hgrn_const(c, rev):
    t = np.arange(c)[:, None]
    s = np.arange(c)[None, :]
    tri = (s >= t) if rev else (s <= t)
    mats = [tri.astype(np.float32)]
    for m in _hgrn_levels(c):
        p = (t // (2 * m)) * (2 * m) + (m if rev else m - 1)
        ref = (s >= p) if rev else (s <= p)
        mats.append(tri.astype(np.float32) - ref.astype(np.float32))
    return np.concatenate(mats, axis=0)


def _hgrn_kernel(*refs, rev, finish, dh):
    if finish:
        const_ref, q_ref, i_ref, f_ref, lb_ref, g_ref, other_ref, ng_ref, out_ref, st_ref = refs
    else:
        const_ref, q_ref, i_ref, f_ref, lb_ref, out_ref, st_ref = refs
    nb, c, w = q_ref.shape

    @pl.when(pl.program_id(1) == 0)
    def _():
        st_ref[...] = jnp.zeros_like(st_ref)

    nh = w // dh
    t_idx = _iota((c, 1), 0)
    s_idx = _mod(_iota((1, nh * c), 1), c)
    bd = _same_head(w, dh)
    lb = lb_ref[...]
    rows = lambda ref: ref[...].reshape(nb * c, w)
    q_all = rows(q_ref)
    v_all = rows(i_ref)
    f_all = lb + (1.0 - lb) * _sigmoid(rows(f_ref))
    kx_all = 1.0 - f_all
    logf_all = jnp.log(f_all)
    (diag_all,) = _head_sums([q_all * kx_all], dh)
    bf = lambda t: t.astype(BF16)
    cs = []
    for bb in range(nb):
        sl = slice(bb * c, (bb + 1) * c)
        seg = _dot_sel(const_ref[...], logf_all[sl])
        bcum = seg[0:c]
        cs.append(dict(bb=bb, q=q_all[sl], v=v_all[sl], kx=kx_all[sl], diag=diag_all[sl], seg=seg, bcum=bcum,
                       bend=bcum[0:1] if rev else bcum[c - 1:c], att=jnp.zeros((c, nh * c), F32)))
    for li, m in enumerate(_hgrn_levels(c)):
        t_up = (t_idx & m) != 0
        s_up = (s_idx & m) != 0
        t_isq = jnp.logical_not(t_up) if rev else t_up
        s_isk = s_up if rev else jnp.logical_not(s_up)
        valid = (_div(t_idx, 2 * m) == _div(s_idx, 2 * m)) & t_isq & s_isk
        for s in cs:
            e = jnp.exp(-jnp.abs(s["seg"][(li + 1) * c:(li + 2) * c]))
            xm = bf(jnp.where(t_isq, s["q"], s["kx"]) * e)
            sc = _dot_nt(xm, _expand_heads(xm, dh))
            s["att"] = s["att"] + jnp.where(valid, sc, 0.0)
    outs = []
    for s in cs:
        q, v, kx, bcum, bend = s["q"], s["v"], s["kx"], s["bcum"], s["bend"]
        o = _dot(bf(s["att"]), _expand_heads(bf(v), dh))
        o = o + s["diag"] * v
        st = st_ref[s["bb"]]
        o = o + _dot_nt(q * jnp.exp(bcum), st, P_HG)
        upd = _dot_tn(v, kx * jnp.exp(bend - bcum), P_HG)
        st_ref[s["bb"]] = st * jnp.exp(bend) + jnp.where(bd, upd, 0.0)
        outs.append(o)
    o_all = jnp.concatenate(outs, axis=0)
    if finish:
        o_all = o_all + rows(other_ref)
        (ms,) = _head_sums([o_all * o_all], dh)
        g = rows(g_ref)
        o_all = o_all * lax.rsqrt(ms * (1.0 / dh) + RMS_EPS) * ng_ref[...] * (g * _sigmoid(g))
    out_ref[...] = o_all.reshape(nb, c, w).astype(out_ref.dtype)


def _hgrn_pass(u_hg, lb_row, n_ctx_chunks, rev, other=None, norm_g_row=None):
    b, n, _ = u_hg.shape
    c = CHUNK
    w = lb_row.shape[-1]
    dh = w // N_HEADS
    nc = n // c
    finish = other is not None
    const = jnp.asarray(np.tile(_hgrn_const(c, rev), (1, 3)), BF16)
    chunk = lambda j: _chunk_of_step(j, nc, n_ctx_chunks, rev)
    nb = math.gcd(b, REC_BATCH)
    col = lambda k: pl.BlockSpec((nb, c, w), lambda bi, j: (bi, chunk(j), k))
    vec = pl.BlockSpec((1, w), lambda bi, j: (0, 0))
    in_specs = [pl.BlockSpec(const.shape, lambda bi, j: (0, 0)), col(0), col(1), col(3 if rev else 2), vec]
    args = [const, u_hg, u_hg, u_hg, lb_row]
    if finish:
        in_specs += [col(4), col(0), vec]
        args += [u_hg, other, norm_g_row]
    return dict(kernel=functools.partial(_hgrn_kernel, rev=rev, finish=finish, dh=dh), grid=(b // nb, nc),
                args=args, in_specs=in_specs, out_specs=[col(0)],
                out_shape=[jax.ShapeDtypeStruct((b, n, w), BF16 if finish else F32)],
                scratch=pltpu.VMEM((nb, w, w), F32))


def _rwkv_kernel(*refs, rev, finish, dh):
    if finish:
        (tri_ref, z_ref, w0_ref, w2_ref, a0_ref, a2_ref, g2_ref,
         kk_ref, ka_ref, rk_ref, oy_ref, ob_ref, lnw_ref, lnb_ref, out_ref, st_ref) = refs
    else:
        (tri_ref, z_ref, w0_ref, w2_ref, a0_ref, a2_ref, g2_ref,
         kk_ref, ka_ref, rk_ref, y_ref, bonus_ref, st_ref) = refs
    nb, c, _ = z_ref.shape
    w = w0_ref.shape[-1]

    @pl.when(pl.program_id(1) == 0)
    def _():
        st_ref[...] = jnp.zeros_like(st_ref)

    row = _iota((c, 1), 0)
    nr = (w // dh) * c
    si = _mod(_iota((1, nr), 1), c)
    strict = (si > row) if rev else (si < row)
    incl = (si >= row) if rev else (si <= row)
    eye_cat = (si == row).astype(F32)
    blk = _same_head(nr, c)
    bd = _same_head(w, dh)
    eye_w = (_iota((w, w), 0) == _iota((w, w), 1)).astype(F32)

    def prelude(group):
        u = jnp.concatenate([z_ref[bb] for bb in group], axis=0)
        r, k, v = u[:, 0:w], u[:, w:2 * w], u[:, 2 * w:3 * w]
        lora = u[:, 3 * w:3 * w + LANE]
        gd = u[:, 3 * w + LANE:3 * w + 2 * LANE]
        w_log = -_softplus(-(w0_ref[...] + _dot_x3(jnp.tanh(lora), w2_ref[0], w2_ref[1]))) - 0.5
        lw = -jnp.exp(w_log)
        a_lr = _sigmoid(a0_ref[...] + _dot_x3(lora, a2_ref[0], a2_ref[1]))
        kk = k * kk_ref[...]
        kd = k * (1.0 + (a_lr - 1.0) * ka_ref[...])
        kk_sq, bonus_dot = _head_sums([kk * kk, r * kd * rk_ref[...]], dh)
        kk = kk * lax.rsqrt(jnp.maximum(kk_sq, 1e-24))
        cols = dict(r=r, v=v, kd=kd, lw=lw, a=-kk, b=kk * a_lr, bonus=bonus_dot * v, gd=gd)
        return [{name: t[i * c:(i + 1) * c] for name, t in cols.items()} for i in range(len(group))]

    chains = list(range(nb))
    pre = prelude(chains)

    bf = lambda t: t.astype(BF16)
    ex = lambda t: _expand_heads(bf(t), dh)
    st = []
    for d in pre:
        v, kd, lw = d["v"], d["kd"], d["lw"]
        cum = _dot_sel(tri_ref[...], lw)
        tot = cum[0:1] if rev else cum[c - 1:c]
        inv_gam = jnp.exp(-cum)
        rest = jnp.exp(tot - cum)
        r_t = d["r"] * jnp.exp(cum)
        a_t = d["a"] * jnp.exp(cum - lw)
        ea, eb, ek, ev = ex(a_t), ex(d["b"] * inv_gam), ex(kd * inv_gam), ex(v)
        sc = _dot_nt(jnp.concatenate([bf(a_t), bf(r_t)], axis=0), jnp.concatenate([eb, ek], axis=0))
        st.append(dict(
            v=v, r_t=r_t, tot=tot, ea=ea, ev=ev, b_h=d["b"] * rest, k_h=kd * rest,
            a_ab=jnp.where(strict, sc[0:c, 0:nr], 0.0), a_ak=jnp.where(strict, sc[0:c, nr:2 * nr], 0.0),
            a_rb=jnp.where(incl, sc[c:2 * c, 0:nr], 0.0), a_rk=jnp.where(incl, sc[c:2 * c, nr:2 * nr], 0.0)))

    bd_cat = lambda t: jnp.where(blk, jnp.concatenate([bf(t)] * (nr // c), axis=0), 0.0)
    for s in st:
        s["pw"] = _dot(bf(s["a_ab"]), bd_cat(s["a_ab"]))
        s["t_cat"] = eye_cat + s["a_ab"]
    n_iter = int(math.log2(c)) - 1
    for it in range(n_iter):
        for s in st:
            pw = bf(s["pw"])
            if it + 1 < n_iter:
                both = _dot(pw, jnp.concatenate([bd_cat(s["t_cat"]), bd_cat(s["pw"])], axis=1))
                s["t_cat"], s["pw"] = s["t_cat"] + both[:, 0:nr], both[:, nr:2 * nr]
            else:
                s["t_cat"] = s["t_cat"] + _dot(pw, bd_cat(s["t_cat"]))

    for s in st:
        s["t_cat"] = bf(s["t_cat"])
        s["x_c"] = _dot(bf(s["a_ak"]), s["ev"])
    for s in st:
        s["w1c"] = _dot(s["t_cat"], s["ea"])
        s["w2c"] = _dot(s["t_cat"], ex(s["x_c"]))
    for s in st:
        a_rb = bf(s["a_rb"])
        s["p_mat"] = s["r_t"] + _dot(a_rb, ex(s["w1c"]))
        s["y0"] = _dot(jnp.concatenate([a_rb, bf(s["a_rk"])], axis=1),
                       jnp.concatenate([ex(s["w2c"]), s["ev"]], axis=0))
    ys = []
    for bb, s in zip(chains, st):
        lhs = jnp.concatenate([s["w1c"], s["p_mat"], eye_w * jnp.exp(s["tot"])], axis=0)
        big = _dot(lhs, st_ref[bb], P_ST)
        ys.append(big[c:2 * c] + s["y0"])
        upd = _dot_tn(jnp.concatenate([s["b_h"], s["k_h"]], axis=0),
                      jnp.concatenate([big[0:c] + s["w2c"], s["v"]], axis=0), P_ST)
        st_ref[bb] = big[2 * c:] + jnp.where(bd, upd, 0.0)

    y = jnp.concatenate(ys, axis=0)
    bonus_all = jnp.concatenate([d["bonus"] for d in pre], axis=0)
    if finish:
        gd = jnp.concatenate([d["gd"] for d in pre], axis=0)
        y = y + oy_ref[...].reshape(nb * c, w)
        (mean,) = _head_sums([y], dh)
        yc = y - mean * (1.0 / dh)
        (var,) = _head_sums([yc * yc], dh)
        yn = yc * lax.rsqrt(var * (1.0 / dh) + RW_GN_EPS) * lnw_ref[...] + lnb_ref[...]
        gate = _dot_x3(_sigmoid(gd), g2_ref[0], g2_ref[1])
        out = (yn + bonus_all + ob_ref[...].reshape(nb * c, w)) * gate
        out_ref[...] = out.reshape(nb, c, w).astype(out_ref.dtype)
    else:
        y_ref[...] = y.reshape(nb, c, w)
        bonus_ref[...] = bonus_all.reshape(nb, c, w)


def _rwkv_pass(u_rw, p, n_ctx_chunks, rev, other=None):
    b, n, wu = u_rw.shape
    c = CHUNK
    w = p["w0"].shape[-1]
    dh = w // N_HEADS
    nc = n // c
    finish = other is not None
    t = np.arange(c)
    tri = (t[None, :] >= t[:, None]) if rev else (t[None, :] <= t[:, None])
    tri = jnp.asarray(np.tile(tri.astype(np.float32), (1, 3)), BF16)
    hi_lo = lambda m: jnp.stack(_split2(m))
    d = 1 if rev else 0
    chunk = lambda j: _chunk_of_step(j, nc, n_ctx_chunks, rev)
    nb = math.gcd(b, REC_BATCH)
    const = lambda a: pl.BlockSpec(a.shape, lambda bi, j: (0,) * a.ndim)
    col = pl.BlockSpec((nb, c, w), lambda bi, j: (bi, chunk(j), 0))
    args = [tri, u_rw, p["w0"][d], hi_lo(p["w2"][d]), p["a0"][d],
            hi_lo(p["a2"][d]), hi_lo(p["g2"]), p["k_k"], p["k_a"], p["r_k"]]
    in_specs = [const(tri), pl.BlockSpec((nb, c, wu), lambda bi, j: (bi, chunk(j), 0))]
    in_specs += [const(a) for a in args[2:]]
    if finish:
        args += [other[0], other[1], p["ln_w"], p["ln_b"]]
        in_specs += [col, col, const(p["ln_w"]), const(p["ln_b"])]
        out_specs, out_shape = [col], [jax.ShapeDtypeStruct((b, n, w), BF16)]
    else:
        out_specs = [col, col]
        out_shape = [jax.ShapeDtypeStruct((b, n, w), F32)] * 2
    return dict(kernel=functools.partial(_rwkv_kernel, rev=rev, finish=finish, dh=dh), grid=(b // nb, nc),
                args=args, in_specs=in_specs, out_specs=out_specs, out_shape=out_shape,
                scratch=pltpu.VMEM((nb, w, w), F32))


def _recurrence_call(hg, rw, name):
    n_hi, n_ri = len(hg["args"]), len(rw["args"])
    n_ho, n_ro = len(hg["out_shape"]), len(rw["out_shape"])
    assert hg["grid"] == rw["grid"]

    def body(*refs):
        ins, outs, (hg_st, rw_st) = refs[:n_hi + n_ri], refs[n_hi + n_ri:-2], refs[-2:]
        hg["kernel"](*ins[:n_hi], *outs[:n_ho], hg_st)
        rw["kernel"](*ins[n_hi:], *outs[n_ho:], rw_st)

    outs = pl.pallas_call(
        body,
        grid=hg["grid"],
        in_specs=hg["in_specs"] + rw["in_specs"],
        out_specs=hg["out_specs"] + rw["out_specs"],
        out_shape=hg["out_shape"] + rw["out_shape"],
        scratch_shapes=[hg["scratch"], rw["scratch"]],
        compiler_params=_cparams(("parallel", "arbitrary")),
        name=name,
    )(*hg["args"], *rw["args"])
    return outs[:n_ho], outs[n_ho:n_ho + n_ro]


def _attn_lam(lam_ref, lam_init):
    lp = lam_ref[...]
    return (jnp.exp(jnp.sum(lp[0:1] * lp[1:2], axis=-1, keepdims=True))
            - jnp.exp(jnp.sum(lp[2:3] * lp[3:4], axis=-1, keepdims=True)) + lam_init)


def _stack_maps(q):
    half = q.shape[-1] // 2
    lane = _iota((1, q.shape[-1]), 1)
    return jnp.concatenate([jnp.where(lane < half, q, 0), jnp.where(lane >= half, q, 0)], axis=0)


def _attn_finish(o, ng, lam_init):
    ms = jnp.mean(o * o, axis=-1, keepdims=True)
    return o * lax.rsqrt(ms + RMS_EPS) * ng * (1.0 - lam_init)


def _attn_kernel(lam_ref, q_ref, k_ref, v_ref, ng_ref, o_ref, *, lam_init, n_ctx, n_ctx_tiles):
    lam = _attn_lam(lam_ref, lam_init)

    def attend(k, v):
        ts = ATT_SUB
        subs = range(q_ref.shape[0] // ts)
        s = [_dot(_stack_maps(q_ref[i * ts:(i + 1) * ts, :]), k) for i in subs]
        for i, x in zip(subs, s):
            p = jnp.exp2(x - jnp.max(x, axis=-1, keepdims=True))
            den = jnp.sum(p, axis=-1, keepdims=True)
            pc = p[0:ts] - p[ts:2 * ts] * (lam * den[0:ts] / den[ts:2 * ts])
            o = _attn_finish(_dot(pc.astype(BF16), v) / den[0:ts], ng_ref[...], lam_init)
            o_ref[i * ts:(i + 1) * ts, :] = o.astype(o_ref.dtype)

    is_ctx = pl.program_id(2) < n_ctx_tiles

    @pl.when(is_ctx)
    def _():
        attend(k_ref[:, 0:n_ctx], v_ref[0:n_ctx, :])

    @pl.when(jnp.logical_not(is_ctx))
    def _():
        attend(k_ref[...], v_ref[...])


def _attention(u_da, u_kt, lam_rows, norm_g_row, lam_init, n_ctx):
    b, n, w2 = u_da.shape
    wd = w2 // 2
    dv = wd // N_HEADS
    tq = ROW_TILE
    kern = functools.partial(_attn_kernel, lam_init=lam_init, n_ctx=n_ctx, n_ctx_tiles=n_ctx // tq)
    return pl.pallas_call(
        kern,
        grid=(b, N_HEADS, n // tq),
        in_specs=[pl.BlockSpec(lam_rows.shape, lambda bi, h, i: (0, 0)),
                  pl.BlockSpec((None, tq, dv), lambda bi, h, i: (bi, i, h)),
                  pl.BlockSpec((None, dv, n), lambda bi, h, i: (bi, h, 0)),
                  pl.BlockSpec((None, n, dv), lambda bi, h, i: (bi, 0, N_HEADS + h)),
                  pl.BlockSpec((1, dv), lambda bi, h, i: (0, 0))],
        out_specs=pl.BlockSpec((None, tq, dv), lambda bi, h, i: (bi, i, h)),
        out_shape=jax.ShapeDtypeStruct((b, n, wd), BF16),
        compiler_params=_cparams(("parallel", "parallel", "arbitrary")),
        name="diff_attn",
    )(lam_rows, u_da, u_kt, u_da, norm_g_row)


def _mlp_kernel(*refs, final):
    if final:
        (s_ref, hg_ref, da_ref, rw_ref, mod_ref, g_ref, whg_ref, wda_ref, wrw_ref, wg_ref, wu_ref, wd_ref,
         fg_ref, o_ref) = refs
    else:
        (s_ref, hg_ref, da_ref, rw_ref, mod_ref, g_ref, whg_ref, wda_ref, wrw_ref, wg_ref, wu_ref, wd_ref,
         o_ref) = refs
    nbp, tm, d = s_ref.shape
    stack = lambda ref: ref[...].reshape(nbp * tm, ref.shape[-1])
    mix = _dot(stack(hg_ref), whg_ref[...]) + _dot(stack(da_ref), wda_ref[...]) + _dot(stack(rw_ref), wrw_ref[...])
    xs, hs = [], []
    for p in range(nbp):
        x = s_ref[p] + mod_ref[p, 2:3, :] * mix[p * tm:(p + 1) * tm]
        ms = jnp.mean(x * x, axis=-1, keepdims=True)
        y = x * lax.rsqrt(ms + RMS_EPS) * g_ref[...]
        xs.append(x)
        hs.append((y * (1.0 + mod_ref[p, 4:5, :]) + mod_ref[p, 3:4, :]).astype(BF16))
    h = jnp.concatenate(hs, axis=0)
    gate = _dot(h, wg_ref[...])
    up = _dot(h, wu_ref[...])
    down = _dot((gate * _sigmoid(gate) * up).astype(BF16), wd_ref[...])
    for p in range(nbp):
        x = xs[p] + mod_ref[p, 5:6, :] * down[p * tm:(p + 1) * tm]
        if final:
            ms = jnp.mean(x * x, axis=-1, keepdims=True)
            x = x * lax.rsqrt(ms + RMS_EPS) * fg_ref[...]
        o_ref[p] = x


def _mlp(s, o_hg, o_da, o_rw, mods_l, g, w_out_parts, wg, wu, wd, n_ctx_tiles, final_g=None):
    b, n, d = s.shape
    tm = ROW_TILE
    final = final_g is not None
    off = n_ctx_tiles if final else 0
    nbp = _dense_batch(b)
    resident = lambda a: pl.BlockSpec(a.shape, lambda bi, i: (0, 0), pipeline_mode=pl.Buffered(1))
    row = lambda a: pl.BlockSpec((nbp, tm, a.shape[-1]), lambda bi, i: (bi, i + off, 0))
    weights = [*w_out_parts, wg, wu, wd]
    in_specs = [row(s), row(o_hg), row(o_da), row(o_rw), _mod_spec(b, d, n_ctx_tiles, off),
                pl.BlockSpec(g.shape, lambda bi, i: (0, 0))] + [resident(a) for a in weights]
    args = [s, o_hg, o_da, o_rw, mods_l, g, *weights]
    if final:
        in_specs.append(pl.BlockSpec((1, d), lambda bi, i: (0, 0)))
        args.append(final_g)
    return pl.pallas_call(
        functools.partial(_mlp_kernel, final=final),
        grid=(b // nbp, n // tm - off),
        in_specs=in_specs,
        out_specs=pl.BlockSpec((nbp, tm, d), lambda bi, i: (bi, i, 0)),
        out_shape=jax.ShapeDtypeStruct((b, n - off * tm, d), F32),
        compiler_params=_cparams(("parallel", "parallel"), VMEM_DENSE_MIB),
        name="mlp",
    )(*args)


def _rope_tables(n_ctx, n_lat, dqk):
    n_freq = dqk // 4
    t = np.arange(n_lat)
    inv_freq = ROPE_BASE ** (-np.arange(n_freq, dtype=np.float32) / n_freq)
    ang = np.concatenate([(t // GRID_W)[:, None] * inv_freq, (t % GRID_W)[:, None] * inv_freq], axis=-1)
    reps = LANE // dqk
    cos = np.tile(np.concatenate([np.cos(ang), np.cos(ang)], axis=-1), (1, reps))
    sin = np.tile(np.concatenate([-np.sin(ang), np.sin(ang)], axis=-1), (1, reps))
    cos = np.concatenate([np.ones((n_ctx, LANE)), cos], axis=0)
    sin = np.concatenate([np.zeros((n_ctx, LANE)), sin], axis=0)
    return jnp.asarray(cos, F32), jnp.asarray(sin, F32)


def _pad_rows(a, start, total):
    return jnp.zeros((total, a.shape[-1]), a.dtype).at[start:start + a.shape[0]].set(a)


def kernel(x, c, ctx, c_ctx, ada_w, ada_b, norm1_g, norm2_g, w_in, w_out, hg_lb_logits, hg_norm_g, da_lam_q1, da_lam_k1, da_lam_q2, da_lam_k2, da_norm_g, rw_mu_prev, rw_mu_next, rw_w0, rw_w2, rw_a0, rw_a2, rw_g2, rw_k_k, rw_k_a, rw_r_k, rw_ln_w, rw_ln_b, ffn_w_gate, ffn_w_up, ffn_w_down, final_norm_g):
    b, n_lat, d = x.shape
    n_ctx = ctx.shape[1]
    depth = ada_w.shape[0]
    hg_w = hg_lb_logits.shape[-1]
    rw_w = rw_w0.shape[-1]
    da_w = w_out.shape[1] - hg_w - rw_w
    hg_cols, da_cols = 5 * hg_w, 3 * da_w
    rw_cols = w_in.shape[-1] - hg_cols - da_cols
    rw_pad = 3 * rw_w + 2 * LANE
    r_dec, r_icl, r_gate = rw_w2.shape[2], rw_a2.shape[2], rw_g2.shape[1]
    assert n_ctx % ROW_TILE == 0 and n_lat % ROW_TILE == 0 and n_lat % GRID_W == 0
    assert 2 * (r_dec + r_icl) == LANE and r_gate <= LANE and rw_cols == 3 * rw_w + LANE + r_gate
    n_ctx_tiles, n_ctx_chunks = n_ctx // ROW_TILE, n_ctx // CHUNK

    n_cond = -(-(b + _dense_batch(b)) // 8) * 8
    cond = jnp.zeros((n_cond, d), F32).at[:b].set(c).at[b:b + _dense_batch(b)].set(c_ctx)
    mods = _ada_mod(cond, ada_w, ada_b).reshape(depth, n_cond, 6, d)
    mods = jnp.pad(mods, ((0, 0), (0, 0), (0, 2), (0, 0)))

    p_lb = jax.nn.softmax(hg_lb_logits.astype(F32), axis=1)
    lower_bounds = jnp.cumsum(p_lb, axis=1) - p_lb[:, :1]
    cos_t, sin_t = _rope_tables(n_ctx, n_lat, da_w // (2 * N_HEADS))

    s = jnp.concatenate([ctx, x], axis=1)
    for l in range(depth):
        w_l = w_in[l]
        whg = w_l[:, :hg_cols].astype(BF16)
        wda = w_l[:, hg_cols:hg_cols + da_cols].astype(BF16)
        wrw = jnp.pad(w_l[:, hg_cols + da_cols:], ((0, 0), (0, rw_pad - rw_cols))).astype(BF16)
        mu_pad = lambda m: jnp.pad(m, (0, rw_pad - rw_cols))[None]
        u_hg, u_da, u_kt, u_rw = _in_proj(s, mods[l], norm1_g[l][None], cos_t, sin_t, mu_pad(rw_mu_prev[l]),
                                    mu_pad(rw_mu_next[l]), whg, wda, wrw, n_ctx_tiles)

        lam_init = 0.8 - 0.6 * math.exp(-0.3 * l)
        lam_rows = jnp.zeros((8, LANE), F32).at[0:4, :da_lam_q1.shape[-1]].set(
            jnp.stack([da_lam_q1[l], da_lam_k1[l], da_lam_q2[l], da_lam_k2[l]]))
        o_da = _attention(u_da, u_kt, lam_rows, da_norm_g[l][None], lam_init, n_ctx)

        off_dec, off_icl = 0, 2 * r_dec
        rw_p = {
            "w0": rw_w0[:, l][:, None, :], "a0": rw_a0[:, l][:, None, :],
            "w2": jnp.stack([_pad_rows(rw_w2[i, l], off_dec + i * r_dec, LANE) for i in range(2)]),
            "a2": jnp.stack([_pad_rows(rw_a2[i, l], off_icl + i * r_icl, LANE) for i in range(2)]),
            "g2": _pad_rows(rw_g2[l], 0, LANE),
            "k_k": rw_k_k[l][None], "k_a": rw_k_a[l][None], "r_k": rw_r_k[l][None],
            "ln_w": rw_ln_w[l][None], "ln_b": rw_ln_b[l][None],
        }
        (o_b,), yb = _recurrence_call(_hgrn_pass(u_hg, lower_bounds[1, l][None], n_ctx_chunks, True),
                                      _rwkv_pass(u_rw, rw_p, n_ctx_chunks, True), "rec_bwd")
        (o_hg,), (o_rw,) = _recurrence_call(
            _hgrn_pass(u_hg, lower_bounds[0, l][None], n_ctx_chunks, False, other=o_b,
                       norm_g_row=jnp.tile(hg_norm_g[l], N_HEADS)[None]),
            _rwkv_pass(u_rw, rw_p, n_ctx_chunks, False, other=yb), "rec_fwd")

        wo = w_out[l].astype(BF16)
        s = _mlp(s, o_hg, o_da, o_rw, mods[l], norm2_g[l][None],
                 (wo[:hg_w], wo[hg_w:hg_w + da_w], wo[hg_w + da_w:]),
                 ffn_w_gate[l].astype(BF16), ffn_w_up[l].astype(BF16), ffn_w_down[l].astype(BF16),
                 n_ctx_tiles, final_g=final_norm_g[None] if l == depth - 1 else None)
    return s
```
